```python
import jax
import jax.numpy as jnp
from jax import lax
import numpy as np

D_MODEL = 4096
BATCH = 16
SEQ = 2048
DEPTH = 1

HEAD_DIM = 128
N_HEADS_TOTAL = D_MODEL // HEAD_DIM
N_HEADS_DIL = N_HEADS_TOTAL // 2
N_HEADS_SB = N_HEADS_TOTAL - N_HEADS_DIL
DIL_WIDTH = N_HEADS_DIL * HEAD_DIM
SB_WIDTH = N_HEADS_SB * HEAD_DIM
MIX_WIDTH = DIL_WIDTH + SB_WIDTH
IN_PROJ_WIDTH = 3 * DIL_WIDTH + 3 * SB_WIDTH
DILATED_PAIRS = ((128, 1), (512, 4), (2048, 16))
BLOCK = 128
ROPE_THETA = 500000.0
ROPE_DIM = HEAD_DIM // 4
D_FF = -(-8 * D_MODEL // (3 * 256)) * 256
RMS_EPS = 1e-5

kernel_name = "hymba_dilated_stickbreaking_swiglu"


def rmsnorm(x, g):
    xf = x.astype(jnp.float32)
    y = xf * lax.rsqrt(jnp.mean(xf * xf, axis=-1, keepdims=True) + RMS_EPS)
    return (y * g.astype(jnp.float32)).astype(x.dtype)


def apply_partial_rope(a, positions):
    half = ROPE_DIM // 2
    inv_freq = jnp.power(jnp.float32(ROPE_THETA), -jnp.arange(half, dtype=jnp.float32) / half)
    ang = positions.astype(jnp.float32)[..., None] * inv_freq
    cos = jnp.cos(ang)[:, :, None, :]
    sin = jnp.sin(ang)[:, :, None, :]
    a1 = a[..., :half].astype(jnp.float32)
    a2 = a[..., half:ROPE_DIM].astype(jnp.float32)
    rot = jnp.concatenate([a1 * cos - a2 * sin, a2 * cos + a1 * sin], axis=-1).astype(a.dtype)
    return jnp.concatenate([rot, a[..., ROPE_DIM:]], axis=-1)


def dilated_branch(q, k, v, window, dilation):
    B, T, H, Dh = q.shape
    steps = window // dilation
    L = T // dilation
    nb = -(-L // BLOCK)
    Lp = nb * BLOCK
    pad_tail = ((0, 0), (0, Lp - L), (0, 0), (0, 0), (0, 0))

    def to_sub(a):
        return jnp.pad(a.reshape(B, L, dilation, H, Dh), pad_tail)

    def with_prev(a):
        a = jnp.pad(a, ((0, 0), (BLOCK, 0), (0, 0), (0, 0), (0, 0)))
        a = a.reshape(B, nb + 1, BLOCK, dilation, H, Dh)
        return jnp.concatenate([a[:, :-1], a[:, 1:]], axis=2)

    qb = to_sub(q).reshape(B, nb, BLOCK, dilation, H, Dh)
    kc = with_prev(to_sub(k))
    vc = with_prev(to_sub(v))
    s = jnp.einsum('bnqrhd,bnkrhd->bnrhqk', qb, kc).astype(jnp.float32) * (Dh ** -0.5)
    qi = jnp.arange(BLOCK)[:, None] + BLOCK
    ki = jnp.arange(2 * BLOCK)[None, :]
    dist = qi - ki
    k_glob = jnp.arange(nb)[:, None, None] * BLOCK + ki[None] - BLOCK
    mask = ((dist >= 0) & (dist <= steps))[None] & (k_glob >= 0)
    s = jnp.where(mask[None, :, None, None], s, -jnp.inf)
    m = jnp.max(s, axis=-1, keepdims=True)
    p = jnp.exp(s - m)
    den = jnp.sum(p, axis=-1, keepdims=True)
    o = jnp.einsum('bnrhqk,bnkrhd->bnqrhd', p / den, vc.astype(jnp.float32))
    o = o.reshape(B, Lp, dilation, H, Dh)[:, :L].reshape(B, T, H, Dh)
    lse = (m + jnp.log(den))[..., 0]
    lse = lse.transpose(0, 1, 4, 2, 3).reshape(B, Lp, dilation, H)[:, :L].reshape(B, T, H)
    return o, lse


def dilated_attention(q, k, v):
    outs, lses = [], []
    for window, dilation in DILATED_PAIRS:
        o, lse = dilated_branch(q, k, v, window, dilation)
        outs.append(o)
        lses.append(lse)
    alpha = jax.nn.softmax(jnp.stack(lses, axis=0), axis=0)
    return jnp.sum(alpha[..., None] * jnp.stack(outs, axis=0), axis=0)


def stick_breaking_attention(q, k, v):
    B, T, H, Dh = q.shape
    outs = []
    for i in range(T // BLOCK):
        end = (i + 1) * BLOCK
        qi = q[:, i * BLOCK:end]
        z = jnp.einsum('bqhd,bkhd->bhqk', qi, k[:, :end]).astype(jnp.float32) * (Dh ** -0.5)
        q_pos = i * BLOCK + jnp.arange(BLOCK)[:, None]
        k_pos = jnp.arange(end)[None, :]
        strict = k_pos < q_pos
        log_not = jnp.where(strict, -jax.nn.softplus(z), 0.0)
        excl = lax.cumsum(log_not, axis=3, reverse=True) - log_not
        a = jnp.where(strict, jnp.exp(jax.nn.log_sigmoid(z) + excl), 0.0)
        outs.append(jnp.einsum('bhqk,bkhd->bqhd', a, v[:, :end].astype(jnp.float32)))
    return jnp.concatenate(outs, axis=1)


def head_rmsnorm(o, g, n_heads):
    of = o.astype(jnp.float32)
    y = of * lax.rsqrt(jnp.mean(of * of, axis=-1, keepdims=True) + RMS_EPS)
    y = y * g.astype(jnp.float32).reshape(n_heads, HEAD_DIM)
    return y.reshape(o.shape[0], o.shape[1], n_heads * HEAD_DIM)


def _fwd_setup_inputs(seed: int = 0) -> dict:
    key = jax.random.key(seed)
    ks = jax.random.split(key, 12)

    def normal(k, shape, scale):
        return jax.random.normal(k, shape, jnp.float32) * scale

    x = normal(ks[0], (BATCH, SEQ, D_MODEL), 1.0)
    offsets = jax.random.randint(ks[1], (BATCH, 1), 0, 4096, dtype=jnp.int32)
    positions = offsets + jnp.arange(SEQ, dtype=jnp.int32)[None, :]
    norm_mix_g = 1.0 + normal(ks[2], (DEPTH, D_MODEL), 0.02)
    w_in = normal(ks[3], (DEPTH, D_MODEL, IN_PROJ_WIDTH), D_MODEL ** -0.5)
    norm_out_dil_g = 1.0 + normal(ks[4], (DEPTH, DIL_WIDTH), 0.02)
    norm_out_sb_g = 1.0 + normal(ks[5], (DEPTH, SB_WIDTH), 0.02)
    w_out = normal(ks[6], (DEPTH, MIX_WIDTH, D_MODEL), MIX_WIDTH ** -0.5)
    norm_ffn_g = 1.0 + normal(ks[7], (DEPTH, D_MODEL), 0.02)
    w_gate = normal(ks[8], (DEPTH, D_MODEL, D_FF), D_MODEL ** -0.5)
    w_up = normal(ks[9], (DEPTH, D_MODEL, D_FF), D_MODEL ** -0.5)
    w_down = normal(ks[10], (DEPTH, D_FF, D_MODEL), D_FF ** -0.5)
    norm_final_g = 1.0 + normal(ks[11], (D_MODEL,), 0.02)
    return {"x": x, "positions": positions, "norm_mix_g": norm_mix_g, "w_in": w_in,
            "norm_out_dil_g": norm_out_dil_g, "norm_out_sb_g": norm_out_sb_g, "w_out": w_out,
            "norm_ffn_g": norm_ffn_g, "w_gate": w_gate, "w_up": w_up, "w_down": w_down,
            "norm_final_g": norm_final_g}


def _fwd_reference(x, positions, norm_mix_g, w_in, norm_out_dil_g, norm_out_sb_g, w_out,
              norm_ffn_g, w_gate, w_up, w_down, norm_final_g):
    B, T, _ = x.shape
    h = x
    for layer in range(DEPTH):
        hn = rmsnorm(h, norm_mix_g[layer])
        proj = jnp.einsum('btd,de->bte', hn, w_in[layer])
        qa, ka, va, qb, kb, vb = jnp.split(
            proj, [DIL_WIDTH, 2 * DIL_WIDTH, 3 * DIL_WIDTH,
                   3 * DIL_WIDTH + SB_WIDTH, 3 * DIL_WIDTH + 2 * SB_WIDTH], axis=-1)
        qa = apply_partial_rope(qa.reshape(B, T, N_HEADS_DIL, HEAD_DIM), positions)
        ka = apply_partial_rope(ka.reshape(B, T, N_HEADS_DIL, HEAD_DIM), positions)
        va = va.reshape(B, T, N_HEADS_DIL, HEAD_DIM)
        qb = qb.reshape(B, T, N_HEADS_SB, HEAD_DIM)
        kb = kb.reshape(B, T, N_HEADS_SB, HEAD_DIM)
        vb = vb.reshape(B, T, N_HEADS_SB, HEAD_DIM)
        o_dil = head_rmsnorm(dilated_attention(qa, ka, va), norm_out_dil_g[layer], N_HEADS_DIL)
        o_sb = head_rmsnorm(stick_breaking_attention(qb, kb, vb), norm_out_sb_g[layer], N_HEADS_SB)
        o_mix = jnp.concatenate([o_dil, o_sb], axis=-1).astype(h.dtype)
        h = h + jnp.einsum('bte,ed->btd', o_mix, w_out[layer])
        hn = rmsnorm(h, norm_ffn_g[layer])
        gate = jnp.einsum('btd,df->btf', hn, w_gate[layer])
        up = jnp.einsum('btd,df->btf', hn, w_up[layer])
        h = h + jnp.einsum('btf,fd->btd', jax.nn.silu(gate) * up, w_down[layer])
    return rmsnorm(h, norm_final_g)


import jax as _jax
import jax.numpy as _jnp

TWIN_FORMAT = 'train_step'
FWD_PARAMS = ['x', 'positions', 'norm_mix_g', 'w_in', 'norm_out_dil_g', 'norm_out_sb_g', 'w_out', 'norm_ffn_g', 'w_gate', 'w_up', 'w_down', 'norm_final_g']
TWIN_WEIGHTS = ['norm_mix_g', 'w_in', 'norm_out_dil_g', 'norm_out_sb_g', 'w_out', 'norm_ffn_g', 'w_gate', 'w_up', 'w_down', 'norm_final_g']
TWIN_DIFF_INPUT = 'x'
TWIN_INPUTS = ['x', 'positions', 'norm_mix_g', 'w_in', 'norm_out_dil_g', 'norm_out_sb_g', 'w_out', 'norm_ffn_g', 'w_gate', 'w_up', 'w_down', 'norm_final_g', 'loss_target', 'm_norm_mix_g', 'm_w_in', 'm_norm_out_dil_g', 'm_norm_out_sb_g', 'm_w_out', 'm_norm_ffn_g', 'm_w_gate', 'm_w_up', 'm_w_down', 'm_norm_final_g', 'v_norm_mix_g', 'v_w_in', 'v_norm_out_dil_g', 'v_norm_out_sb_g', 'v_w_out', 'v_norm_ffn_g', 'v_w_gate', 'v_w_up', 'v_w_down', 'v_norm_final_g']
TWIN_OUTPUTS = ['loss', 'grad_x', 'grad_norm_mix_g', 'grad_w_in', 'grad_norm_out_dil_g', 'grad_norm_out_sb_g', 'grad_w_out', 'grad_norm_ffn_g', 'grad_w_gate', 'grad_w_up', 'grad_w_down', 'grad_norm_final_g', 'delta_norm_mix_g', 'delta_w_in', 'delta_norm_out_dil_g', 'delta_norm_out_sb_g', 'delta_w_out', 'delta_norm_ffn_g', 'delta_w_gate', 'delta_w_up', 'delta_w_down', 'delta_norm_final_g', 'new_m_norm_mix_g', 'new_m_w_in', 'new_m_norm_out_dil_g', 'new_m_norm_out_sb_g', 'new_m_w_out', 'new_m_norm_ffn_g', 'new_m_w_gate', 'new_m_w_up', 'new_m_w_down', 'new_m_norm_final_g', 'new_v_norm_mix_g', 'new_v_w_in', 'new_v_norm_out_dil_g', 'new_v_norm_out_sb_g', 'new_v_w_out', 'new_v_norm_ffn_g', 'new_v_w_gate', 'new_v_w_up', 'new_v_w_down', 'new_v_norm_final_g']
TWIN_LEAF_KINDS = {'loss': 'loss', 'grad_x': 'grad_x', 'grad_norm_mix_g': 'grad_w', 'grad_w_in': 'grad_w', 'grad_norm_out_dil_g': 'grad_w', 'grad_norm_out_sb_g': 'grad_w', 'grad_w_out': 'grad_w', 'grad_norm_ffn_g': 'grad_w', 'grad_w_gate': 'grad_w', 'grad_w_up': 'grad_w', 'grad_w_down': 'grad_w', 'grad_norm_final_g': 'grad_w', 'delta_norm_mix_g': 'delta_w', 'delta_w_in': 'delta_w', 'delta_norm_out_dil_g': 'delta_w', 'delta_norm_out_sb_g': 'delta_w', 'delta_w_out': 'delta_w', 'delta_norm_ffn_g': 'delta_w', 'delta_w_gate': 'delta_w', 'delta_w_up': 'delta_w', 'delta_w_down': 'delta_w', 'delta_norm_final_g': 'delta_w', 'new_m_norm_mix_g': 'new_m', 'new_m_w_in': 'new_m', 'new_m_norm_out_dil_g': 'new_m', 'new_m_norm_out_sb_g': 'new_m', 'new_m_w_out': 'new_m', 'new_m_norm_ffn_g': 'new_m', 'new_m_w_gate': 'new_m', 'new_m_w_up': 'new_m', 'new_m_w_down': 'new_m', 'new_m_norm_final_g': 'new_m', 'new_v_norm_mix_g': 'new_v', 'new_v_w_in': 'new_v', 'new_v_norm_out_dil_g': 'new_v', 'new_v_norm_out_sb_g': 'new_v', 'new_v_w_out': 'new_v', 'new_v_norm_ffn_g': 'new_v', 'new_v_w_gate': 'new_v', 'new_v_w_up': 'new_v', 'new_v_w_down': 'new_v', 'new_v_norm_final_g': 'new_v'}


def _forward(args):
    return _fwd_reference(*[args[k] for k in FWD_PARAMS])


def _output_shape():
    def fwd():
        inp = _fwd_setup_inputs(0)
        return _fwd_reference(*[inp[k] for k in FWD_PARAMS])
    out = _jax.eval_shape(fwd)
    return out.shape, out.dtype

N_MICROBATCH = 1
ADAM_LR = 0.001
ADAM_B1 = 0.9
ADAM_B2 = 0.999
ADAM_EPS = 1e-08
ADAM_WD = 0.01
ADAM_STEP = 10
PER_EXAMPLE_BATCH_AXIS = {'x': 0, 'positions': 0, 'loss_target': 0}
SHARED_INPUTS = []
_WEIGHT_DTYPES = {'norm_mix_g': _jnp.float32, 'w_in': _jnp.float32, 'norm_out_dil_g': _jnp.float32, 'norm_out_sb_g': _jnp.float32, 'w_out': _jnp.float32, 'norm_ffn_g': _jnp.float32, 'w_gate': _jnp.float32, 'w_up': _jnp.float32, 'w_down': _jnp.float32, 'norm_final_g': _jnp.float32}
MOMENT_SCALE = {'norm_mix_g': 5.044686e-02, 'w_in': 2.877741e-02, 'norm_out_dil_g': 3.446406e-02, 'norm_out_sb_g': 3.366935e-02, 'w_out': 3.380331e-02, 'norm_ffn_g': 2.530315e-02, 'w_gate': 1.083995e-02, 'w_up': 1.049881e-02, 'w_down': 1.720079e-02, 'norm_final_g': 7.996303e+00}


def _to_microbatches(a, axis):
    t = _jnp.moveaxis(a, axis, 0)
    t = t.reshape((N_MICROBATCH, t.shape[0] // N_MICROBATCH) + t.shape[1:])
    return _jnp.moveaxis(t, 1, axis + 1)


def setup_inputs(seed: int = 0) -> dict:
    inp = _fwd_setup_inputs(seed)
    key = _jax.random.fold_in(_jax.random.key(seed), 7919)
    shape, _ = _output_shape()
    out = dict(inp)
    out["loss_target"] = _jax.random.normal(_jax.random.fold_in(key, 0), shape, _jnp.float32)
    for i, name in enumerate(TWIN_WEIGHTS):
        w = inp[name].astype(_jnp.float32)
        if MOMENT_SCALE is None:
            s = _jnp.sqrt(_jnp.mean(_jnp.square(w)) + 1e-30)
        else:
            s = MOMENT_SCALE[name]
        km, kv = _jax.random.split(_jax.random.fold_in(key, i + 1))
        out[name] = w
        out["m_" + name] = s * _jax.random.normal(km, w.shape, _jnp.float32)
        out["v_" + name] = (s * s) * _jax.random.uniform(kv, w.shape, _jnp.float32, 0.5, 1.5)
    if N_MICROBATCH > 1:
        for name, axis in PER_EXAMPLE_BATCH_AXIS.items():
            out[name] = _to_microbatches(out[name], axis)
    return {'x': out['x'], 'positions': out['positions'], 'norm_mix_g': out['norm_mix_g'], 'w_in': out['w_in'], 'norm_out_dil_g': out['norm_out_dil_g'], 'norm_out_sb_g': out['norm_out_sb_g'], 'w_out': out['w_out'], 'norm_ffn_g': out['norm_ffn_g'], 'w_gate': out['w_gate'], 'w_up': out['w_up'], 'w_down': out['w_down'], 'norm_final_g': out['norm_final_g'], 'loss_target': out['loss_target'], 'm_norm_mix_g': out['m_norm_mix_g'], 'm_w_in': out['m_w_in'], 'm_norm_out_dil_g': out['m_norm_out_dil_g'], 'm_norm_out_sb_g': out['m_norm_out_sb_g'], 'm_w_out': out['m_w_out'], 'm_norm_ffn_g': out['m_norm_ffn_g'], 'm_w_gate': out['m_w_gate'], 'm_w_up': out['m_w_up'], 'm_w_down': out['m_w_down'], 'm_norm_final_g': out['m_norm_final_g'], 'v_norm_mix_g': out['v_norm_mix_g'], 'v_w_in': out['v_w_in'], 'v_norm_out_dil_g': out['v_norm_out_dil_g'], 'v_norm_out_sb_g': out['v_norm_out_sb_g'], 'v_w_out': out['v_w_out'], 'v_norm_ffn_g': out['v_norm_ffn_g'], 'v_w_gate': out['v_w_gate'], 'v_w_up': out['v_w_up'], 'v_w_down': out['v_w_down'], 'v_norm_final_g': out['v_norm_final_g']}


def _loss(weights, diff, rest, loss_target):
    with _jax.named_scope("forward"):
        args = {**rest, TWIN_DIFF_INPUT: diff, **{k: w.astype(_WEIGHT_DTYPES[k]) for k, w in weights.items()}}
        y = _forward(args)
    with _jax.named_scope("loss_head"):
        err = _jnp.square(y.astype(_jnp.float32) - loss_target)
        return 0.5 * _jnp.sum(_jnp.mean(err, axis=-1)) if err.ndim else 0.5 * err


def _adamw(w, g, m, v):
    m = ADAM_B1 * m + (1.0 - ADAM_B1) * g
    v = ADAM_B2 * v + (1.0 - ADAM_B2) * _jnp.square(g)
    m_hat = m / (1.0 - ADAM_B1 ** ADAM_STEP)
    v_hat = v / (1.0 - ADAM_B2 ** ADAM_STEP)
    delta = -ADAM_LR * (m_hat / (_jnp.sqrt(v_hat) + ADAM_EPS) + ADAM_WD * w)
    return delta, m, v


def reference(x, positions, norm_mix_g, w_in, norm_out_dil_g, norm_out_sb_g, w_out, norm_ffn_g, w_gate, w_up, w_down, norm_final_g, loss_target, m_norm_mix_g, m_w_in, m_norm_out_dil_g, m_norm_out_sb_g, m_w_out, m_norm_ffn_g, m_w_gate, m_w_up, m_w_down, m_norm_final_g, v_norm_mix_g, v_w_in, v_norm_out_dil_g, v_norm_out_sb_g, v_w_out, v_norm_ffn_g, v_w_gate, v_w_up, v_w_down, v_norm_final_g):
    given = dict(x=x, positions=positions, norm_mix_g=norm_mix_g, w_in=w_in, norm_out_dil_g=norm_out_dil_g, norm_out_sb_g=norm_out_sb_g, w_out=w_out, norm_ffn_g=norm_ffn_g, w_gate=w_gate, w_up=w_up, w_down=w_down, norm_final_g=norm_final_g, loss_target=loss_target, m_norm_mix_g=m_norm_mix_g, m_w_in=m_w_in, m_norm_out_dil_g=m_norm_out_dil_g, m_norm_out_sb_g=m_norm_out_sb_g, m_w_out=m_w_out, m_norm_ffn_g=m_norm_ffn_g, m_w_gate=m_w_gate, m_w_up=m_w_up, m_w_down=m_w_down, m_norm_final_g=m_norm_final_g, v_norm_mix_g=v_norm_mix_g, v_w_in=v_w_in, v_norm_out_dil_g=v_norm_out_dil_g, v_norm_out_sb_g=v_norm_out_sb_g, v_w_out=v_w_out, v_norm_ffn_g=v_norm_ffn_g, v_w_gate=v_w_gate, v_w_up=v_w_up, v_w_down=v_w_down, v_norm_final_g=v_norm_final_g)
    weights = {n: given[n] for n in TWIN_WEIGHTS}
    shared = {n: given[n] for n in SHARED_INPUTS}
    per_example = {n: given[n] for n in ['x', 'positions']}
    grad_fn = _jax.value_and_grad(_loss, argnums=(0, 1))

    def one_microbatch(ex, loss_target):
        ex = dict(ex)
        diff = ex.pop(TWIN_DIFF_INPUT)
        return grad_fn(weights, diff, {**shared, **ex}, loss_target)

    if N_MICROBATCH == 1:
        loss, (grad_w, grad_x) = one_microbatch(per_example, given["loss_target"])
    else:
        def body(carry, xs):
            loss_sum, grad_sum = carry
            l_k, (gw_k, gx_k) = one_microbatch(xs[0], xs[1])
            with _jax.named_scope("update"):
                return (loss_sum + l_k, _jax.tree.map(_jnp.add, grad_sum, gw_k)), gx_k

        init = (_jnp.zeros((), _jnp.float32), _jax.tree.map(_jnp.zeros_like, weights))
        (loss, grad_w), grad_x = _jax.lax.scan(body, init, (per_example, given["loss_target"]))
    with _jax.named_scope("update"):
        delta_w, new_m, new_v = {}, {}, {}
        for n in TWIN_WEIGHTS:
            delta_w[n], new_m[n], new_v[n] = _adamw(weights[n], grad_w[n], given["m_" + n], given["v_" + n])
    return (loss, grad_x, *[grad_w[n] for n in TWIN_WEIGHTS], *[delta_w[n] for n in TWIN_WEIGHTS],
            *[new_m[n] for n in TWIN_WEIGHTS], *[new_v[n] for n in TWIN_WEIGHTS])
```

```python
import functools
import math

import jax
import jax.numpy as jnp
from jax import lax
from jax.experimental import pallas as pl
from jax.experimental.pallas import tpu as pltpu

F32 = jnp.float32
BF16 = jnp.bfloat16
S = jax.ShapeDtypeStruct

N_DEV = 8
HEAD_DIM = 128
BLK = 128
ROPE_HALF = 16
ROPE_THETA = 500000.0
RMS_EPS = 1e-5
DILATIONS = (1, 4, 16)
NEG = -1e30
VMEM_LIMIT = 56 * 1024 * 1024

ADAM_LR = 0.001
ADAM_B1 = 0.9
ADAM_B2 = 0.999
ADAM_EPS = 1e-08
ADAM_WD = 0.01
ADAM_STEP = 10

MESH = pl.DeviceIdType.MESH
HBM_SPEC = pl.BlockSpec(memory_space=pltpu.HBM)


def _cp(n_axes):
    return pltpu.CompilerParams(dimension_semantics=("arbitrary",) * n_axes, vmem_limit_bytes=VMEM_LIMIT)


def _tile(n, want):
    if n <= want:
        return n
    t = want
    while t >= 16:
        if n % t == 0 and t % 16 == 0:
            return t
        t -= 16
    return n


NN = (((1,), (0,)), ((), ()))
NT = (((1,), (1,)), ((), ()))
TN = (((0,), (0,)), ((), ()))


def _matmul(name, grid, red_axis, ins, in_specs, terms, dims, acc_shapes, out_shapes, out_specs, epilogue):
    n_in, n_out = len(ins), len(out_shapes)
    n_red = grid[red_axis]

    def body(*refs):
        in_refs, out_refs, acc_refs = refs[:n_in], refs[n_in:n_in + n_out], refs[n_in + n_out:]
        k = pl.program_id(red_axis)

        @pl.when(k == 0)
        def _():
            for acc in acc_refs:
                acc[...] = jnp.zeros_like(acc)

        for a_idx, b_idx, acc_idx in terms:
            a = in_refs[a_idx][...].astype(BF16)
            b = in_refs[b_idx][...].astype(BF16)
            acc_refs[acc_idx][...] += lax.dot_general(a, b, dims, preferred_element_type=F32)

        @pl.when(k == n_red - 1)
        def _():
            epilogue(acc_refs, in_refs, out_refs)

    return pl.pallas_call(
        body, name=name, grid=grid, in_specs=in_specs, out_specs=out_specs, out_shape=out_shapes,
        scratch_shapes=[pltpu.VMEM(s, F32) for s in acc_shapes], compiler_params=_cp(len(grid)),
    )(*ins)


def _store_epilogue(acc_refs, in_refs, out_refs):
    for acc, out in zip(acc_refs, out_refs):
        out[...] = acc[...].astype(out.dtype)


def _rope_fwd(a, c, s1, s2):
    return a * c + pltpu.roll(a, ROPE_HALF, 1) * s1 + pltpu.roll(a, HEAD_DIM - ROPE_HALF, 1) * s2


def _rope_bwd(d, c, s1, s2):
    return d * c + pltpu.roll(d * s1, HEAD_DIM - ROPE_HALF, 1) + pltpu.roll(d * s2, ROPE_HALF, 1)


def _proj_fwd(hn, w_all, rc, rs1, rs2, n_rope_heads):
    n, d = hn.shape
    _, _, ws = w_all.shape
    tm, tk = _tile(n, 1024), _tile(d, 512)
    heads_per_shard = ws // HEAD_DIM
    rows = _tile(tm, 256)

    def epilogue(acc_refs, in_refs, out_refs):
        acc, out = acc_refs[0], out_refs[0]
        j = pl.program_id(0)
        for r0 in range(0, tm, rows):
            c, s1, s2 = (ref[pl.ds(r0, rows), :] for ref in in_refs[2:5])
            for hh in range(heads_per_shard):
                a = acc[pl.ds(r0, rows), pl.ds(hh * HEAD_DIM, HEAD_DIM)]
                roped = _rope_fwd(a, c, s1, s2)
                a = jnp.where(j * heads_per_shard + hh < n_rope_heads, roped, a)
                out[pl.ds(r0, rows), pl.ds(hh * HEAD_DIM, HEAD_DIM)] = a.astype(out.dtype)

    tab = pl.BlockSpec((tm, HEAD_DIM), lambda j, m, k: (m, 0))
    return _matmul(
        "proj_fwd", (N_DEV, n // tm, d // tk), 2, (hn, w_all, rc, rs1, rs2),
        [pl.BlockSpec((tm, tk), lambda j, m, k: (m, k)), pl.BlockSpec((None, tk, ws), lambda j, m, k: (j, k, 0)), tab, tab, tab],
        [(0, 1, 0)], NN, [(tm, ws)], [S((n, N_DEV * ws), BF16)], [pl.BlockSpec((tm, ws), lambda j, m, k: (m, j))], epilogue)[0]


def _dense_res(name, a, b, res, dims, out_dtype=F32):
    m, kdim = a.shape
    n = b.shape[1] if dims == NN else b.shape[0]
    tm, tn, tk = _tile(m, 1024), _tile(n, 1024), _tile(kdim, 512)
    ins = [a, b] + ([res] if res is not None else [])
    b_spec = pl.BlockSpec((tk, tn), lambda i, j, k: (k, j)) if dims == NN else pl.BlockSpec((tn, tk), lambda i, j, k: (j, k))
    specs = [pl.BlockSpec((tm, tk), lambda i, j, k: (i, k)), b_spec]
    if res is not None:
        specs.append(pl.BlockSpec((tm, tn), lambda i, j, k: (i, j)))

    def epilogue(acc_refs, in_refs, out_refs):
        v = acc_refs[0][...]
        if res is not None:
            v = v + in_refs[2][...]
        out_refs[0][...] = v.astype(out_dtype)

    return _matmul(name, (m // tm, n // tn, kdim // tk), 2, ins, specs, [(0, 1, 0)], dims, [(tm, tn)],
                   [S((m, n), out_dtype)], [pl.BlockSpec((tm, tn), lambda i, j, k: (i, j))], epilogue)[0]


def _tn_full(name, a, b, out_dtype=BF16):
    m, kdim = a.shape
    n = b.shape[1]
    tm, tk, tn = _tile(m, 512), _tile(kdim, 1024), _tile(n, 1024)
    return _matmul(name, (kdim // tk, n // tn, m // tm), 2, (a, b),
                   [pl.BlockSpec((tm, tk), lambda i, j, t: (t, i)), pl.BlockSpec((tm, tn), lambda i, j, t: (t, j))],
                   [(0, 1, 0)], TN, [(tk, tn)], [S((kdim, n), out_dtype)], [pl.BlockSpec((tk, tn), lambda i, j, t: (i, j))],
                   _store_epilogue)[0]


def _gateup_fwd(hn, wg_all, wu_all):
    n, d = hn.shape
    _, _, fs = wg_all.shape
    tm, tk = _tile(n, 1024), _tile(d, 512)
    rows = _tile(tm, 256)

    def epilogue(acc_refs, in_refs, out_refs):
        for r0 in range(0, tm, rows):
            g = acc_refs[0][pl.ds(r0, rows), :]
            u = acc_refs[1][pl.ds(r0, rows), :]
            out_refs[0][pl.ds(r0, rows), :] = g.astype(BF16)
            out_refs[1][pl.ds(r0, rows), :] = u.astype(BF16)
            out_refs[2][pl.ds(r0, rows), :] = (g * jax.nn.sigmoid(g) * u).astype(BF16)

    w_spec = pl.BlockSpec((None, tk, fs), lambda j, m, k: (j, k, 0))
    o_spec = pl.BlockSpec((None, tm, fs), lambda j, m, k: (j, m, 0))
    o_shape = S((N_DEV, n, fs), BF16)
    return _matmul("gateup_fwd", (N_DEV, n // tm, d // tk), 2, (hn, wg_all, wu_all),
                   [pl.BlockSpec((tm, tk), lambda j, m, k: (m, k)), w_spec, w_spec],
                   [(0, 1, 0), (0, 2, 1)], NN, [(tm, fs), (tm, fs)], [o_shape] * 3, [o_spec] * 3, epilogue)


def _down_fwd(act, wd_all, res):
    _, n, fs = act.shape
    d = wd_all.shape[2]
    tm, tn = _tile(n, 1024), _tile(d, 1024)

    def epilogue(acc_refs, in_refs, out_refs):
        out_refs[0][...] = acc_refs[0][...] + in_refs[2][...]

    return _matmul("down_fwd", (n // tm, d // tn, N_DEV), 2, (act, wd_all, res),
                   [pl.BlockSpec((None, tm, fs), lambda i, j, s: (s, i, 0)), pl.BlockSpec((None, fs, tn), lambda i, j, s: (s, 0, j)),
                    pl.BlockSpec((tm, tn), lambda i, j, s: (i, j))],
                   [(0, 1, 0)], NN, [(tm, tn)], [S((n, d), F32)], [pl.BlockSpec((tm, tn), lambda i, j, s: (i, j))], epilogue)[0]


def _dact_bwd(dh, wd_all, gate, up):
    n, d = dh.shape
    _, fs, _ = wd_all.shape
    tm, tk = _tile(n, 1024), _tile(d, 512)
    rows = _tile(tm, 256)

    def epilogue(acc_refs, in_refs, out_refs):
        for r0 in range(0, tm, rows):
            da = acc_refs[0][pl.ds(r0, rows), :]
            g = in_refs[2][pl.ds(r0, rows), :].astype(F32)
            u = in_refs[3][pl.ds(r0, rows), :].astype(F32)
            sg = jax.nn.sigmoid(g)
            out_refs[0][pl.ds(r0, rows), :] = (da * u * (sg * (1.0 + g * (1.0 - sg)))).astype(BF16)
            out_refs[1][pl.ds(r0, rows), :] = (da * (g * sg)).astype(BF16)

    t_spec = pl.BlockSpec((None, tm, fs), lambda j, m, k: (j, m, 0))
    o_shape = S((N_DEV, n, fs), BF16)
    return _matmul("dact_bwd", (N_DEV, n // tm, d // tk), 2, (dh, wd_all, gate, up),
                   [pl.BlockSpec((tm, tk), lambda j, m, k: (m, k)), pl.BlockSpec((None, fs, tk), lambda j, m, k: (j, 0, k)), t_spec, t_spec],
                   [(0, 1, 0)], NT, [(tm, fs)], [o_shape] * 2, [t_spec] * 2, epilogue)


def _dwd_bwd(act, dh):
    _, n, fs = act.shape
    d = dh.shape[1]
    tm, tn = _tile(n, 512), _tile(d, 1024)
    return _matmul("dwd_bwd", (N_DEV, d // tn, n // tm), 2, (act, dh),
                   [pl.BlockSpec((None, tm, fs), lambda j, c, t: (j, t, 0)), pl.BlockSpec((tm, tn), lambda j, c, t: (t, c))],
                   [(0, 1, 0)], TN, [(fs, tn)], [S((N_DEV, fs, d), BF16)], [pl.BlockSpec((None, fs, tn), lambda j, c, t: (j, 0, c))],
                   _store_epilogue)[0]


def _dw_cols_bwd(name, hn, dys):
    n, d = hn.shape
    ws = dys[0].shape[2]
    tm, tk = _tile(n, 512), _tile(d, 1024)
    k_out = len(dys)
    y_spec = pl.BlockSpec((None, tm, ws), lambda j, c, t: (j, t, 0))
    o_spec = pl.BlockSpec((None, tk, ws), lambda j, c, t: (j, c, 0))
    return _matmul(name, (N_DEV, d // tk, n // tm), 2, (hn, *dys),
                   [pl.BlockSpec((tm, tk), lambda j, c, t: (t, c))] + [y_spec] * k_out,
                   [(0, 1 + i, i) for i in range(k_out)], TN, [(tk, ws)] * k_out, [S((N_DEV, d, ws), BF16)] * k_out, [o_spec] * k_out,
                   _store_epilogue)


def _dwin_bwd(hn, dproj):
    n, d = hn.shape
    ws = dproj.shape[1] // N_DEV
    tm, tk = _tile(n, 512), _tile(d, 1024)
    return _matmul("dwin_bwd", (N_DEV, d // tk, n // tm), 2, (hn, dproj),
                   [pl.BlockSpec((tm, tk), lambda j, c, t: (t, c)), pl.BlockSpec((tm, ws), lambda j, c, t: (t, j))],
                   [(0, 1, 0)], TN, [(tk, ws)], [S((N_DEV, d, ws), BF16)], [pl.BlockSpec((None, tk, ws), lambda j, c, t: (j, c, 0))],
                   _store_epilogue)[0]


def _dhn_from_shards(name, dys, ws_all, dy_is_flat):
    if dy_is_flat:
        n, ws = dys[0].shape[0], dys[0].shape[1] // N_DEV
    else:
        _, n, ws = dys[0].shape
    d = ws_all[0].shape[1]
    tm, tn = _tile(n, 1024), _tile(d, 1024)
    if dy_is_flat:
        y_spec = pl.BlockSpec((tm, ws), lambda i, j, s: (i, s))
    else:
        y_spec = pl.BlockSpec((None, tm, ws), lambda i, j, s: (s, i, 0))
    w_spec = pl.BlockSpec((None, tn, ws), lambda i, j, s: (s, j, 0))
    k_terms = len(dys)
    return _matmul(name, (n // tm, d // tn, N_DEV), 2, (*dys, *ws_all), [y_spec] * k_terms + [w_spec] * k_terms,
                   [(i, k_terms + i, 0) for i in range(k_terms)], NT, [(tm, tn)], [S((n, d), F32)],
                   [pl.BlockSpec((tm, tn), lambda i, j, s: (i, j))], _store_epilogue)[0]


def _rms_fwd(name, x, g):
    n, d = x.shape
    tm = _tile(n, 256)

    def body(x_ref, g_ref, o_ref):
        xv = x_ref[...]
        r = lax.rsqrt(jnp.mean(xv * xv, axis=-1, keepdims=True) + RMS_EPS)
        o_ref[...] = (xv * r * g_ref[...]).astype(BF16)

    return pl.pallas_call(body, name=name, grid=(n // tm,),
                          in_specs=[pl.BlockSpec((tm, d), lambda i: (i, 0)), pl.BlockSpec((1, d), lambda i: (0, 0))],
                          out_specs=pl.BlockSpec((tm, d), lambda i: (i, 0)), out_shape=S((n, d), BF16), compiler_params=_cp(1))(x, g)


def _rms_bwd(name, x, g, dy, res):
    n, d = x.shape
    tm = _tile(n, 256)

    def body(x_ref, g_ref, dy_ref, res_ref, dx_ref, dg_ref):
        xv, dyv = x_ref[...], dy_ref[...]
        r = lax.rsqrt(jnp.mean(xv * xv, axis=-1, keepdims=True) + RMS_EPS)
        xr = xv * r
        dgy = dyv * g_ref[...]
        dx_ref[...] = res_ref[...] + r * (dgy - xr * jnp.mean(dgy * xr, axis=-1, keepdims=True))

        @pl.when(pl.program_id(0) == 0)
        def _():
            dg_ref[...] = jnp.zeros_like(dg_ref)

        dg_ref[...] += jnp.sum(dyv * xr, axis=0, keepdims=True)

    row = pl.BlockSpec((tm, d), lambda i: (i, 0))
    vec = pl.BlockSpec((1, d), lambda i: (0, 0))
    return pl.pallas_call(body, name=name, grid=(n // tm,), in_specs=[row, vec, row, row], out_specs=[row, vec],
                          out_shape=[S((n, d), F32), S((1, d), F32)], compiler_params=_cp(1))(x, g, dy, res)


def _final_loss(h, g, target):
    n, d = h.shape
    tm = _tile(n, 256)

    def body(h_ref, g_ref, t_ref, dh_ref, dhb_ref, dg_ref, sse_ref):
        hv, gv = h_ref[...], g_ref[...]
        r = lax.rsqrt(jnp.mean(hv * hv, axis=-1, keepdims=True) + RMS_EPS)
        hr = hv * r
        err = hr * gv - t_ref[...]
        dy = err * (1.0 / d)
        dgy = dy * gv
        dh = r * (dgy - hr * jnp.mean(dgy * hr, axis=-1, keepdims=True))
        dh_ref[...] = dh
        dhb_ref[...] = dh.astype(BF16)

        @pl.when(pl.program_id(0) == 0)
        def _():
            dg_ref[...] = jnp.zeros_like(dg_ref)
            sse_ref[...] = jnp.zeros_like(sse_ref)

        dg_ref[...] += jnp.sum(dy * hr, axis=0, keepdims=True)
        sse_ref[...] += jnp.sum(err * err)

    row = pl.BlockSpec((tm, d), lambda i: (i, 0))
    vec = pl.BlockSpec((1, d), lambda i: (0, 0))
    one = pl.BlockSpec((8, HEAD_DIM), lambda i: (0, 0))
    return pl.pallas_call(body, name="final_loss", grid=(n // tm,), in_specs=[row, vec, row], out_specs=[row, row, vec, one],
                          out_shape=[S((n, d), F32), S((n, d), BF16), S((1, d), F32), S((8, HEAD_DIM), F32)],
                          compiler_params=_cp(1))(h, g, target)


def _dot(a, b, dims):
    return lax.dot_general(a, b, dims, preferred_element_type=F32)


def _split3(x):
    hi = x.astype(BF16)
    r1 = x - hi.astype(F32)
    mid = r1.astype(BF16)
    lo = (r1 - mid.astype(F32)).astype(BF16)
    return hi, mid, lo


def _scan_cols(x, tri, terms):
    parts = _split3(x)[:terms]
    out = _dot(parts[0], tri, NN)
    for p in parts[1:]:
        out = out + _dot(p, tri, NN)
    return out


def _head_norm_fwd(o, g):
    r = lax.rsqrt(jnp.mean(o * o, axis=-1, keepdims=True) + RMS_EPS)
    return o * r * g


def _head_norm_bwd(o, g, dy):
    r = lax.rsqrt(jnp.mean(o * o, axis=-1, keepdims=True) + RMS_EPS)
    orr = o * r
    dgy = dy * g
    return r * (dgy - orr * jnp.mean(dgy * orr, axis=-1, keepdims=True)), dy * orr


def _interleave_plan(t):
    return [(i, dil, t // dil) for i, dil in enumerate(DILATIONS)]


def _band_mask(u, blocks_per_seq):
    row = lax.broadcasted_iota(jnp.int32, (BLK, 2 * BLK), 0)
    col = lax.broadcasted_iota(jnp.int32, (BLK, 2 * BLK), 1)
    dist = row + BLK - col
    has_prev = (u % blocks_per_seq) != 0
    return (dist >= 0) & (dist <= BLK) & ((col >= BLK) | has_prev)


def _dil_fwd(proj, g_dil, bl, t, n_heads):
    n = bl * t
    nb = t // BLK
    scale = HEAD_DIM ** -0.5
    plan = _interleave_plan(t)
    chunk = _tile(t, 256)

    def body(q_ref, k_ref, v_ref, g_ref, omix_ref, opre_ref, lse_ref, stg, qd, kd, vd, ob, lb, on, ln):
        for src, dst, pad in ((q_ref, qd, 0), (k_ref, kd, BLK), (v_ref, vd, BLK)):
            stg[...] = src[...].astype(F32)
            for bi, dil, sub in plan:
                if pad:
                    dst[bi, pl.ds(0, BLK), :] = jnp.zeros((BLK, HEAD_DIM), BF16)
                if dil == 1:
                    dst[bi, pl.ds(pad, t), :] = src[...]
                else:
                    for r in range(dil):
                        dst[bi, pl.ds(pad + r * sub, sub), :] = stg[pl.ds(r, sub, stride=dil), :].astype(BF16)

        for bi, dil, sub in plan:
            def blk(u, carry, bi=bi, sub=sub):
                rows = pl.ds(pl.multiple_of(u * BLK, BLK), BLK)
                win = pl.ds(pl.multiple_of(u * BLK, BLK), 2 * BLK)
                sc = _dot(qd[bi, rows, :], kd[bi, win, :], NT) * scale
                sc = jnp.where(_band_mask(u, sub // BLK), sc, NEG)
                m = jnp.max(sc, axis=-1, keepdims=True)
                p = jnp.exp(sc - m)
                den = jnp.sum(p, axis=-1, keepdims=True)
                ob[bi, rows, :] = _dot((p / den).astype(BF16), vd[bi, win, :], NN)
                lb[bi, rows, :] = jnp.broadcast_to(m + jnp.log(den), (BLK, HEAD_DIM))
                return carry
            lax.fori_loop(0, nb, blk, 0)

        for bi, dil, sub in plan[1:]:
            for r in range(dil):
                on[bi - 1, pl.ds(r, sub, stride=dil), :] = ob[bi, pl.ds(r * sub, sub), :]
                ln[bi - 1, pl.ds(r, sub, stride=dil), :] = lb[bi, pl.ds(r * sub, sub), :]

        def merge(i, carry):
            rows = pl.ds(pl.multiple_of(i * chunk, chunk), chunk)
            l0, l1, l2 = lb[0, rows, :], ln[0, rows, :], ln[1, rows, :]
            mx = jnp.maximum(jnp.maximum(l0, l1), l2)
            w0, w1, w2 = jnp.exp(l0 - mx), jnp.exp(l1 - mx), jnp.exp(l2 - mx)
            tot = w0 + w1 + w2
            o = (w0 / tot) * ob[0, rows, :] + (w1 / tot) * on[0, rows, :] + (w2 / tot) * on[1, rows, :]
            lse_ref[rows, :] = mx + jnp.log(tot)
            opre_ref[rows, :] = o
            omix_ref[rows, :] = _head_norm_fwd(o, g_ref[...]).astype(BF16)
            return carry
        lax.fori_loop(0, t // chunk, merge, 0)

    hs = n_heads
    col = lambda off: pl.BlockSpec((t, HEAD_DIM), lambda b, h: (b, off + h))
    return pl.pallas_call(
        body, name="dil_fwd", grid=(bl, hs),
        in_specs=[col(0), col(hs), col(2 * hs), pl.BlockSpec((1, HEAD_DIM), lambda b, h: (0, h))],
        out_specs=[col(0), col(0), pl.BlockSpec((t, HEAD_DIM), lambda b, h: (b * hs + h, 0))],
        out_shape=[S((n, 2 * hs * HEAD_DIM), BF16), S((n, 2 * hs * HEAD_DIM), F32), S((bl * hs * t, HEAD_DIM), F32)],
        scratch_shapes=[pltpu.VMEM((t, HEAD_DIM), F32), pltpu.VMEM((3, t, HEAD_DIM), BF16),
                        pltpu.VMEM((3, t + BLK, HEAD_DIM), BF16), pltpu.VMEM((3, t + BLK, HEAD_DIM), BF16),
                        pltpu.VMEM((3, t, HEAD_DIM), F32), pltpu.VMEM((3, t, HEAD_DIM), F32),
                        pltpu.VMEM((2, t, HEAD_DIM), F32), pltpu.VMEM((2, t, HEAD_DIM), F32)],
        compiler_params=_cp(2),
    )(proj, proj, proj, g_dil)


def _dil_bwd(proj, opre, lse, d_omix, g_dil, rc, rs1, rs2, bl, t, n_heads):
    n = bl * t
    nb = t // BLK
    scale = HEAD_DIM ** -0.5
    plan = _interleave_plan(t)
    chunk = _tile(t, 256)

    def body(q_ref, k_ref, v_ref, opre_ref, lse_ref, dy_ref, g_ref, c_ref, s1_ref, s2_ref, out_ref, dg_ref,
             stg, qd, kd, vd, dod, ld, dd, dqd, dkd, dvd, sk, sv):
        which = pl.program_id(2)

        @pl.when(jnp.logical_and(which == 0, pl.program_id(1) == 0))
        def _():
            dg_ref[...] = jnp.zeros_like(dg_ref)

        @pl.when(which == 0)
        def _():
            def prep(i, dg):
                rows = pl.ds(pl.multiple_of(i * chunk, chunk), chunk)
                o = opre_ref[rows, :]
                d_o, dg_rows = _head_norm_bwd(o, g_ref[...], dy_ref[rows, :])
                stg[rows, :] = d_o
                dd[0, rows, :] = jnp.broadcast_to(jnp.sum(d_o * o, axis=-1, keepdims=True), (chunk, HEAD_DIM))
                return dg + jnp.sum(dg_rows, axis=0, keepdims=True)
            dg_ref[...] += lax.fori_loop(0, t // chunk, prep, jnp.zeros((1, HEAD_DIM), F32))

            dod[0] = stg[...].astype(BF16)
            ld[0] = lse_ref[...]
            for bi, dil, sub in plan[1:]:
                for r in range(dil):
                    dst = pl.ds(r * sub, sub)
                    dod[bi, dst, :] = stg[pl.ds(r, sub, stride=dil), :].astype(BF16)
                    dd[bi, dst, :] = dd[0, pl.ds(r, sub, stride=dil), :]
                    ld[bi, dst, :] = lse_ref[pl.ds(r, sub, stride=dil), :]
            for src, dst, pad in ((q_ref, qd, 0), (k_ref, kd, BLK), (v_ref, vd, BLK)):
                stg[...] = src[...].astype(F32)
                for bi, dil, sub in plan:
                    if pad:
                        dst[bi, pl.ds(0, BLK), :] = jnp.zeros((BLK, HEAD_DIM), BF16)
                    if dil == 1:
                        dst[bi, pl.ds(pad, t), :] = src[...]
                    else:
                        for r in range(dil):
                            dst[bi, pl.ds(pad + r * sub, sub), :] = stg[pl.ds(r, sub, stride=dil), :].astype(BF16)
            dkd[...] = jnp.zeros_like(dkd)
            dvd[...] = jnp.zeros_like(dvd)

            for bi, dil, sub in plan:
                def blk(u, carry, bi=bi, sub=sub):
                    rows = pl.ds(pl.multiple_of(u * BLK, BLK), BLK)
                    win = pl.ds(pl.multiple_of(u * BLK, BLK), 2 * BLK)
                    qb, kw, vw, dob = qd[bi, rows, :], kd[bi, win, :], vd[bi, win, :], dod[bi, rows, :]
                    sc = _dot(qb, kw, NT) * scale
                    p = jnp.where(_band_mask(u, sub // BLK), jnp.exp(sc - ld[bi, rows, :][:, :1]), 0.0)
                    dp = _dot(dob, vw, NT)
                    ds = (p * (dp - dd[bi, rows, :][:, :1]) * scale).astype(BF16)
                    dqd[bi, rows, :] = _dot(ds, kw, NN)
                    dkd[bi, win, :] += _dot(ds, qb, TN)
                    dvd[bi, win, :] += _dot(p.astype(BF16), dob, TN)
                    return carry
                lax.fori_loop(0, nb, blk, 0)

            for acc, pad, undo_rope, dst in ((dqd, 0, True, out_ref), (dkd, BLK, True, sk), (dvd, BLK, False, sv)):
                stg[...] = acc[0, pl.ds(pad, t), :]
                for bi, dil, sub in plan[1:]:
                    for r in range(dil):
                        stg[pl.ds(r, sub, stride=dil), :] += acc[bi, pl.ds(pad + r * sub, sub), :]
                if undo_rope:
                    dst[...] = _rope_bwd(stg[...], c_ref[...], s1_ref[...], s2_ref[...]).astype(BF16)
                else:
                    dst[...] = stg[...].astype(BF16)

        @pl.when(which == 1)
        def _():
            out_ref[...] = sk[...]

        @pl.when(which == 2)
        def _():
            out_ref[...] = sv[...]

    hs = n_heads
    col = lambda off: pl.BlockSpec((t, HEAD_DIM), lambda h, b, w: (b, off + h))
    per_head = pl.BlockSpec((t, HEAD_DIM), lambda h, b, w: (b * hs + h, 0))
    tab = pl.BlockSpec((t, HEAD_DIM), lambda h, b, w: (b, 0))
    gvec = pl.BlockSpec((1, HEAD_DIM), lambda h, b, w: (0, h))
    tb = (t, HEAD_DIM)
    tp = (t + BLK, HEAD_DIM)
    return pl.pallas_call(
        body, name="dil_bwd", grid=(hs, bl, 3),
        in_specs=[col(0), col(hs), col(2 * hs), col(0), per_head, col(0), gvec, tab, tab, tab],
        out_specs=[pl.BlockSpec((t, HEAD_DIM), lambda h, b, w: (b, w * hs + h)), gvec],
        out_shape=[S((n, 6 * hs * HEAD_DIM), BF16), S((1, hs * HEAD_DIM), F32)],
        scratch_shapes=[pltpu.VMEM(tb, F32), pltpu.VMEM((3,) + tb, BF16), pltpu.VMEM((3,) + tp, BF16), pltpu.VMEM((3,) + tp, BF16),
                        pltpu.VMEM((3,) + tb, BF16), pltpu.VMEM((3,) + tb, F32), pltpu.VMEM((3,) + tb, F32),
                        pltpu.VMEM((3,) + tb, F32), pltpu.VMEM((3,) + tp, F32), pltpu.VMEM((3,) + tp, F32),
                        pltpu.VMEM(tb, BF16), pltpu.VMEM(tb, BF16)],
        compiler_params=_cp(3),
    )(proj, proj, proj, opre, lse, d_omix, g_dil, rc, rs1, rs2)


def _sb_pieces(qb, kb, qi, kj, scale):
    z = _dot(qb, kb, NT) * scale
    row = lax.broadcasted_iota(jnp.int32, (BLK, BLK), 0)
    col = lax.broadcasted_iota(jnp.int32, (BLK, BLK), 1)
    strict = (col - row) < (qi - kj) * BLK
    return z, strict, jnp.log1p(jnp.exp(-jnp.abs(z)))


def _tri(cmp):
    row = lax.broadcasted_iota(jnp.int32, (BLK, BLK), 0)
    col = lax.broadcasted_iota(jnp.int32, (BLK, BLK), 1)
    return jnp.where(cmp(row, col), 1.0, 0.0).astype(BF16)


def _sb_fwd(proj, g_sb, omix_in, opre_in, bl, t, n_heads):
    nb = t // BLK
    scale = HEAD_DIM ** -0.5

    def body(q_ref, k_ref, v_ref, g_ref, _omix_in, _opre_in, omix_ref, opre_ref, lt_ref):
        later = _tri(lambda r, c: r > c)

        def q_block(qi, carry):
            rows = pl.ds(pl.multiple_of(qi * BLK, BLK), BLK)
            qb = q_ref[rows, :]

            def k_block(it, st):
                run, acc = st
                kj = qi - it
                krows = pl.ds(pl.multiple_of(kj * BLK, BLK), BLK)
                z, strict, tl = _sb_pieces(qb, k_ref[krows, :], qi, kj, scale)
                log_not = jnp.where(strict, -(jnp.maximum(z, 0.0) + tl), 0.0)
                excl = _scan_cols(log_not, later, 3) + run
                a = jnp.where(strict, jnp.exp(jnp.minimum(z, 0.0) - tl + excl), 0.0)
                acc = acc + _dot(a.astype(BF16), v_ref[krows, :], NN)
                return run + jnp.sum(log_not, axis=-1, keepdims=True), acc

            run, acc = lax.fori_loop(0, qi + 1, k_block, (jnp.zeros((BLK, 1), F32), jnp.zeros((BLK, HEAD_DIM), F32)))
            lt_ref[rows, :] = jnp.broadcast_to(run, (BLK, HEAD_DIM))
            opre_ref[rows, :] = acc
            omix_ref[rows, :] = _head_norm_fwd(acc, g_ref[...]).astype(BF16)
            return carry

        lax.fori_loop(0, nb, q_block, 0)

    hs = n_heads
    col = lambda off: pl.BlockSpec((t, HEAD_DIM), lambda b, h: (b, off + h))
    return pl.pallas_call(
        body, name="sb_fwd", grid=(bl, hs),
        in_specs=[col(3 * hs), col(4 * hs), col(5 * hs), pl.BlockSpec((1, HEAD_DIM), lambda b, h: (0, h)), HBM_SPEC, HBM_SPEC],
        out_specs=[col(hs), col(hs), pl.BlockSpec((t, HEAD_DIM), lambda b, h: (b * hs + h, 0))],
        out_shape=[S(omix_in.shape, BF16), S(opre_in.shape, F32), S((bl * hs * t, HEAD_DIM), F32)],
        input_output_aliases={4: 0, 5: 1}, compiler_params=_cp(2),
    )(proj, proj, proj, g_sb, omix_in, opre_in)


def _sb_bwd(proj, opre, ltot, d_omix, g_sb, dproj_in, bl, t, n_heads):
    nb = t // BLK
    scale = HEAD_DIM ** -0.5

    def body(q_ref, k_ref, v_ref, opre_ref, lt_ref, dy_ref, g_ref, _dproj_in, out_ref, dg_ref, dq, dk, dv):
        which = pl.program_id(2)

        @pl.when(jnp.logical_and(which == 0, pl.program_id(1) == 0))
        def _():
            dg_ref[...] = jnp.zeros_like(dg_ref)

        @pl.when(which == 0)
        def _():
            upto = _tri(lambda r, c: r <= c)
            before = _tri(lambda r, c: r < c)
            dk[...] = jnp.zeros_like(dk)
            dv[...] = jnp.zeros_like(dv)

            def q_block(qi, dg):
                rows = pl.ds(pl.multiple_of(qi * BLK, BLK), BLK)
                qb = q_ref[rows, :]
                o = opre_ref[rows, :]
                d_o, dg_rows = _head_norm_bwd(o, g_ref[...], dy_ref[rows, :])
                dob = d_o.astype(BF16)
                lt = lt_ref[rows, :][:, :1]

                def k_block(kj, st):
                    run, grun, dq_acc = st
                    krows = pl.ds(pl.multiple_of(kj * BLK, BLK), BLK)
                    kb, vb = k_ref[krows, :], v_ref[krows, :]
                    z, strict, tl = _sb_pieces(qb, kb, qi, kj, scale)
                    log_not = jnp.where(strict, -(jnp.maximum(z, 0.0) + tl), 0.0)
                    excl = lt - (run + _scan_cols(log_not, upto, 3))
                    log_beta = jnp.minimum(z, 0.0) - tl
                    a = jnp.where(strict, jnp.exp(log_beta + excl), 0.0)
                    g_a = a * _dot(dob, vb, NT)
                    g_before = grun + _scan_cols(g_a, before, 2)
                    dz = jnp.where(strict, g_a - (g_a + g_before) * jnp.exp(log_beta), 0.0) * scale
                    dzb = dz.astype(BF16)
                    dk[krows, :] += _dot(dzb, qb, TN)
                    dv[krows, :] += _dot(a.astype(BF16), dob, TN)
                    return (run + jnp.sum(log_not, axis=-1, keepdims=True), grun + jnp.sum(g_a, axis=-1, keepdims=True),
                            dq_acc + _dot(dzb, kb, NN))

                zero_col = jnp.zeros((BLK, 1), F32)
                _, _, dq_acc = lax.fori_loop(0, qi + 1, k_block, (zero_col, zero_col, jnp.zeros((BLK, HEAD_DIM), F32)))
                dq[rows, :] = dq_acc.astype(BF16)
                return dg + jnp.sum(dg_rows, axis=0, keepdims=True)

            dg_ref[...] += lax.fori_loop(0, nb, q_block, jnp.zeros((1, HEAD_DIM), F32))
            out_ref[...] = dq[...]

        @pl.when(which == 1)
        def _():
            out_ref[...] = dk[...].astype(BF16)

        @pl.when(which == 2)
        def _():
            out_ref[...] = dv[...].astype(BF16)

    hs = n_heads
    col = lambda off: pl.BlockSpec((t, HEAD_DIM), lambda h, b, w: (b, off + h))
    per_head = pl.BlockSpec((t, HEAD_DIM), lambda h, b, w: (b * hs + h, 0))
    gvec = pl.BlockSpec((1, HEAD_DIM), lambda h, b, w: (0, h))
    tb = (t, HEAD_DIM)
    return pl.pallas_call(
        body, name="sb_bwd", grid=(hs, bl, 3),
        in_specs=[col(3 * hs), col(4 * hs), col(5 * hs), col(hs), per_head, col(hs), gvec, HBM_SPEC],
        out_specs=[pl.BlockSpec((t, HEAD_DIM), lambda h, b, w: (b, (3 + w) * hs + h)), gvec],
        out_shape=[S(dproj_in.shape, BF16), S((1, hs * HEAD_DIM), F32)],
        scratch_shapes=[pltpu.VMEM(tb, BF16), pltpu.VMEM(tb, F32), pltpu.VMEM(tb, F32)],
        input_output_aliases={7: 0}, compiler_params=_cp(3),
    )(proj, proj, proj, opre, ltot, d_omix, g_sb, dproj_in)


def _all_gather(shards):
    k_w = len(shards)

    def body(*refs):
        ins, outs = refs[:k_w], refs[k_w:2 * k_w]
        send_sems, recv_sems, local_sems = refs[2 * k_w:]
        x, y, c = lax.axis_index("x"), lax.axis_index("y"), lax.axis_index("c")
        me, sibling = (x, y, c), (x, y, 1 - c)
        chips = [(1 - x, y), (x, 1 - y), (1 - x, 1 - y)]

        def slot(dev):
            return 4 * dev[0] + 2 * dev[1] + dev[2]

        def copy(w, k, block, to, src=None):
            dst = outs[w].at[slot(block)]
            return pltpu.make_async_remote_copy(
                src_ref=dst if src is None else src, dst_ref=dst, send_sem=send_sems.at[w * 7 + k], recv_sem=recv_sems.at[w * 7 + k],
                device_id=to, device_id_type=MESH)

        mine = [pltpu.make_async_copy(ins[w], outs[w].at[slot(me)], local_sems.at[w]) for w in range(k_w)]
        first = []
        for w in range(k_w):
            mine[w].start()
            first.append(copy(w, 0, me, sibling, src=ins[w]))
            first += [copy(w, 1 + j, me, (*chip, c), src=ins[w]) for j, chip in enumerate(chips)]
        for cp in first:
            cp.start()
        passed = []
        for w in range(k_w):
            for j, chip in enumerate(chips):
                copy(w, 1 + j, (*chip, c), me).wait_recv()
                fwd = copy(w, 4 + j, (*chip, c), sibling)
                fwd.start()
                passed.append(fwd)
        for w in range(k_w):
            copy(w, 0, sibling, me).wait_recv()
            for j, chip in enumerate(chips):
                copy(w, 4 + j, (*chip, 1 - c), me).wait_recv()
        for cp in first + passed:
            cp.wait_send()
        for cp in mine:
            cp.wait()

    return pl.pallas_call(
        body, name="weights_all_gather", in_specs=[HBM_SPEC] * k_w, out_specs=[HBM_SPEC] * k_w,
        out_shape=[S((N_DEV,) + s.shape, s.dtype) for s in shards],
        scratch_shapes=[pltpu.SemaphoreType.DMA((7 * k_w,)), pltpu.SemaphoreType.DMA((7 * k_w,)), pltpu.SemaphoreType.DMA((k_w,))],
    )(*shards)


def _grad_exchange(grads):
    k_w = len(grads)

    def body(*refs):
        ins, outs = refs[:k_w], refs[k_w:2 * k_w]
        send_sems, recv_sems, local_sems = refs[2 * k_w:]
        x, y, c = lax.axis_index("x"), lax.axis_index("y"), lax.axis_index("c")
        my_slot = 4 * x + 2 * y + c
        copies = []
        for w in range(k_w):
            own = pltpu.make_async_copy(ins[w].at[my_slot], outs[w].at[my_slot], local_sems.at[w])
            own.start()
            copies.append(own)
        remote = []
        for w in range(k_w):
            for k in range(1, N_DEV):
                px, py, pc = x ^ (k >> 2), y ^ ((k >> 1) & 1), c ^ (k & 1)
                cp = pltpu.make_async_remote_copy(
                    src_ref=ins[w].at[4 * px + 2 * py + pc], dst_ref=outs[w].at[my_slot],
                    send_sem=send_sems.at[w * 7 + k - 1], recv_sem=recv_sems.at[w * 7 + k - 1],
                    device_id=(px, py, pc), device_id_type=MESH)
                cp.start()
                remote.append(cp)
        for cp in remote:
            cp.wait_send()
        for w in range(k_w):
            for k in range(1, N_DEV):
                px, py, pc = x ^ (k >> 2), y ^ ((k >> 1) & 1), c ^ (k & 1)
                peer_slot = 4 * px + 2 * py + pc
                pltpu.make_async_remote_copy(
                    src_ref=ins[w].at[my_slot], dst_ref=outs[w].at[peer_slot],
                    send_sem=send_sems.at[w * 7 + k - 1], recv_sem=recv_sems.at[w * 7 + k - 1],
                    device_id=(px, py, pc), device_id_type=MESH).wait_recv()
        for cp in copies:
            cp.wait()

    return pl.pallas_call(
        body, name="grad_exchange", in_specs=[HBM_SPEC] * k_w, out_specs=[HBM_SPEC] * k_w,
        out_shape=[S(g.shape, g.dtype) for g in grads],
        scratch_shapes=[pltpu.SemaphoreType.DMA((7 * k_w,)), pltpu.SemaphoreType.DMA((7 * k_w,)), pltpu.SemaphoreType.DMA((k_w,))],
    )(*grads)


def _adamw(name, parts, w, m, v):
    r, c = w.shape
    tr = _tile(r, max(16, (1 << 19) // c // 16 * 16))

    def body(p_ref, w_ref, m_ref, v_ref, g_ref, d_ref, nm_ref, nv_ref):
        g = p_ref[0].astype(F32)
        for s in range(1, N_DEV):
            g = g + p_ref[s].astype(F32)
        m_new = ADAM_B1 * m_ref[...] + (1.0 - ADAM_B1) * g
        v_new = ADAM_B2 * v_ref[...] + (1.0 - ADAM_B2) * jnp.square(g)
        m_hat = m_new / (1.0 - ADAM_B1 ** ADAM_STEP)
        v_hat = v_new / (1.0 - ADAM_B2 ** ADAM_STEP)
        g_ref[...] = g
        d_ref[...] = -ADAM_LR * (m_hat / (jnp.sqrt(v_hat) + ADAM_EPS) + ADAM_WD * w_ref[...])
        nm_ref[...] = m_new
        nv_ref[...] = v_new

    blk = pl.BlockSpec((tr, c), lambda i: (i, 0))
    return pl.pallas_call(body, name=name, grid=(r // tr,), in_specs=[pl.BlockSpec((N_DEV, tr, c), lambda i: (0, i, 0)), blk, blk, blk],
                          out_specs=[blk] * 4, out_shape=[S((r, c), F32)] * 4, compiler_params=_cp(1))(parts, w, m, v)


def _rope_tables(positions):
    inv_freq = jnp.power(jnp.float32(ROPE_THETA), -jnp.arange(ROPE_HALF, dtype=F32) / ROPE_HALF)
    ang = positions.astype(F32).reshape(-1, 1) * inv_freq
    cos, sin = jnp.cos(ang), jnp.sin(ang)
    n = ang.shape[0]
    rest = HEAD_DIM - 2 * ROPE_HALF
    zeros = jnp.zeros((n, ROPE_HALF), F32)
    c = jnp.concatenate([cos, cos, jnp.ones((n, rest), F32)], axis=1)
    s1 = jnp.concatenate([zeros, sin, jnp.zeros((n, rest), F32)], axis=1)
    s2 = jnp.concatenate([-sin, zeros, jnp.zeros((n, rest), F32)], axis=1)
    return c, s1, s2


def kernel(x, positions, norm_mix_g, w_in, norm_out_dil_g, norm_out_sb_g, w_out, norm_ffn_g, w_gate, w_up, w_down, norm_final_g, loss_target, m_norm_mix_g, m_w_in, m_norm_out_dil_g, m_norm_out_sb_g, m_w_out, m_norm_ffn_g, m_w_gate, m_w_up, m_w_down, m_norm_final_g, v_norm_mix_g, v_w_in, v_norm_out_dil_g, v_norm_out_sb_g, v_w_out, v_norm_ffn_g, v_w_gate, v_w_up, v_w_down, v_norm_final_g):
    bl, t, d = x.shape
    n = bl * t
    hs = d // (2 * HEAD_DIM)
    x2 = x.reshape(n, d)
    target = loss_target.reshape(n, d)
    g_final = norm_final_g.reshape(1, d)
    rc, rs1, rs2 = _rope_tables(positions)

    win_all, wout_all, wg_all, wu_all, wd_all = _all_gather(
        [w_in[0].astype(BF16), w_out[0].astype(BF16), w_gate[0].astype(BF16), w_up[0].astype(BF16), w_down[0].astype(BF16)])
    wout_full = wout_all.reshape(d, d)

    hn1 = _rms_fwd("rms_mix_fwd", x2, norm_mix_g)
    proj = _proj_fwd(hn1, win_all, rc, rs1, rs2, 2 * hs)
    omix, opre, lse = _dil_fwd(proj, norm_out_dil_g, bl, t, hs)
    omix, opre, ltot = _sb_fwd(proj, norm_out_sb_g, omix, opre, bl, t, hs)
    h1 = _dense_res("out_fwd", omix, wout_full, x2, NN)
    hn2 = _rms_fwd("rms_ffn_fwd", h1, norm_ffn_g)
    gate, up, act = _gateup_fwd(hn2, wg_all, wu_all)
    h2 = _down_fwd(act, wd_all, h1)
    dh2, dh2b, dg_final, sse = _final_loss(h2, g_final, target)
    loss = lax.psum(sse[0, 0], ("x", "y", "c")) * (0.5 / d)

    dgate, dup = _dact_bwd(dh2b, wd_all, gate, up)
    dwd = _dwd_bwd(act, dh2b)
    dwg, dwu = _dw_cols_bwd("dwgu_bwd", hn2, (dgate, dup))
    dhn2 = _dhn_from_shards("dhn2_bwd", (dgate, dup), (wg_all, wu_all), False)
    dh1, dg_ffn = _rms_bwd("rms_ffn_bwd", h1, norm_ffn_g, dhn2, dh2)
    d_omix = _dense_res("domix_bwd", dh1, wout_full, None, NT)
    dwout = _tn_full("dwout_bwd", omix, dh1).reshape(N_DEV, d // N_DEV, d)
    dproj, dg_dil = _dil_bwd(proj, opre, lse, d_omix, norm_out_dil_g, rc, rs1, rs2, bl, t, hs)
    dproj, dg_sb = _sb_bwd(proj, opre, ltot, d_omix, norm_out_sb_g, dproj, bl, t, hs)
    dwin = _dwin_bwd(hn1, dproj)
    dhn1 = _dhn_from_shards("dhn1_bwd", (dproj,), (win_all,), True)
    dx, dg_mix = _rms_bwd("rms_mix_bwd", x2, norm_mix_g, dhn1, dh1)

    gains = [norm_mix_g, norm_out_dil_g, norm_out_sb_g, norm_ffn_g, g_final]
    m_gains = [m_norm_mix_g, m_norm_out_dil_g, m_norm_out_sb_g, m_norm_ffn_g, m_norm_final_g.reshape(1, d)]
    v_gains = [v_norm_mix_g, v_norm_out_dil_g, v_norm_out_sb_g, v_norm_ffn_g, v_norm_final_g.reshape(1, d)]
    dg_vec = jnp.concatenate([dg_mix, dg_dil, dg_sb, dg_ffn, dg_final], axis=1)
    dg_all = jnp.broadcast_to(dg_vec[None], (N_DEV,) + dg_vec.shape)
    rwin, rwout, rwg, rwu, rwd, rg = _grad_exchange([dwin, dwout, dwg, dwu, dwd, dg_all])

    out_w = {}
    for name, parts, w, m, v in (("w_in", rwin, w_in, m_w_in, v_w_in), ("w_out", rwout, w_out, m_w_out, v_w_out),
                                 ("w_gate", rwg, w_gate, m_w_gate, v_w_gate), ("w_up", rwu, w_up, m_w_up, v_w_up),
                                 ("w_down", rwd, w_down, m_w_down, v_w_down)):
        out_w[name] = [o[None] for o in _adamw("adamw_" + name, parts, w[0], m[0], v[0])]
    cat = lambda vs: jnp.concatenate(vs, axis=1)
    gain_out = _adamw("adamw_gains", rg, cat(gains), cat(m_gains), cat(v_gains))
    widths = [d, d // 2, d // 2, d]
    cuts = [sum(widths[:i + 1]) for i in range(4)]
    gain_split = [jnp.split(o, cuts, axis=1) for o in gain_out]

    def ordered(kind):
        gs = gain_split[kind]
        return (gs[0], out_w["w_in"][kind], gs[1], gs[2], out_w["w_out"][kind], gs[3], out_w["w_gate"][kind],
                out_w["w_up"][kind], out_w["w_down"][kind], gs[4].reshape(d))

    return (loss, dx.reshape(bl, t, d), *ordered(0), *ordered(1), *ordered(2), *ordered(3))
```

```python
import functools
import math

import jax
import jax.numpy as jnp
from jax import lax
from jax.experimental import pallas as pl
from jax.experimental.pallas import tpu as pltpu

F32 = jnp.float32
BF16 = jnp.bfloat16
S = jax.ShapeDtypeStruct

N_DEV = 8
HEAD_DIM = 128
BLK = 128
SBT = 256
DIL_UNROLL = 4
ROPE_HALF = 16
ROPE_THETA = 500000.0
RMS_EPS = 1e-5
DILATIONS = (1, 4, 16)
NEG = -1e30
VMEM_LIMIT = 56 * 1024 * 1024

ADAM_LR = 0.001
ADAM_B1 = 0.9
ADAM_B2 = 0.999
ADAM_EPS = 1e-08
ADAM_WD = 0.01
ADAM_STEP = 10

MESH = pl.DeviceIdType.MESH
HBM_SPEC = pl.BlockSpec(memory_space=pltpu.HBM)


def _cp(n_axes):
    return pltpu.CompilerParams(dimension_semantics=("arbitrary",) * n_axes, vmem_limit_bytes=VMEM_LIMIT)


def _tile(n, want):
    if n <= want:
        return n
    t = want
    while t >= 16:
        if n % t == 0 and t % 16 == 0:
            return t
        t -= 16
    return n


NN = (((1,), (0,)), ((), ()))
NT = (((1,), (1,)), ((), ()))
TN = (((0,), (0,)), ((), ()))


def _matmul(name, grid, red_axis, ins, in_specs, terms, dims, acc_shapes, out_shapes, out_specs, epilogue):
    n_in, n_out = len(ins), len(out_shapes)
    n_red = grid[red_axis]

    def body(*refs):
        in_refs, out_refs, acc_refs = refs[:n_in], refs[n_in:n_in + n_out], refs[n_in + n_out:]
        k = pl.program_id(red_axis)

        @pl.when(k == 0)
        def _():
            for acc in acc_refs:
                acc[...] = jnp.zeros_like(acc)

        for a_idx, b_idx, acc_idx in terms:
            a = in_refs[a_idx][...].astype(BF16)
            b = in_refs[b_idx][...].astype(BF16)
            acc_refs[acc_idx][...] += lax.dot_general(a, b, dims, preferred_element_type=F32)

        @pl.when(k == n_red - 1)
        def _():
            epilogue(acc_refs, in_refs, out_refs)

    return pl.pallas_call(
        body, name=name, grid=grid, in_specs=in_specs, out_specs=out_specs, out_shape=out_shapes,
        scratch_shapes=[pltpu.VMEM(s, F32) for s in acc_shapes], compiler_params=_cp(len(grid)),
    )(*ins)


def _store_epilogue(acc_refs, in_refs, out_refs):
    for acc, out in zip(acc_refs, out_refs):
        out[...] = acc[...].astype(out.dtype)


def _rope_fwd(a, c, s1, s2):
    return a * c + pltpu.roll(a, ROPE_HALF, 1) * s1 + pltpu.roll(a, HEAD_DIM - ROPE_HALF, 1) * s2


def _rope_bwd(d, c, s1, s2):
    return d * c + pltpu.roll(d * s1, HEAD_DIM - ROPE_HALF, 1) + pltpu.roll(d * s2, ROPE_HALF, 1)


def _proj_fwd(hn, w_all, rc, rs1, rs2, n_rope_heads):
    n, d = hn.shape
    _, _, ws = w_all.shape
    tm, tk = _tile(n, 1024), _tile(d, 512)
    heads_per_shard = ws // HEAD_DIM
    rows = _tile(tm, 256)

    def epilogue(acc_refs, in_refs, out_refs):
        acc, out = acc_refs[0], out_refs[0]
        j = pl.program_id(0)
        for r0 in range(0, tm, rows):
            c, s1, s2 = (ref[pl.ds(r0, rows), :] for ref in in_refs[2:5])
            for hh in range(heads_per_shard):
                a = acc[pl.ds(r0, rows), pl.ds(hh * HEAD_DIM, HEAD_DIM)]
                roped = _rope_fwd(a, c, s1, s2)
                a = jnp.where(j * heads_per_shard + hh < n_rope_heads, roped, a)
                out[pl.ds(r0, rows), pl.ds(hh * HEAD_DIM, HEAD_DIM)] = a.astype(out.dtype)

    tab = pl.BlockSpec((tm, HEAD_DIM), lambda j, m, k: (m, 0))
    return _matmul(
        "proj_fwd", (N_DEV, n // tm, d // tk), 2, (hn, w_all, rc, rs1, rs2),
        [pl.BlockSpec((tm, tk), lambda j, m, k: (m, k)), pl.BlockSpec((None, tk, ws), lambda j, m, k: (j, k, 0)), tab, tab, tab],
        [(0, 1, 0)], NN, [(tm, ws)], [S((n, N_DEV * ws), BF16)], [pl.BlockSpec((tm, ws), lambda j, m, k: (m, j))], epilogue)[0]


def _dense_res(name, a, b, res, dims, out_dtype=F32):
    m, kdim = a.shape
    n = b.shape[1] if dims == NN else b.shape[0]
    tm, tn, tk = _tile(m, 1024), _tile(n, 1024), _tile(kdim, 512)
    ins = [a, b] + ([res] if res is not None else [])
    b_spec = pl.BlockSpec((tk, tn), lambda i, j, k: (k, j)) if dims == NN else pl.BlockSpec((tn, tk), lambda i, j, k: (j, k))
    specs = [pl.BlockSpec((tm, tk), lambda i, j, k: (i, k)), b_spec]
    if res is not None:
        specs.append(pl.BlockSpec((tm, tn), lambda i, j, k: (i, j)))

    def epilogue(acc_refs, in_refs, out_refs):
        v = acc_refs[0][...]
        if res is not None:
            v = v + in_refs[2][...]
        out_refs[0][...] = v.astype(out_dtype)

    return _matmul(name, (m // tm, n // tn, kdim // tk), 2, ins, specs, [(0, 1, 0)], dims, [(tm, tn)],
                   [S((m, n), out_dtype)], [pl.BlockSpec((tm, tn), lambda i, j, k: (i, j))], epilogue)[0]


def _tn_full(name, a, b, out_dtype=BF16):
    m, kdim = a.shape
    n = b.shape[1]
    tm, tk, tn = _tile(m, 512), _tile(kdim, 1024), _tile(n, 1024)
    return _matmul(name, (kdim // tk, n // tn, m // tm), 2, (a, b),
                   [pl.BlockSpec((tm, tk), lambda i, j, t: (t, i)), pl.BlockSpec((tm, tn), lambda i, j, t: (t, j))],
                   [(0, 1, 0)], TN, [(tk, tn)], [S((kdim, n), out_dtype)], [pl.BlockSpec((tk, tn), lambda i, j, t: (i, j))],
                   _store_epilogue)[0]


def _gateup_fwd(hn, wg_all, wu_all):
    n, d = hn.shape
    _, _, fs = wg_all.shape
    tm, tk = _tile(n, 1024), _tile(d, 512)
    rows = _tile(tm, 256)

    def epilogue(acc_refs, in_refs, out_refs):
        for r0 in range(0, tm, rows):
            g = acc_refs[0][pl.ds(r0, rows), :]
            u = acc_refs[1][pl.ds(r0, rows), :]
            out_refs[0][pl.ds(r0, rows), :] = g.astype(BF16)
            out_refs[1][pl.ds(r0, rows), :] = u.astype(BF16)
            out_refs[2][pl.ds(r0, rows), :] = (g * jax.nn.sigmoid(g) * u).astype(BF16)

    w_spec = pl.BlockSpec((None, tk, fs), lambda j, m, k: (j, k, 0))
    o_spec = pl.BlockSpec((None, tm, fs), lambda j, m, k: (j, m, 0))
    o_shape = S((N_DEV, n, fs), BF16)
    return _matmul("gateup_fwd", (N_DEV, n // tm, d // tk), 2, (hn, wg_all, wu_all),
                   [pl.BlockSpec((tm, tk), lambda j, m, k: (m, k)), w_spec, w_spec],
                   [(0, 1, 0), (0, 2, 1)], NN, [(tm, fs), (tm, fs)], [o_shape] * 3, [o_spec] * 3, epilogue)


def _down_fwd(act, wd_all, res):
    _, n, fs = act.shape
    d = wd_all.shape[2]
    tm, tn = _tile(n, 1024), _tile(d, 1024)

    def epilogue(acc_refs, in_refs, out_refs):
        out_refs[0][...] = acc_refs[0][...] + in_refs[2][...]

    return _matmul("down_fwd", (n // tm, d // tn, N_DEV), 2, (act, wd_all, res),
                   [pl.BlockSpec((None, tm, fs), lambda i, j, s: (s, i, 0)), pl.BlockSpec((None, fs, tn), lambda i, j, s: (s, 0, j)),
                    pl.BlockSpec((tm, tn), lambda i, j, s: (i, j))],
                   [(0, 1, 0)], NN, [(tm, tn)], [S((n, d), F32)], [pl.BlockSpec((tm, tn), lambda i, j, s: (i, j))], epilogue)[0]


def _dact_bwd(dh, wd_all, gate, up):
    n, d = dh.shape
    _, fs, _ = wd_all.shape
    tm, tk = _tile(n, 1024), _tile(d, 512)
    rows = _tile(tm, 256)

    def epilogue(acc_refs, in_refs, out_refs):
        for r0 in range(0, tm, rows):
            da = acc_refs[0][pl.ds(r0, rows), :]
            g = in_refs[2][pl.ds(r0, rows), :].astype(F32)
            u = in_refs[3][pl.ds(r0, rows), :].astype(F32)
            sg = jax.nn.sigmoid(g)
            out_refs[0][pl.ds(r0, rows), :] = (da * u * (sg * (1.0 + g * (1.0 - sg)))).astype(BF16)
            out_refs[1][pl.ds(r0, rows), :] = (da * (g * sg)).astype(BF16)

    t_spec = pl.BlockSpec((None, tm, fs), lambda j, m, k: (j, m, 0))
    o_shape = S((N_DEV, n, fs), BF16)
    return _matmul("dact_bwd", (N_DEV, n // tm, d // tk), 2, (dh, wd_all, gate, up),
                   [pl.BlockSpec((tm, tk), lambda j, m, k: (m, k)), pl.BlockSpec((None, fs, tk), lambda j, m, k: (j, 0, k)), t_spec, t_spec],
                   [(0, 1, 0)], NT, [(tm, fs)], [o_shape] * 2, [t_spec] * 2, epilogue)


def _dwd_bwd(act, dh):
    _, n, fs = act.shape
    d = dh.shape[1]
    tm, tn = _tile(n, 512), _tile(d, 1024)
    return _matmul("dwd_bwd", (N_DEV, d // tn, n // tm), 2, (act, dh),
                   [pl.BlockSpec((None, tm, fs), lambda j, c, t: (j, t, 0)), pl.BlockSpec((tm, tn), lambda j, c, t: (t, c))],
                   [(0, 1, 0)], TN, [(fs, tn)], [S((N_DEV, fs, d), BF16)], [pl.BlockSpec((None, fs, tn), lambda j, c, t: (j, 0, c))],
                   _store_epilogue)[0]


def _dw_cols_bwd(name, hn, dys):
    n, d = hn.shape
    ws = dys[0].shape[2]
    tm, tk = _tile(n, 512), _tile(d, 1024)
    k_out = len(dys)
    y_spec = pl.BlockSpec((None, tm, ws), lambda j, c, t: (j, t, 0))
    o_spec = pl.BlockSpec((None, tk, ws), lambda j, c, t: (j, c, 0))
    return _matmul(name, (N_DEV, d // tk, n // tm), 2, (hn, *dys),
                   [pl.BlockSpec((tm, tk), lambda j, c, t: (t, c))] + [y_spec] * k_out,
                   [(0, 1 + i, i) for i in range(k_out)], TN, [(tk, ws)] * k_out, [S((N_DEV, d, ws), BF16)] * k_out, [o_spec] * k_out,
                   _store_epilogue)


def _dwin_bwd(hn, dproj):
    n, d = hn.shape
    ws = dproj.shape[1] // N_DEV
    tm, tk = _tile(n, 512), _tile(d, 1024)
    return _matmul("dwin_bwd", (N_DEV, d // tk, n // tm), 2, (hn, dproj),
                   [pl.BlockSpec((tm, tk), lambda j, c, t: (t, c)), pl.BlockSpec((tm, ws), lambda j, c, t: (t, j))],
                   [(0, 1, 0)], TN, [(tk, ws)], [S((N_DEV, d, ws), BF16)], [pl.BlockSpec((None, tk, ws), lambda j, c, t: (j, c, 0))],
                   _store_epilogue)[0]


def _dhn_from_shards(name, dys, ws_all, dy_is_flat):
    if dy_is_flat:
        n, ws = dys[0].shape[0], dys[0].shape[1] // N_DEV
    else:
        _, n, ws = dys[0].shape
    d = ws_all[0].shape[1]
    tm, tn = _tile(n, 1024), _tile(d, 1024)
    if dy_is_flat:
        y_spec = pl.BlockSpec((tm, ws), lambda i, j, s: (i, s))
    else:
        y_spec = pl.BlockSpec((None, tm, ws), lambda i, j, s: (s, i, 0))
    w_spec = pl.BlockSpec((None, tn, ws), lambda i, j, s: (s, j, 0))
    k_terms = len(dys)
    return _matmul(name, (n // tm, d // tn, N_DEV), 2, (*dys, *ws_all), [y_spec] * k_terms + [w_spec] * k_terms,
                   [(i, k_terms + i, 0) for i in range(k_terms)], NT, [(tm, tn)], [S((n, d), F32)],
                   [pl.BlockSpec((tm, tn), lambda i, j, s: (i, j))], _store_epilogue)[0]


def _rms_fwd(name, x, g):
    n, d = x.shape
    tm = _tile(n, 256)

    def body(x_ref, g_ref, o_ref):
        xv = x_ref[...]
        r = lax.rsqrt(jnp.mean(xv * xv, axis=-1, keepdims=True) + RMS_EPS)
        o_ref[...] = (xv * r * g_ref[...]).astype(BF16)

    return pl.pallas_call(body, name=name, grid=(n // tm,),
                          in_specs=[pl.BlockSpec((tm, d), lambda i: (i, 0)), pl.BlockSpec((1, d), lambda i: (0, 0))],
                          out_specs=pl.BlockSpec((tm, d), lambda i: (i, 0)), out_shape=S((n, d), BF16), compiler_params=_cp(1))(x, g)


def _rms_bwd(name, x, g, dy, res):
    n, d = x.shape
    tm = _tile(n, 256)

    def body(x_ref, g_ref, dy_ref, res_ref, dx_ref, dg_ref):
        xv, dyv = x_ref[...], dy_ref[...]
        r = lax.rsqrt(jnp.mean(xv * xv, axis=-1, keepdims=True) + RMS_EPS)
        xr = xv * r
        dgy = dyv * g_ref[...]
        dx_ref[...] = res_ref[...] + r * (dgy - xr * jnp.mean(dgy * xr, axis=-1, keepdims=True))

        @pl.when(pl.program_id(0) == 0)
        def _():
            dg_ref[...] = jnp.zeros_like(dg_ref)

        dg_ref[...] += jnp.sum(dyv * xr, axis=0, keepdims=True)

    row = pl.BlockSpec((tm, d), lambda i: (i, 0))
    vec = pl.BlockSpec((1, d), lambda i: (0, 0))
    return pl.pallas_call(body, name=name, grid=(n // tm,), in_specs=[row, vec, row, row], out_specs=[row, vec],
                          out_shape=[S((n, d), F32), S((1, d), F32)], compiler_params=_cp(1))(x, g, dy, res)


def _final_loss(h, g, target):
    n, d = h.shape
    tm = _tile(n, 256)

    def body(h_ref, g_ref, t_ref, dh_ref, dhb_ref, dg_ref, sse_ref):
        hv, gv = h_ref[...], g_ref[...]
        r = lax.rsqrt(jnp.mean(hv * hv, axis=-1, keepdims=True) + RMS_EPS)
        hr = hv * r
        err = hr * gv - t_ref[...]
        dy = err * (1.0 / d)
        dgy = dy * gv
        dh = r * (dgy - hr * jnp.mean(dgy * hr, axis=-1, keepdims=True))
        dh_ref[...] = dh
        dhb_ref[...] = dh.astype(BF16)

        @pl.when(pl.program_id(0) == 0)
        def _():
            dg_ref[...] = jnp.zeros_like(dg_ref)
            sse_ref[...] = jnp.zeros_like(sse_ref)

        dg_ref[...] += jnp.sum(dy * hr, axis=0, keepdims=True)
        sse_ref[...] += jnp.sum(err * err)

    row = pl.BlockSpec((tm, d), lambda i: (i, 0))
    vec = pl.BlockSpec((1, d), lambda i: (0, 0))
    one = pl.BlockSpec((8, HEAD_DIM), lambda i: (0, 0))
    return pl.pallas_call(body, name="final_loss", grid=(n // tm,), in_specs=[row, vec, row], out_specs=[row, row, vec, one],
                          out_shape=[S((n, d), F32), S((n, d), BF16), S((1, d), F32), S((8, HEAD_DIM), F32)],
                          compiler_params=_cp(1))(h, g, target)


def _dot(a, b, dims):
    return lax.dot_general(a, b, dims, preferred_element_type=F32)


def _split3(x):
    hi = x.astype(BF16)
    r1 = x - hi.astype(F32)
    mid = r1.astype(BF16)
    lo = (r1 - mid.astype(F32)).astype(BF16)
    return hi, mid, lo


def _scan_cols(x, tri, terms):
    parts = _split3(x)[:terms]
    out = _dot(parts[0], tri, NN)
    for p in parts[1:]:
        out = out + _dot(p, tri, NN)
    return out


def _head_norm_fwd(o, g):
    r = lax.rsqrt(jnp.mean(o * o, axis=-1, keepdims=True) + RMS_EPS)
    return o * r * g


def _head_norm_bwd(o, g, dy):
    r = lax.rsqrt(jnp.mean(o * o, axis=-1, keepdims=True) + RMS_EPS)
    orr = o * r
    dgy = dy * g
    return r * (dgy - orr * jnp.mean(dgy * orr, axis=-1, keepdims=True)), dy * orr


def _interleave_plan(t):
    return [(i, dil, t // dil) for i, dil in enumerate(DILATIONS)]


def _band_mask(u, blocks_per_seq):
    row = lax.broadcasted_iota(jnp.int32, (BLK, 2 * BLK), 0)
    col = lax.broadcasted_iota(jnp.int32, (BLK, 2 * BLK), 1)
    dist = row + BLK - col
    has_prev = (u % blocks_per_seq) != 0
    return (dist >= 0) & (dist <= BLK) & ((col >= BLK) | has_prev)


def _dil_fwd(proj, g_dil, bl, t, n_heads):
    n = bl * t
    nb = t // BLK
    scale = HEAD_DIM ** -0.5
    plan = _interleave_plan(t)
    chunk = _tile(t, 256)

    def body(q_ref, k_ref, v_ref, g_ref, omix_ref, opre_ref, lse_ref, stg, qd, kd, vd, ob, lb, on, ln):
        for src, dst, pad in ((q_ref, qd, 0), (k_ref, kd, BLK), (v_ref, vd, BLK)):
            stg[...] = src[...].astype(F32)
            for bi, dil, sub in plan:
                if pad:
                    dst[bi, pl.ds(0, BLK), :] = jnp.zeros((BLK, HEAD_DIM), BF16)
                if dil == 1:
                    dst[bi, pl.ds(pad, t), :] = src[...]
                else:
                    for r in range(dil):
                        dst[bi, pl.ds(pad + r * sub, sub), :] = stg[pl.ds(r, sub, stride=dil), :].astype(BF16)

        for bi, dil, sub in plan:
            def blk(u, carry, bi=bi, sub=sub):
                rows = pl.ds(pl.multiple_of(u * BLK, BLK), BLK)
                win = pl.ds(pl.multiple_of(u * BLK, BLK), 2 * BLK)
                sc = _dot(qd[bi, rows, :], kd[bi, win, :], NT) * scale
                sc = jnp.where(_band_mask(u, sub // BLK), sc, NEG)
                m = jnp.max(sc, axis=-1, keepdims=True)
                p = jnp.exp(sc - m)
                den = jnp.sum(p, axis=-1, keepdims=True)
                ob[bi, rows, :] = _dot((p / den).astype(BF16), vd[bi, win, :], NN)
                lb[bi, rows, :] = jnp.broadcast_to(m + jnp.log(den), (BLK, HEAD_DIM))
                return carry
            lax.fori_loop(0, nb // DIL_UNROLL, lambda i, c, blk=blk: [blk(i * DIL_UNROLL + s, c) for s in range(DIL_UNROLL)][-1], 0)

        for bi, dil, sub in plan[1:]:
            for r in range(dil):
                on[bi - 1, pl.ds(r, sub, stride=dil), :] = ob[bi, pl.ds(r * sub, sub), :]
                ln[bi - 1, pl.ds(r, sub, stride=dil), :] = lb[bi, pl.ds(r * sub, sub), :]

        def merge(i, carry):
            rows = pl.ds(pl.multiple_of(i * chunk, chunk), chunk)
            l0, l1, l2 = lb[0, rows, :], ln[0, rows, :], ln[1, rows, :]
            mx = jnp.maximum(jnp.maximum(l0, l1), l2)
            w0, w1, w2 = jnp.exp(l0 - mx), jnp.exp(l1 - mx), jnp.exp(l2 - mx)
            tot = w0 + w1 + w2
            o = (w0 / tot) * ob[0, rows, :] + (w1 / tot) * on[0, rows, :] + (w2 / tot) * on[1, rows, :]
            lse_ref[rows, :] = mx + jnp.log(tot)
            opre_ref[rows, :] = o
            omix_ref[rows, :] = _head_norm_fwd(o, g_ref[...]).astype(BF16)
            return carry
        lax.fori_loop(0, t // chunk, merge, 0)

    hs = n_heads
    col = lambda off: pl.BlockSpec((t, HEAD_DIM), lambda b, h: (b, off + h))
    return pl.pallas_call(
        body, name="dil_fwd", grid=(bl, hs),
        in_specs=[col(0), col(hs), col(2 * hs), pl.BlockSpec((1, HEAD_DIM), lambda b, h: (0, h))],
        out_specs=[col(0), col(0), pl.BlockSpec((t, HEAD_DIM), lambda b, h: (b * hs + h, 0))],
        out_shape=[S((n, 2 * hs * HEAD_DIM), BF16), S((n, 2 * hs * HEAD_DIM), F32), S((bl * hs * t, HEAD_DIM), F32)],
        scratch_shapes=[pltpu.VMEM((t, HEAD_DIM), F32), pltpu.VMEM((3, t, HEAD_DIM), BF16),
                        pltpu.VMEM((3, t + BLK, HEAD_DIM), BF16), pltpu.VMEM((3, t + BLK, HEAD_DIM), BF16),
                        pltpu.VMEM((3, t, HEAD_DIM), F32), pltpu.VMEM((3, t, HEAD_DIM), F32),
                        pltpu.VMEM((2, t, HEAD_DIM), F32), pltpu.VMEM((2, t, HEAD_DIM), F32)],
        compiler_params=_cp(2),
    )(proj, proj, proj, g_dil)


def _dil_bwd(proj, opre, lse, d_omix, g_dil, rc, rs1, rs2, bl, t, n_heads):
    n = bl * t
    nb = t // BLK
    scale = HEAD_DIM ** -0.5
    plan = _interleave_plan(t)
    chunk = _tile(t, 256)

    def body(q_ref, k_ref, v_ref, opre_ref, lse_ref, dy_ref, g_ref, c_ref, s1_ref, s2_ref, out_ref, dg_ref,
             stg, qd, kd, vd, dod, ldd, dqd, dkc, dkp, dvc, dvp, sk, sv):
        which = pl.program_id(2)

        @pl.when(jnp.logical_and(which == 0, pl.program_id(1) == 0))
        def _():
            dg_ref[...] = jnp.zeros_like(dg_ref)

        @pl.when(which == 0)
        def _():
            def prep(i, dg):
                rows = pl.ds(pl.multiple_of(i * chunk, chunk), chunk)
                o = opre_ref[rows, :]
                d_o, dg_rows = _head_norm_bwd(o, g_ref[...], dy_ref[rows, :])
                stg[rows, :] = d_o
                lane = lax.broadcasted_iota(jnp.int32, (chunk, HEAD_DIM), 1)
                ldd[0, rows, :] = jnp.where(lane < HEAD_DIM // 2, lse_ref[rows, :], jnp.sum(d_o * o, axis=-1, keepdims=True))
                return dg + jnp.sum(dg_rows, axis=0, keepdims=True)
            dg_ref[...] += lax.fori_loop(0, t // chunk, prep, jnp.zeros((1, HEAD_DIM), F32))

            dod[0] = stg[...].astype(BF16)
            for bi, dil, sub in plan[1:]:
                for r in range(dil):
                    dst = pl.ds(r * sub, sub)
                    dod[bi, dst, :] = stg[pl.ds(r, sub, stride=dil), :].astype(BF16)
                    ldd[bi, dst, :] = ldd[0, pl.ds(r, sub, stride=dil), :]
            for src, dst, pad in ((q_ref, qd, 0), (k_ref, kd, BLK), (v_ref, vd, BLK)):
                stg[...] = src[...].astype(F32)
                for bi, dil, sub in plan:
                    if pad:
                        dst[bi, pl.ds(0, BLK), :] = jnp.zeros((BLK, HEAD_DIM), BF16)
                    if dil == 1:
                        dst[bi, pl.ds(pad, t), :] = src[...]
                    else:
                        for r in range(dil):
                            dst[bi, pl.ds(pad + r * sub, sub), :] = stg[pl.ds(r, sub, stride=dil), :].astype(BF16)

            for bi, dil, sub in plan:
                def blk(u, carry, bi=bi, sub=sub):
                    rows = pl.ds(pl.multiple_of(u * BLK, BLK), BLK)
                    win = pl.ds(pl.multiple_of(u * BLK, BLK), 2 * BLK)
                    qb, kw, vw, dob = qd[bi, rows, :], kd[bi, win, :], vd[bi, win, :], dod[bi, rows, :]
                    sc = _dot(qb, kw, NT) * scale
                    stats = ldd[bi, rows, :]
                    p = jnp.where(_band_mask(u, sub // BLK), jnp.exp(sc - stats[:, :1]), 0.0)
                    dp = _dot(dob, vw, NT)
                    ds = (p * (dp - stats[:, HEAD_DIM // 2:HEAD_DIM // 2 + 1]) * scale).astype(BF16)
                    dqd[bi, rows, :] = _dot(ds, kw, NN)
                    dk_win = _dot(ds, qb, TN)
                    dv_win = _dot(p.astype(BF16), dob, TN)
                    dkp[bi, rows, :] = dk_win[:BLK]
                    dkc[bi, rows, :] = dk_win[BLK:]
                    dvp[bi, rows, :] = dv_win[:BLK]
                    dvc[bi, rows, :] = dv_win[BLK:]
                    return carry
                lax.fori_loop(0, nb // DIL_UNROLL, lambda i, c, blk=blk: [blk(i * DIL_UNROLL + s, c) for s in range(DIL_UNROLL)][-1], 0)

            for cur, prev, undo_rope, dst in ((dqd, None, True, out_ref), (dkc, dkp, True, sk), (dvc, dvp, False, sv)):
                def summed(bi, start, size, cur=cur, prev=prev):
                    v = cur[bi, pl.ds(start, size), :]
                    if prev is None:
                        return v
                    if start + size < t:
                        return v + prev[bi, pl.ds(start + BLK, size), :]
                    if size == BLK:
                        return v
                    return v + jnp.concatenate([prev[bi, pl.ds(start + BLK, size - BLK), :], jnp.zeros((BLK, HEAD_DIM), F32)], axis=0)
                stg[...] = summed(0, 0, t)
                for bi, dil, sub in plan[1:]:
                    for r in range(dil):
                        stg[pl.ds(r, sub, stride=dil), :] += summed(bi, r * sub, sub)
                if undo_rope:
                    dst[...] = _rope_bwd(stg[...], c_ref[...], s1_ref[...], s2_ref[...]).astype(BF16)
                else:
                    dst[...] = stg[...].astype(BF16)

        @pl.when(which == 1)
        def _():
            out_ref[...] = sk[...]

        @pl.when(which == 2)
        def _():
            out_ref[...] = sv[...]

    hs = n_heads
    col = lambda off: pl.BlockSpec((t, HEAD_DIM), lambda h, b, w: (b, off + h))
    per_head = pl.BlockSpec((t, HEAD_DIM), lambda h, b, w: (b * hs + h, 0))
    tab = pl.BlockSpec((t, HEAD_DIM), lambda h, b, w: (b, 0))
    gvec = pl.BlockSpec((1, HEAD_DIM), lambda h, b, w: (0, h))
    tb = (t, HEAD_DIM)
    tp = (t + BLK, HEAD_DIM)
    return pl.pallas_call(
        body, name="dil_bwd", grid=(hs, bl, 3),
        in_specs=[col(0), col(hs), col(2 * hs), col(0), per_head, col(0), gvec, tab, tab, tab],
        out_specs=[pl.BlockSpec((t, HEAD_DIM), lambda h, b, w: (b, w * hs + h)), gvec],
        out_shape=[S((n, 6 * hs * HEAD_DIM), BF16), S((1, hs * HEAD_DIM), F32)],
        scratch_shapes=[pltpu.VMEM(tb, F32), pltpu.VMEM((3,) + tb, BF16), pltpu.VMEM((3,) + tp, BF16), pltpu.VMEM((3,) + tp, BF16),
                        pltpu.VMEM((3,) + tb, BF16), pltpu.VMEM((3,) + tb, F32), pltpu.VMEM((3,) + tb, F32),
                        pltpu.VMEM((3,) + tb, F32), pltpu.VMEM((3,) + tb, F32), pltpu.VMEM((3,) + tb, F32), pltpu.VMEM((3,) + tb, F32),
                        pltpu.VMEM(tb, BF16), pltpu.VMEM(tb, BF16)],
        compiler_params=_cp(3),
    )(proj, proj, proj, opre, lse, d_omix, g_dil, rc, rs1, rs2)


def _sb_tile(qb, kb, scale, diag):
    z = _dot(qb, kb, NT) * scale
    tl = jnp.log1p(jnp.exp(-jnp.abs(z)))
    log_not = -(jnp.maximum(z, 0.0) + tl)
    strict = None
    if diag:
        row = lax.broadcasted_iota(jnp.int32, (SBT, SBT), 0)
        col = lax.broadcasted_iota(jnp.int32, (SBT, SBT), 1)
        strict = col < row
        log_not = jnp.where(strict, log_not, 0.0)
    return log_not, jnp.minimum(z, 0.0) - tl, strict


def _tri(cmp):
    row = lax.broadcasted_iota(jnp.int32, (SBT, SBT), 0)
    col = lax.broadcasted_iota(jnp.int32, (SBT, SBT), 1)
    return jnp.where(cmp(row, col), 1.0, 0.0).astype(BF16)


def _sb_fwd(proj, g_sb, omix_in, opre_in, bl, t, n_heads):
    nb = t // SBT
    scale = HEAD_DIM ** -0.5

    def body(q_ref, k_ref, v_ref, g_ref, _omix_in, _opre_in, omix_ref, opre_ref, lt_ref):
        later = _tri(lambda r, c: r > c)

        def q_block(qi, carry):
            rows = pl.ds(pl.multiple_of(qi * SBT, SBT), SBT)
            qb = q_ref[rows, :]

            def tile(kj, run, acc, diag):
                krows = pl.ds(pl.multiple_of(kj * SBT, SBT), SBT)
                log_not, log_beta, strict = _sb_tile(qb, k_ref[krows, :], scale, diag)
                a = jnp.exp(log_beta + _scan_cols(log_not, later, 2) + run)
                if diag:
                    a = jnp.where(strict, a, 0.0)
                return run + jnp.sum(log_not, axis=-1, keepdims=True), acc + _dot(a.astype(BF16), v_ref[krows, :], NN)

            st = tile(qi, jnp.zeros((SBT, 1), F32), jnp.zeros((SBT, HEAD_DIM), F32), True)
            run, acc = lax.fori_loop(0, qi, lambda it, st: tile(qi - 1 - it, st[0], st[1], False), st)
            lt_ref[rows, :] = jnp.broadcast_to(run, (SBT, HEAD_DIM))
            opre_ref[rows, :] = acc
            omix_ref[rows, :] = _head_norm_fwd(acc, g_ref[...]).astype(BF16)
            return carry

        lax.fori_loop(0, nb, q_block, 0)

    hs = n_heads
    col = lambda off: pl.BlockSpec((t, HEAD_DIM), lambda b, h: (b, off + h))
    return pl.pallas_call(
        body, name="sb_fwd", grid=(bl, hs),
        in_specs=[col(3 * hs), col(4 * hs), col(5 * hs), pl.BlockSpec((1, HEAD_DIM), lambda b, h: (0, h)), HBM_SPEC, HBM_SPEC],
        out_specs=[col(hs), col(hs), pl.BlockSpec((t, HEAD_DIM), lambda b, h: (b * hs + h, 0))],
        out_shape=[S(omix_in.shape, BF16), S(opre_in.shape, F32), S((bl * hs * t, HEAD_DIM), F32)],
        input_output_aliases={4: 0, 5: 1}, compiler_params=_cp(2),
    )(proj, proj, proj, g_sb, omix_in, opre_in)


def _sb_bwd(proj, opre, ltot, d_omix, g_sb, dproj_in, bl, t, n_heads):
    nb = t // SBT
    scale = HEAD_DIM ** -0.5

    def body(q_ref, k_ref, v_ref, opre_ref, lt_ref, dy_ref, g_ref, _dproj_in, out_ref, dg_ref, dq, dk, dv):
        which = pl.program_id(2)

        @pl.when(jnp.logical_and(which == 0, pl.program_id(1) == 0))
        def _():
            dg_ref[...] = jnp.zeros_like(dg_ref)

        @pl.when(which == 0)
        def _():
            upto = _tri(lambda r, c: r <= c)
            before = _tri(lambda r, c: r < c)
            dk[...] = jnp.zeros_like(dk)
            dv[...] = jnp.zeros_like(dv)

            def q_block(qi, dg):
                rows = pl.ds(pl.multiple_of(qi * SBT, SBT), SBT)
                qb = q_ref[rows, :]
                o = opre_ref[rows, :]
                d_o, dg_rows = _head_norm_bwd(o, g_ref[...], dy_ref[rows, :])
                dob = d_o.astype(BF16)
                lt = lt_ref[rows, :][:, :1]

                def tile(kj, run, grun, dq_acc, diag):
                    krows = pl.ds(pl.multiple_of(kj * SBT, SBT), SBT)
                    kb, vb = k_ref[krows, :], v_ref[krows, :]
                    log_not, log_beta, strict = _sb_tile(qb, kb, scale, diag)
                    excl = lt - (run + _scan_cols(log_not, upto, 2))
                    a = jnp.exp(log_beta + excl)
                    if diag:
                        a = jnp.where(strict, a, 0.0)
                    g_a = a * _dot(dob, vb, NT)
                    g_before = grun + _scan_cols(g_a, before, 2)
                    dz = (g_a - (g_a + g_before) * jnp.exp(log_beta)) * scale
                    if diag:
                        dz = jnp.where(strict, dz, 0.0)
                    dzb = dz.astype(BF16)
                    dk[krows, :] += _dot(dzb, qb, TN)
                    dv[krows, :] += _dot(a.astype(BF16), dob, TN)
                    return (run + jnp.sum(log_not, axis=-1, keepdims=True), grun + jnp.sum(g_a, axis=-1, keepdims=True),
                            dq_acc + _dot(dzb, kb, NN))

                zero_col = jnp.zeros((SBT, 1), F32)
                st = lax.fori_loop(0, qi, lambda kj, st: tile(kj, st[0], st[1], st[2], False),
                                   (zero_col, zero_col, jnp.zeros((SBT, HEAD_DIM), F32)))
                _, _, dq_acc = tile(qi, st[0], st[1], st[2], True)
                dq[rows, :] = dq_acc.astype(BF16)
                return dg + jnp.sum(dg_rows, axis=0, keepdims=True)

            dg_ref[...] += lax.fori_loop(0, nb, q_block, jnp.zeros((1, HEAD_DIM), F32))
            out_ref[...] = dq[...]

        @pl.when(which == 1)
        def _():
            out_ref[...] = dk[...].astype(BF16)

        @pl.when(which == 2)
        def _():
            out_ref[...] = dv[...].astype(BF16)

    hs = n_heads
    col = lambda off: pl.BlockSpec((t, HEAD_DIM), lambda h, b, w: (b, off + h))
    per_head = pl.BlockSpec((t, HEAD_DIM), lambda h, b, w: (b * hs + h, 0))
    gvec = pl.BlockSpec((1, HEAD_DIM), lambda h, b, w: (0, h))
    tb = (t, HEAD_DIM)
    return pl.pallas_call(
        body, name="sb_bwd", grid=(hs, bl, 3),
        in_specs=[col(3 * hs), col(4 * hs), col(5 * hs), col(hs), per_head, col(hs), gvec, HBM_SPEC],
        out_specs=[pl.BlockSpec((t, HEAD_DIM), lambda h, b, w: (b, (3 + w) * hs + h)), gvec],
        out_shape=[S(dproj_in.shape, BF16), S((1, hs * HEAD_DIM), F32)],
        scratch_shapes=[pltpu.VMEM(tb, BF16), pltpu.VMEM(tb, F32), pltpu.VMEM(tb, F32)],
        input_output_aliases={7: 0}, compiler_params=_cp(3),
    )(proj, proj, proj, opre, ltot, d_omix, g_sb, dproj_in)


def _all_gather(shards):
    k_w = len(shards)

    def body(*refs):
        ins, outs = refs[:k_w], refs[k_w:2 * k_w]
        send_sems, recv_sems, local_sems = refs[2 * k_w:]
        x, y, c = lax.axis_index("x"), lax.axis_index("y"), lax.axis_index("c")
        me, sibling = (x, y, c), (x, y, 1 - c)
        chips = [(1 - x, y), (x, 1 - y), (1 - x, 1 - y)]

        def slot(dev):
            return 4 * dev[0] + 2 * dev[1] + dev[2]

        def copy(w, k, block, to, src=None):
            dst = outs[w].at[slot(block)]
            return pltpu.make_async_remote_copy(
                src_ref=dst if src is None else src, dst_ref=dst, send_sem=send_sems.at[w * 7 + k], recv_sem=recv_sems.at[w * 7 + k],
                device_id=to, device_id_type=MESH)

        mine = [pltpu.make_async_copy(ins[w], outs[w].at[slot(me)], local_sems.at[w]) for w in range(k_w)]
        first = []
        for w in range(k_w):
            mine[w].start()
            first.append(copy(w, 0, me, sibling, src=ins[w]))
            first += [copy(w, 1 + j, me, (*chip, c), src=ins[w]) for j, chip in enumerate(chips)]
        for cp in first:
            cp.start()
        passed = []
        for w in range(k_w):
            for j, chip in enumerate(chips):
                copy(w, 1 + j, (*chip, c), me).wait_recv()
                fwd = copy(w, 4 + j, (*chip, c), sibling)
                fwd.start()
                passed.append(fwd)
        for w in range(k_w):
            copy(w, 0, sibling, me).wait_recv()
            for j, chip in enumerate(chips):
                copy(w, 4 + j, (*chip, 1 - c), me).wait_recv()
        for cp in first + passed:
            cp.wait_send()
        for cp in mine:
            cp.wait()

    return pl.pallas_call(
        body, name="weights_all_gather", in_specs=[HBM_SPEC] * k_w, out_specs=[HBM_SPEC] * k_w,
        out_shape=[S((N_DEV,) + s.shape, s.dtype) for s in shards],
        scratch_shapes=[pltpu.SemaphoreType.DMA((7 * k_w,)), pltpu.SemaphoreType.DMA((7 * k_w,)), pltpu.SemaphoreType.DMA((k_w,))],
    )(*shards)


def _grad_exchange(grads):
    k_w = len(grads)

    def body(*refs):
        ins, outs = refs[:k_w], refs[k_w:2 * k_w]
        send_sems, recv_sems, local_sems = refs[2 * k_w:]
        x, y, c = lax.axis_index("x"), lax.axis_index("y"), lax.axis_index("c")
        my_slot = 4 * x + 2 * y + c
        copies = []
        for w in range(k_w):
            own = pltpu.make_async_copy(ins[w].at[my_slot], outs[w].at[my_slot], local_sems.at[w])
            own.start()
            copies.append(own)
        remote = []
        for w in range(k_w):
            for k in range(1, N_DEV):
                px, py, pc = x ^ (k >> 2), y ^ ((k >> 1) & 1), c ^ (k & 1)
                cp = pltpu.make_async_remote_copy(
                    src_ref=ins[w].at[4 * px + 2 * py + pc], dst_ref=outs[w].at[my_slot],
                    send_sem=send_sems.at[w * 7 + k - 1], recv_sem=recv_sems.at[w * 7 + k - 1],
                    device_id=(px, py, pc), device_id_type=MESH)
                cp.start()
                remote.append(cp)
        for cp in remote:
            cp.wait_send()
        for w in range(k_w):
            for k in range(1, N_DEV):
                px, py, pc = x ^ (k >> 2), y ^ ((k >> 1) & 1), c ^ (k & 1)
                peer_slot = 4 * px + 2 * py + pc
                pltpu.make_async_remote_copy(
                    src_ref=ins[w].at[my_slot], dst_ref=outs[w].at[peer_slot],
                    send_sem=send_sems.at[w * 7 + k - 1], recv_sem=recv_sems.at[w * 7 + k - 1],
                    device_id=(px, py, pc), device_id_type=MESH).wait_recv()
        for cp in copies:
            cp.wait()

    return pl.pallas_call(
        body, name="grad_exchange", in_specs=[HBM_SPEC] * k_w, out_specs=[HBM_SPEC] * k_w,
        out_shape=[S(g.shape, g.dtype) for g in grads],
        scratch_shapes=[pltpu.SemaphoreType.DMA((7 * k_w,)), pltpu.SemaphoreType.DMA((7 * k_w,)), pltpu.SemaphoreType.DMA((k_w,))],
    )(*grads)


def _adamw(name, parts, w, m, v):
    r, c = w.shape
    tr = _tile(r, max(16, (1 << 19) // c // 16 * 16))

    def body(p_ref, w_ref, m_ref, v_ref, g_ref, d_ref, nm_ref, nv_ref):
        g = p_ref[0].astype(F32)
        for s in range(1, N_DEV):
            g = g + p_ref[s].astype(F32)
        m_new = ADAM_B1 * m_ref[...] + (1.0 - ADAM_B1) * g
        v_new = ADAM_B2 * v_ref[...] + (1.0 - ADAM_B2) * jnp.square(g)
        m_hat = m_new / (1.0 - ADAM_B1 ** ADAM_STEP)
        v_hat = v_new / (1.0 - ADAM_B2 ** ADAM_STEP)
        g_ref[...] = g
        d_ref[...] = -ADAM_LR * (m_hat / (jnp.sqrt(v_hat) + ADAM_EPS) + ADAM_WD * w_ref[...])
        nm_ref[...] = m_new
        nv_ref[...] = v_new

    blk = pl.BlockSpec((tr, c), lambda i: (i, 0))
    return pl.pallas_call(body, name=name, grid=(r // tr,), in_specs=[pl.BlockSpec((N_DEV, tr, c), lambda i: (0, i, 0)), blk, blk, blk],
                          out_specs=[blk] * 4, out_shape=[S((r, c), F32)] * 4, compiler_params=_cp(1))(parts, w, m, v)


def _rope_tables(positions):
    inv_freq = jnp.power(jnp.float32(ROPE_THETA), -jnp.arange(ROPE_HALF, dtype=F32) / ROPE_HALF)
    ang = positions.astype(F32).reshape(-1, 1) * inv_freq
    cos, sin = jnp.cos(ang), jnp.sin(ang)
    n = ang.shape[0]
    rest = HEAD_DIM - 2 * ROPE_HALF
    zeros = jnp.zeros((n, ROPE_HALF), F32)
    c = jnp.concatenate([cos, cos, jnp.ones((n, rest), F32)], axis=1)
    s1 = jnp.concatenate([zeros, sin, jnp.zeros((n, rest), F32)], axis=1)
    s2 = jnp.concatenate([-sin, zeros, jnp.zeros((n, rest), F32)], axis=1)
    return c, s1, s2


def kernel(x, positions, norm_mix_g, w_in, norm_out_dil_g, norm_out_sb_g, w_out, norm_ffn_g, w_gate, w_up, w_down, norm_final_g, loss_target, m_norm_mix_g, m_w_in, m_norm_out_dil_g, m_norm_out_sb_g, m_w_out, m_norm_ffn_g, m_w_gate, m_w_up, m_w_down, m_norm_final_g, v_norm_mix_g, v_w_in, v_norm_out_dil_g, v_norm_out_sb_g, v_w_out, v_norm_ffn_g, v_w_gate, v_w_up, v_w_down, v_norm_final_g):
    bl, t, d = x.shape
    n = bl * t
    hs = d // (2 * HEAD_DIM)
    x2 = x.reshape(n, d)
    target = loss_target.reshape(n, d)
    g_final = norm_final_g.reshape(1, d)
    rc, rs1, rs2 = _rope_tables(positions)

    win_all, wout_all, wg_all, wu_all, wd_all = _all_gather(
        [w_in[0].astype(BF16), w_out[0].astype(BF16), w_gate[0].astype(BF16), w_up[0].astype(BF16), w_down[0].astype(BF16)])
    wout_full = wout_all.reshape(d, d)

    hn1 = _rms_fwd("rms_mix_fwd", x2, norm_mix_g)
    proj = _proj_fwd(hn1, win_all, rc, rs1, rs2, 2 * hs)
    omix, opre, lse = _dil_fwd(proj, norm_out_dil_g, bl, t, hs)
    omix, opre, ltot = _sb_fwd(proj, norm_out_sb_g, omix, opre, bl, t, hs)
    h1 = _dense_res("out_fwd", omix, wout_full, x2, NN)
    hn2 = _rms_fwd("rms_ffn_fwd", h1, norm_ffn_g)
    gate, up, act = _gateup_fwd(hn2, wg_all, wu_all)
    h2 = _down_fwd(act, wd_all, h1)
    dh2, dh2b, dg_final, sse = _final_loss(h2, g_final, target)
    loss = lax.psum(sse[0, 0], ("x", "y", "c")) * (0.5 / d)

    dgate, dup = _dact_bwd(dh2b, wd_all, gate, up)
    dwd = _dwd_bwd(act, dh2b)
    dwg, dwu = _dw_cols_bwd("dwgu_bwd", hn2, (dgate, dup))
    dhn2 = _dhn_from_shards("dhn2_bwd", (dgate, dup), (wg_all, wu_all), False)
    dh1, dg_ffn = _rms_bwd("rms_ffn_bwd", h1, norm_ffn_g, dhn2, dh2)
    d_omix = _dense_res("domix_bwd", dh1, wout_full, None, NT)
    dwout = _tn_full("dwout_bwd", omix, dh1).reshape(N_DEV, d // N_DEV, d)
    dproj, dg_dil = _dil_bwd(proj, opre, lse, d_omix, norm_out_dil_g, rc, rs1, rs2, bl, t, hs)
    dproj, dg_sb = _sb_bwd(proj, opre, ltot, d_omix, norm_out_sb_g, dproj, bl, t, hs)
    dwin = _dwin_bwd(hn1, dproj)
    dhn1 = _dhn_from_shards("dhn1_bwd", (dproj,), (win_all,), True)
    dx, dg_mix = _rms_bwd("rms_mix_bwd", x2, norm_mix_g, dhn1, dh1)

    gains = [norm_mix_g, norm_out_dil_g, norm_out_sb_g, norm_ffn_g, g_final]
    m_gains = [m_norm_mix_g, m_norm_out_dil_g, m_norm_out_sb_g, m_norm_ffn_g, m_norm_final_g.reshape(1, d)]
    v_gains = [v_norm_mix_g, v_norm_out_dil_g, v_norm_out_sb_g, v_norm_ffn_g, v_norm_final_g.reshape(1, d)]
    dg_vec = jnp.concatenate([dg_mix, dg_dil, dg_sb, dg_ffn, dg_final], axis=1)
    dg_all = jnp.broadcast_to(dg_vec[None], (N_DEV,) + dg_vec.shape)
    rwin, rwout, rwg, rwu, rwd, rg = _grad_exchange([dwin, dwout, dwg, dwu, dwd, dg_all])

    out_w = {}
    for name, parts, w, m, v in (("w_in", rwin, w_in, m_w_in, v_w_in), ("w_out", rwout, w_out, m_w_out, v_w_out),
                                 ("w_gate", rwg, w_gate, m_w_gate, v_w_gate), ("w_up", rwu, w_up, m_w_up, v_w_up),
                                 ("w_down", rwd, w_down, m_w_down, v_w_down)):
        out_w[name] = [o[None] for o in _adamw("adamw_" + name, parts, w[0], m[0], v[0])]
    cat = lambda vs: jnp.concatenate(vs, axis=1)
    gain_out = _adamw("adamw_gains", rg, cat(gains), cat(m_gains), cat(v_gains))
    widths = [d, d // 2, d // 2, d]
    cuts = [sum(widths[:i + 1]) for i in range(4)]
    gain_split = [jnp.split(o, cuts, axis=1) for o in gain_out]

    def ordered(kind):
        gs = gain_split[kind]
        return (gs[0], out_w["w_in"][kind], gs[1], gs[2], out_w["w_out"][kind], gs[3], out_w["w_gate"][kind],
                out_w["w_up"][kind], out_w["w_down"][kind], gs[4].reshape(d))

    return (loss, dx.reshape(bl, t, d), *ordered(0), *ordered(1), *ordered(2), *ordered(3))
```

```python
import functools
import math

import jax
import jax.numpy as jnp
from jax import lax
from jax.experimental import pallas as pl
from jax.experimental.pallas import tpu as pltpu

F32 = jnp.float32
BF16 = jnp.bfloat16
S = jax.ShapeDtypeStruct

N_DEV = 8
HEAD_DIM = 128
BLK = 128
SBT = 256
DIL_UNROLL = 4
ROPE_HALF = 16
ROPE_THETA = 500000.0
RMS_EPS = 1e-5
DILATIONS = (1, 4, 16)
NEG = -1e30
VMEM_LIMIT = 56 * 1024 * 1024

ADAM_LR = 0.001
ADAM_B1 = 0.9
ADAM_B2 = 0.999
ADAM_EPS = 1e-08
ADAM_WD = 0.01
ADAM_STEP = 10

MESH = pl.DeviceIdType.MESH
HBM_SPEC = pl.BlockSpec(memory_space=pltpu.HBM)
SEM_SPEC = pl.BlockSpec(memory_space=pltpu.SEMAPHORE)
SIDE_EFFECT = pltpu.SideEffectType.DATAFLOW_SIDE_EFFECTING


def _cp(n_axes):
    return pltpu.CompilerParams(dimension_semantics=("arbitrary",) * n_axes, vmem_limit_bytes=VMEM_LIMIT)


def _tile(n, want):
    if n <= want:
        return n
    t = want
    while t >= 16:
        if n % t == 0 and t % 16 == 0:
            return t
        t -= 16
    return n


NN = (((1,), (0,)), ((), ()))
NT = (((1,), (1,)), ((), ()))
TN = (((0,), (0,)), ((), ()))


def _matmul(name, grid, red_axis, ins, in_specs, terms, dims, acc_shapes, out_shapes, out_specs, epilogue, after=None):
    if after is not None:
        ins, in_specs = (*ins, after), [*in_specs, pl.BlockSpec(memory_space=pl.ANY)]
    n_in, n_out = len(ins), len(out_shapes)
    n_red = grid[red_axis]

    def body(*refs):
        in_refs, out_refs, acc_refs = refs[:n_in], refs[n_in:n_in + n_out], refs[n_in + n_out:]
        k = pl.program_id(red_axis)

        @pl.when(k == 0)
        def _():
            for acc in acc_refs:
                acc[...] = jnp.zeros_like(acc)

        for a_idx, b_idx, acc_idx in terms:
            a = in_refs[a_idx][...].astype(BF16)
            b = in_refs[b_idx][...].astype(BF16)
            acc_refs[acc_idx][...] += lax.dot_general(a, b, dims, preferred_element_type=F32)

        @pl.when(k == n_red - 1)
        def _():
            epilogue(acc_refs, in_refs, out_refs)

    return pl.pallas_call(
        body, name=name, grid=grid, in_specs=in_specs, out_specs=out_specs, out_shape=out_shapes,
        scratch_shapes=[pltpu.VMEM(s, F32) for s in acc_shapes], compiler_params=_cp(len(grid)),
    )(*ins)


def _store_epilogue(acc_refs, in_refs, out_refs):
    for acc, out in zip(acc_refs, out_refs):
        out[...] = acc[...].astype(out.dtype)


def _rope_fwd(a, c, s1, s2):
    return a * c + pltpu.roll(a, ROPE_HALF, 1) * s1 + pltpu.roll(a, HEAD_DIM - ROPE_HALF, 1) * s2


def _rope_bwd(d, c, s1, s2):
    return d * c + pltpu.roll(d * s1, HEAD_DIM - ROPE_HALF, 1) + pltpu.roll(d * s2, ROPE_HALF, 1)


def _proj_fwd(hn, w_all, rc, rs1, rs2, n_rope_heads):
    n, d = hn.shape
    _, _, ws = w_all.shape
    tm, tk = _tile(n, 1024), _tile(d, 512)
    heads_per_shard = ws // HEAD_DIM
    rows = _tile(tm, 256)

    def epilogue(acc_refs, in_refs, out_refs):
        acc, out = acc_refs[0], out_refs[0]
        j = pl.program_id(0)
        for r0 in range(0, tm, rows):
            c, s1, s2 = (ref[pl.ds(r0, rows), :] for ref in in_refs[2:5])
            for hh in range(heads_per_shard):
                a = acc[pl.ds(r0, rows), pl.ds(hh * HEAD_DIM, HEAD_DIM)]
                roped = _rope_fwd(a, c, s1, s2)
                a = jnp.where(j * heads_per_shard + hh < n_rope_heads, roped, a)
                out[pl.ds(r0, rows), pl.ds(hh * HEAD_DIM, HEAD_DIM)] = a.astype(out.dtype)

    tab = pl.BlockSpec((tm, HEAD_DIM), lambda j, m, k: (m, 0))
    return _matmul(
        "proj_fwd", (N_DEV, n // tm, d // tk), 2, (hn, w_all, rc, rs1, rs2),
        [pl.BlockSpec((tm, tk), lambda j, m, k: (m, k)), pl.BlockSpec((None, tk, ws), lambda j, m, k: (j, k, 0)), tab, tab, tab],
        [(0, 1, 0)], NN, [(tm, ws)], [S((n, N_DEV * ws), BF16)], [pl.BlockSpec((tm, ws), lambda j, m, k: (m, j))], epilogue)[0]


def _dense_res(name, a, b, res, dims, out_dtype=F32):
    m, kdim = a.shape
    n = b.shape[1] if dims == NN else b.shape[0]
    tm, tn, tk = _tile(m, 1024), _tile(n, 1024), _tile(kdim, 512)
    ins = [a, b] + ([res] if res is not None else [])
    b_spec = pl.BlockSpec((tk, tn), lambda i, j, k: (k, j)) if dims == NN else pl.BlockSpec((tn, tk), lambda i, j, k: (j, k))
    specs = [pl.BlockSpec((tm, tk), lambda i, j, k: (i, k)), b_spec]
    if res is not None:
        specs.append(pl.BlockSpec((tm, tn), lambda i, j, k: (i, j)))

    def epilogue(acc_refs, in_refs, out_refs):
        v = acc_refs[0][...]
        if res is not None:
            v = v + in_refs[2][...]
        out_refs[0][...] = v.astype(out_dtype)

    return _matmul(name, (m // tm, n // tn, kdim // tk), 2, ins, specs, [(0, 1, 0)], dims, [(tm, tn)],
                   [S((m, n), out_dtype)], [pl.BlockSpec((tm, tn), lambda i, j, k: (i, j))], epilogue)[0]


def _tn_full(name, a, b, out_dtype=BF16):
    m, kdim = a.shape
    n = b.shape[1]
    tm, tk, tn = _tile(m, 512), _tile(kdim, 1024), _tile(n, 1024)
    return _matmul(name, (kdim // tk, n // tn, m // tm), 2, (a, b),
                   [pl.BlockSpec((tm, tk), lambda i, j, t: (t, i)), pl.BlockSpec((tm, tn), lambda i, j, t: (t, j))],
                   [(0, 1, 0)], TN, [(tk, tn)], [S((kdim, n), out_dtype)], [pl.BlockSpec((tk, tn), lambda i, j, t: (i, j))],
                   _store_epilogue)[0]


def _gateup_fwd(hn, wg_all, wu_all):
    n, d = hn.shape
    _, _, fs = wg_all.shape
    tm, tk = _tile(n, 1024), _tile(d, 512)
    rows = _tile(tm, 256)

    def epilogue(acc_refs, in_refs, out_refs):
        for r0 in range(0, tm, rows):
            g = acc_refs[0][pl.ds(r0, rows), :]
            u = acc_refs[1][pl.ds(r0, rows), :]
            out_refs[0][pl.ds(r0, rows), :] = g.astype(BF16)
            out_refs[1][pl.ds(r0, rows), :] = u.astype(BF16)
            out_refs[2][pl.ds(r0, rows), :] = (g * jax.nn.sigmoid(g) * u).astype(BF16)

    w_spec = pl.BlockSpec((None, tk, fs), lambda j, m, k: (j, k, 0))
    o_spec = pl.BlockSpec((None, tm, fs), lambda j, m, k: (j, m, 0))
    o_shape = S((N_DEV, n, fs), BF16)
    return _matmul("gateup_fwd", (N_DEV, n // tm, d // tk), 2, (hn, wg_all, wu_all),
                   [pl.BlockSpec((tm, tk), lambda j, m, k: (m, k)), w_spec, w_spec],
                   [(0, 1, 0), (0, 2, 1)], NN, [(tm, fs), (tm, fs)], [o_shape] * 3, [o_spec] * 3, epilogue)


def _down_fwd(act, wd_all, res):
    _, n, fs = act.shape
    d = wd_all.shape[2]
    tm, tn = _tile(n, 1024), _tile(d, 1024)

    def epilogue(acc_refs, in_refs, out_refs):
        out_refs[0][...] = acc_refs[0][...] + in_refs[2][...]

    return _matmul("down_fwd", (n // tm, d // tn, N_DEV), 2, (act, wd_all, res),
                   [pl.BlockSpec((None, tm, fs), lambda i, j, s: (s, i, 0)), pl.BlockSpec((None, fs, tn), lambda i, j, s: (s, 0, j)),
                    pl.BlockSpec((tm, tn), lambda i, j, s: (i, j))],
                   [(0, 1, 0)], NN, [(tm, tn)], [S((n, d), F32)], [pl.BlockSpec((tm, tn), lambda i, j, s: (i, j))], epilogue)[0]


def _dact_bwd(dh, wd_all, gate, up):
    n, d = dh.shape
    _, fs, _ = wd_all.shape
    tm, tk = _tile(n, 1024), _tile(d, 512)
    rows = _tile(tm, 256)

    def epilogue(acc_refs, in_refs, out_refs):
        for r0 in range(0, tm, rows):
            da = acc_refs[0][pl.ds(r0, rows), :]
            g = in_refs[2][pl.ds(r0, rows), :].astype(F32)
            u = in_refs[3][pl.ds(r0, rows), :].astype(F32)
            sg = jax.nn.sigmoid(g)
            out_refs[0][pl.ds(r0, rows), :] = (da * u * (sg * (1.0 + g * (1.0 - sg)))).astype(BF16)
            out_refs[1][pl.ds(r0, rows), :] = (da * (g * sg)).astype(BF16)

    t_spec = pl.BlockSpec((None, tm, fs), lambda j, m, k: (j, m, 0))
    o_shape = S((N_DEV, n, fs), BF16)
    return _matmul("dact_bwd", (N_DEV, n // tm, d // tk), 2, (dh, wd_all, gate, up),
                   [pl.BlockSpec((tm, tk), lambda j, m, k: (m, k)), pl.BlockSpec((None, fs, tk), lambda j, m, k: (j, 0, k)), t_spec, t_spec],
                   [(0, 1, 0)], NT, [(tm, fs)], [o_shape] * 2, [t_spec] * 2, epilogue)


def _dwd_bwd(act, dh):
    _, n, fs = act.shape
    d = dh.shape[1]
    tm, tn = _tile(n, 512), _tile(d, 1024)
    return _matmul("dwd_bwd", (N_DEV, d // tn, n // tm), 2, (act, dh),
                   [pl.BlockSpec((None, tm, fs), lambda j, c, t: (j, t, 0)), pl.BlockSpec((tm, tn), lambda j, c, t: (t, c))],
                   [(0, 1, 0)], TN, [(fs, tn)], [S((N_DEV, fs, d), BF16)], [pl.BlockSpec((None, fs, tn), lambda j, c, t: (j, 0, c))],
                   _store_epilogue)[0]


def _dw_cols_bwd(name, hn, dys, after=None):
    n, d = hn.shape
    ws = dys[0].shape[2]
    tm, tk = _tile(n, 512), _tile(d, 1024)
    k_out = len(dys)
    y_spec = pl.BlockSpec((None, tm, ws), lambda j, c, t: (j, t, 0))
    o_spec = pl.BlockSpec((None, tk, ws), lambda j, c, t: (j, c, 0))
    return _matmul(name, (N_DEV, d // tk, n // tm), 2, (hn, *dys),
                   [pl.BlockSpec((tm, tk), lambda j, c, t: (t, c))] + [y_spec] * k_out,
                   [(0, 1 + i, i) for i in range(k_out)], TN, [(tk, ws)] * k_out, [S((N_DEV, d, ws), BF16)] * k_out, [o_spec] * k_out,
                   _store_epilogue, after)


def _dwin_bwd(hn, dproj):
    n, d = hn.shape
    ws = dproj.shape[1] // N_DEV
    tm, tk = _tile(n, 512), _tile(d, 1024)
    return _matmul("dwin_bwd", (N_DEV, d // tk, n // tm), 2, (hn, dproj),
                   [pl.BlockSpec((tm, tk), lambda j, c, t: (t, c)), pl.BlockSpec((tm, ws), lambda j, c, t: (t, j))],
                   [(0, 1, 0)], TN, [(tk, ws)], [S((N_DEV, d, ws), BF16)], [pl.BlockSpec((None, tk, ws), lambda j, c, t: (j, c, 0))],
                   _store_epilogue)[0]


def _dhn_from_shards(name, dys, ws_all, dy_is_flat, after=None):
    if dy_is_flat:
        n, ws = dys[0].shape[0], dys[0].shape[1] // N_DEV
    else:
        _, n, ws = dys[0].shape
    d = ws_all[0].shape[1]
    tm, tn = _tile(n, 1024), _tile(d, 1024)
    if dy_is_flat:
        y_spec = pl.BlockSpec((tm, ws), lambda i, j, s: (i, s))
    else:
        y_spec = pl.BlockSpec((None, tm, ws), lambda i, j, s: (s, i, 0))
    w_spec = pl.BlockSpec((None, tn, ws), lambda i, j, s: (s, j, 0))
    k_terms = len(dys)
    return _matmul(name, (n // tm, d // tn, N_DEV), 2, (*dys, *ws_all), [y_spec] * k_terms + [w_spec] * k_terms,
                   [(i, k_terms + i, 0) for i in range(k_terms)], NT, [(tm, tn)], [S((n, d), F32)],
                   [pl.BlockSpec((tm, tn), lambda i, j, s: (i, j))], _store_epilogue, after)[0]


def _rms_fwd(name, x, g):
    n, d = x.shape
    tm = _tile(n, 256)

    def body(x_ref, g_ref, o_ref):
        xv = x_ref[...]
        r = lax.rsqrt(jnp.mean(xv * xv, axis=-1, keepdims=True) + RMS_EPS)
        o_ref[...] = (xv * r * g_ref[...]).astype(BF16)

    return pl.pallas_call(body, name=name, grid=(n // tm,),
                          in_specs=[pl.BlockSpec((tm, d), lambda i: (i, 0)), pl.BlockSpec((1, d), lambda i: (0, 0))],
                          out_specs=pl.BlockSpec((tm, d), lambda i: (i, 0)), out_shape=S((n, d), BF16), compiler_params=_cp(1))(x, g)


def _rms_bwd(name, x, g, dy, res):
    n, d = x.shape
    tm = _tile(n, 256)

    def body(x_ref, g_ref, dy_ref, res_ref, dx_ref, dg_ref):
        xv, dyv = x_ref[...], dy_ref[...]
        r = lax.rsqrt(jnp.mean(xv * xv, axis=-1, keepdims=True) + RMS_EPS)
        xr = xv * r
        dgy = dyv * g_ref[...]
        dx_ref[...] = res_ref[...] + r * (dgy - xr * jnp.mean(dgy * xr, axis=-1, keepdims=True))

        @pl.when(pl.program_id(0) == 0)
        def _():
            dg_ref[...] = jnp.zeros_like(dg_ref)

        dg_ref[...] += jnp.sum(dyv * xr, axis=0, keepdims=True)

    row = pl.BlockSpec((tm, d), lambda i: (i, 0))
    vec = pl.BlockSpec((1, d), lambda i: (0, 0))
    return pl.pallas_call(body, name=name, grid=(n // tm,), in_specs=[row, vec, row, row], out_specs=[row, vec],
                          out_shape=[S((n, d), F32), S((1, d), F32)], compiler_params=_cp(1))(x, g, dy, res)


def _final_loss(h, g, target):
    n, d = h.shape
    tm = _tile(n, 256)

    def body(h_ref, g_ref, t_ref, dh_ref, dhb_ref, dg_ref, sse_ref):
        hv, gv = h_ref[...], g_ref[...]
        r = lax.rsqrt(jnp.mean(hv * hv, axis=-1, keepdims=True) + RMS_EPS)
        hr = hv * r
        err = hr * gv - t_ref[...]
        dy = err * (1.0 / d)
        dgy = dy * gv
        dh = r * (dgy - hr * jnp.mean(dgy * hr, axis=-1, keepdims=True))
        dh_ref[...] = dh
        dhb_ref[...] = dh.astype(BF16)

        @pl.when(pl.program_id(0) == 0)
        def _():
            dg_ref[...] = jnp.zeros_like(dg_ref)
            sse_ref[...] = jnp.zeros_like(sse_ref)

        dg_ref[...] += jnp.sum(dy * hr, axis=0, keepdims=True)
        sse_ref[...] += jnp.sum(err * err)

    row = pl.BlockSpec((tm, d), lambda i: (i, 0))
    vec = pl.BlockSpec((1, d), lambda i: (0, 0))
    one = pl.BlockSpec((8, HEAD_DIM), lambda i: (0, 0))
    return pl.pallas_call(body, name="final_loss", grid=(n // tm,), in_specs=[row, vec, row], out_specs=[row, row, vec, one],
                          out_shape=[S((n, d), F32), S((n, d), BF16), S((1, d), F32), S((8, HEAD_DIM), F32)],
                          compiler_params=_cp(1))(h, g, target)


def _dot(a, b, dims):
    return lax.dot_general(a, b, dims, preferred_element_type=F32)


def _split3(x):
    hi = x.astype(BF16)
    r1 = x - hi.astype(F32)
    mid = r1.astype(BF16)
    lo = (r1 - mid.astype(F32)).astype(BF16)
    return hi, mid, lo


def _scan_cols(x, tri, terms):
    parts = _split3(x)[:terms]
    out = _dot(parts[0], tri, NN)
    for p in parts[1:]:
        out = out + _dot(p, tri, NN)
    return out


def _head_norm_fwd(o, g):
    r = lax.rsqrt(jnp.mean(o * o, axis=-1, keepdims=True) + RMS_EPS)
    return o * r * g


def _head_norm_bwd(o, g, dy):
    r = lax.rsqrt(jnp.mean(o * o, axis=-1, keepdims=True) + RMS_EPS)
    orr = o * r
    dgy = dy * g
    return r * (dgy - orr * jnp.mean(dgy * orr, axis=-1, keepdims=True)), dy * orr


def _interleave_plan(t):
    return [(i, dil, t // dil) for i, dil in enumerate(DILATIONS)]


def _band_mask(u, blocks_per_seq):
    row = lax.broadcasted_iota(jnp.int32, (BLK, 2 * BLK), 0)
    col = lax.broadcasted_iota(jnp.int32, (BLK, 2 * BLK), 1)
    dist = row + BLK - col
    has_prev = (u % blocks_per_seq) != 0
    return (dist >= 0) & (dist <= BLK) & ((col >= BLK) | has_prev)


def _dil_fwd(proj, g_dil, bl, t, n_heads):
    n = bl * t
    nb = t // BLK
    scale = HEAD_DIM ** -0.5
    plan = _interleave_plan(t)
    chunk = _tile(t, 256)

    def body(q_ref, k_ref, v_ref, g_ref, omix_ref, opre_ref, lse_ref, stg, qd, kd, vd, ob, lb, on, ln):
        for src, dst, pad in ((q_ref, qd, 0), (k_ref, kd, BLK), (v_ref, vd, BLK)):
            stg[...] = src[...].astype(F32)
            for bi, dil, sub in plan:
                if pad:
                    dst[bi, pl.ds(0, BLK), :] = jnp.zeros((BLK, HEAD_DIM), BF16)
                if dil == 1:
                    dst[bi, pl.ds(pad, t), :] = src[...]
                else:
                    for r in range(dil):
                        dst[bi, pl.ds(pad + r * sub, sub), :] = stg[pl.ds(r, sub, stride=dil), :].astype(BF16)

        for bi, dil, sub in plan:
            def blk(u, carry, bi=bi, sub=sub):
                rows = pl.ds(pl.multiple_of(u * BLK, BLK), BLK)
                win = pl.ds(pl.multiple_of(u * BLK, BLK), 2 * BLK)
                sc = _dot(qd[bi, rows, :], kd[bi, win, :], NT) * scale
                sc = jnp.where(_band_mask(u, sub // BLK), sc, NEG)
                m = jnp.max(sc, axis=-1, keepdims=True)
                p = jnp.exp(sc - m)
                den = jnp.sum(p, axis=-1, keepdims=True)
                ob[bi, rows, :] = _dot((p / den).astype(BF16), vd[bi, win, :], NN)
                lb[bi, rows, :] = jnp.broadcast_to(m + jnp.log(den), (BLK, HEAD_DIM))
                return carry
            lax.fori_loop(0, nb // DIL_UNROLL, lambda i, c, blk=blk: [blk(i * DIL_UNROLL + s, c) for s in range(DIL_UNROLL)][-1], 0)

        for bi, dil, sub in plan[1:]:
            for r in range(dil):
                on[bi - 1, pl.ds(r, sub, stride=dil), :] = ob[bi, pl.ds(r * sub, sub), :]
                ln[bi - 1, pl.ds(r, sub, stride=dil), :] = lb[bi, pl.ds(r * sub, sub), :]

        def merge(i, carry):
            rows = pl.ds(pl.multiple_of(i * chunk, chunk), chunk)
            l0, l1, l2 = lb[0, rows, :], ln[0, rows, :], ln[1, rows, :]
            mx = jnp.maximum(jnp.maximum(l0, l1), l2)
            w0, w1, w2 = jnp.exp(l0 - mx), jnp.exp(l1 - mx), jnp.exp(l2 - mx)
            tot = w0 + w1 + w2
            o = (w0 / tot) * ob[0, rows, :] + (w1 / tot) * on[0, rows, :] + (w2 / tot) * on[1, rows, :]
            lse_ref[rows, :] = mx + jnp.log(tot)
            opre_ref[rows, :] = o
            omix_ref[rows, :] = _head_norm_fwd(o, g_ref[...]).astype(BF16)
            return carry
        lax.fori_loop(0, t // chunk, merge, 0)

    hs = n_heads
    col = lambda off: pl.BlockSpec((t, HEAD_DIM), lambda b, h: (b, off + h))
    return pl.pallas_call(
        body, name="dil_fwd", grid=(bl, hs),
        in_specs=[col(0), col(hs), col(2 * hs), pl.BlockSpec((1, HEAD_DIM), lambda b, h: (0, h))],
        out_specs=[col(0), col(0), pl.BlockSpec((t, HEAD_DIM), lambda b, h: (b * hs + h, 0))],
        out_shape=[S((n, 2 * hs * HEAD_DIM), BF16), S((n, 2 * hs * HEAD_DIM), F32), S((bl * hs * t, HEAD_DIM), F32)],
        scratch_shapes=[pltpu.VMEM((t, HEAD_DIM), F32), pltpu.VMEM((3, t, HEAD_DIM), BF16),
                        pltpu.VMEM((3, t + BLK, HEAD_DIM), BF16), pltpu.VMEM((3, t + BLK, HEAD_DIM), BF16),
                        pltpu.VMEM((3, t, HEAD_DIM), F32), pltpu.VMEM((3, t, HEAD_DIM), F32),
                        pltpu.VMEM((2, t, HEAD_DIM), F32), pltpu.VMEM((2, t, HEAD_DIM), F32)],
        compiler_params=_cp(2),
    )(proj, proj, proj, g_dil)


def _dil_bwd(proj, opre, lse, d_omix, g_dil, rc, rs1, rs2, bl, t, n_heads):
    n = bl * t
    nb = t // BLK
    scale = HEAD_DIM ** -0.5
    plan = _interleave_plan(t)
    chunk = _tile(t, 256)

    def body(q_ref, k_ref, v_ref, opre_ref, lse_ref, dy_ref, g_ref, c_ref, s1_ref, s2_ref, out_ref, dg_ref,
             stg, qd, kd, vd, dod, ldd, dqd, dkc, dkp, dvc, dvp, sk, sv):
        which = pl.program_id(2)

        @pl.when(jnp.logical_and(which == 0, pl.program_id(1) == 0))
        def _():
            dg_ref[...] = jnp.zeros_like(dg_ref)

        @pl.when(which == 0)
        def _():
            def prep(i, dg):
                rows = pl.ds(pl.multiple_of(i * chunk, chunk), chunk)
                o = opre_ref[rows, :]
                d_o, dg_rows = _head_norm_bwd(o, g_ref[...], dy_ref[rows, :])
                stg[rows, :] = d_o
                lane = lax.broadcasted_iota(jnp.int32, (chunk, HEAD_DIM), 1)
                ldd[0, rows, :] = jnp.where(lane < HEAD_DIM // 2, lse_ref[rows, :], jnp.sum(d_o * o, axis=-1, keepdims=True))
                return dg + jnp.sum(dg_rows, axis=0, keepdims=True)
            dg_ref[...] += lax.fori_loop(0, t // chunk, prep, jnp.zeros((1, HEAD_DIM), F32))

            dod[0] = stg[...].astype(BF16)
            for bi, dil, sub in plan[1:]:
                for r in range(dil):
                    dst = pl.ds(r * sub, sub)
                    dod[bi, dst, :] = stg[pl.ds(r, sub, stride=dil), :].astype(BF16)
                    ldd[bi, dst, :] = ldd[0, pl.ds(r, sub, stride=dil), :]
            for src, dst, pad in ((q_ref, qd, 0), (k_ref, kd, BLK), (v_ref, vd, BLK)):
                stg[...] = src[...].astype(F32)
                for bi, dil, sub in plan:
                    if pad:
                        dst[bi, pl.ds(0, BLK), :] = jnp.zeros((BLK, HEAD_DIM), BF16)
                    if dil == 1:
                        dst[bi, pl.ds(pad, t), :] = src[...]
                    else:
                        for r in range(dil):
                            dst[bi, pl.ds(pad + r * sub, sub), :] = stg[pl.ds(r, sub, stride=dil), :].astype(BF16)

            for bi, dil, sub in plan:
                def blk(u, carry, bi=bi, sub=sub):
                    rows = pl.ds(pl.multiple_of(u * BLK, BLK), BLK)
                    win = pl.ds(pl.multiple_of(u * BLK, BLK), 2 * BLK)
                    qb, kw, vw, dob = qd[bi, rows, :], kd[bi, win, :], vd[bi, win, :], dod[bi, rows, :]
                    sc = _dot(qb, kw, NT) * scale
                    stats = ldd[bi, rows, :]
                    p = jnp.where(_band_mask(u, sub // BLK), jnp.exp(sc - stats[:, :1]), 0.0)
                    dp = _dot(dob, vw, NT)
                    ds = (p * (dp - stats[:, HEAD_DIM // 2:HEAD_DIM // 2 + 1]) * scale).astype(BF16)
                    dqd[bi, rows, :] = _dot(ds, kw, NN)
                    dk_win = _dot(ds, qb, TN)
                    dv_win = _dot(p.astype(BF16), dob, TN)
                    dkp[bi, rows, :] = dk_win[:BLK]
                    dkc[bi, rows, :] = dk_win[BLK:]
                    dvp[bi, rows, :] = dv_win[:BLK]
                    dvc[bi, rows, :] = dv_win[BLK:]
                    return carry
                lax.fori_loop(0, nb // DIL_UNROLL, lambda i, c, blk=blk: [blk(i * DIL_UNROLL + s, c) for s in range(DIL_UNROLL)][-1], 0)

            for cur, prev, undo_rope, dst in ((dqd, None, True, out_ref), (dkc, dkp, True, sk), (dvc, dvp, False, sv)):
                def summed(bi, start, size, cur=cur, prev=prev):
                    v = cur[bi, pl.ds(start, size), :]
                    if prev is None:
                        return v
                    if start + size < t:
                        return v + prev[bi, pl.ds(start + BLK, size), :]
                    if size == BLK:
                        return v
                    return v + jnp.concatenate([prev[bi, pl.ds(start + BLK, size - BLK), :], jnp.zeros((BLK, HEAD_DIM), F32)], axis=0)
                stg[...] = summed(0, 0, t)
                for bi, dil, sub in plan[1:]:
                    for r in range(dil):
                        stg[pl.ds(r, sub, stride=dil), :] += summed(bi, r * sub, sub)
                if undo_rope:
                    dst[...] = _rope_bwd(stg[...], c_ref[...], s1_ref[...], s2_ref[...]).astype(BF16)
                else:
                    dst[...] = stg[...].astype(BF16)

        @pl.when(which == 1)
        def _():
            out_ref[...] = sk[...]

        @pl.when(which == 2)
        def _():
            out_ref[...] = sv[...]

    hs = n_heads
    col = lambda off: pl.BlockSpec((t, HEAD_DIM), lambda h, b, w: (b, off + h))
    per_head = pl.BlockSpec((t, HEAD_DIM), lambda h, b, w: (b * hs + h, 0))
    tab = pl.BlockSpec((t, HEAD_DIM), lambda h, b, w: (b, 0))
    gvec = pl.BlockSpec((1, HEAD_DIM), lambda h, b, w: (0, h))
    tb = (t, HEAD_DIM)
    tp = (t + BLK, HEAD_DIM)
    return pl.pallas_call(
        body, name="dil_bwd", grid=(hs, bl, 3),
        in_specs=[col(0), col(hs), col(2 * hs), col(0), per_head, col(0), gvec, tab, tab, tab],
        out_specs=[pl.BlockSpec((t, HEAD_DIM), lambda h, b, w: (b, w * hs + h)), gvec],
        out_shape=[S((n, 6 * hs * HEAD_DIM), BF16), S((1, hs * HEAD_DIM), F32)],
        scratch_shapes=[pltpu.VMEM(tb, F32), pltpu.VMEM((3,) + tb, BF16), pltpu.VMEM((3,) + tp, BF16), pltpu.VMEM((3,) + tp, BF16),
                        pltpu.VMEM((3,) + tb, BF16), pltpu.VMEM((3,) + tb, F32), pltpu.VMEM((3,) + tb, F32),
                        pltpu.VMEM((3,) + tb, F32), pltpu.VMEM((3,) + tb, F32), pltpu.VMEM((3,) + tb, F32), pltpu.VMEM((3,) + tb, F32),
                        pltpu.VMEM(tb, BF16), pltpu.VMEM(tb, BF16)],
        compiler_params=_cp(3),
    )(proj, proj, proj, opre, lse, d_omix, g_dil, rc, rs1, rs2)


def _sb_tile(qb, kb, scale, diag):
    z = _dot(qb, kb, NT) * scale
    tl = jnp.log1p(jnp.exp(-jnp.abs(z)))
    log_not = -(jnp.maximum(z, 0.0) + tl)
    strict = None
    if diag:
        row = lax.broadcasted_iota(jnp.int32, (SBT, SBT), 0)
        col = lax.broadcasted_iota(jnp.int32, (SBT, SBT), 1)
        strict = col < row
        log_not = jnp.where(strict, log_not, 0.0)
    return log_not, jnp.minimum(z, 0.0) - tl, strict


def _tri(cmp):
    row = lax.broadcasted_iota(jnp.int32, (SBT, SBT), 0)
    col = lax.broadcasted_iota(jnp.int32, (SBT, SBT), 1)
    return jnp.where(cmp(row, col), 1.0, 0.0).astype(BF16)


def _sb_fwd(proj, g_sb, omix_in, opre_in, bl, t, n_heads):
    nb = t // SBT
    scale = HEAD_DIM ** -0.5

    def body(q_ref, k_ref, v_ref, g_ref, _omix_in, _opre_in, omix_ref, opre_ref, lt_ref):
        later = _tri(lambda r, c: r > c)

        def q_block(qi, carry):
            rows = pl.ds(pl.multiple_of(qi * SBT, SBT), SBT)
            qb = q_ref[rows, :]

            def tile(kj, run, acc, diag):
                krows = pl.ds(pl.multiple_of(kj * SBT, SBT), SBT)
                log_not, log_beta, strict = _sb_tile(qb, k_ref[krows, :], scale, diag)
                a = jnp.exp(log_beta + _scan_cols(log_not, later, 2) + run)
                if diag:
                    a = jnp.where(strict, a, 0.0)
                return run + jnp.sum(log_not, axis=-1, keepdims=True), acc + _dot(a.astype(BF16), v_ref[krows, :], NN)

            st = tile(qi, jnp.zeros((SBT, 1), F32), jnp.zeros((SBT, HEAD_DIM), F32), True)
            run, acc = lax.fori_loop(0, qi, lambda it, st: tile(qi - 1 - it, st[0], st[1], False), st)
            lt_ref[rows, :] = jnp.broadcast_to(run, (SBT, HEAD_DIM))
            opre_ref[rows, :] = acc
            omix_ref[rows, :] = _head_norm_fwd(acc, g_ref[...]).astype(BF16)
            return carry

        lax.fori_loop(0, nb, q_block, 0)

    hs = n_heads
    col = lambda off: pl.BlockSpec((t, HEAD_DIM), lambda b, h: (b, off + h))
    return pl.pallas_call(
        body, name="sb_fwd", grid=(bl, hs),
        in_specs=[col(3 * hs), col(4 * hs), col(5 * hs), pl.BlockSpec((1, HEAD_DIM), lambda b, h: (0, h)), HBM_SPEC, HBM_SPEC],
        out_specs=[col(hs), col(hs), pl.BlockSpec((t, HEAD_DIM), lambda b, h: (b * hs + h, 0))],
        out_shape=[S(omix_in.shape, BF16), S(opre_in.shape, F32), S((bl * hs * t, HEAD_DIM), F32)],
        input_output_aliases={4: 0, 5: 1}, compiler_params=_cp(2),
    )(proj, proj, proj, g_sb, omix_in, opre_in)


def _sb_bwd(proj, opre, ltot, d_omix, g_sb, dproj_in, bl, t, n_heads):
    nb = t // SBT
    scale = HEAD_DIM ** -0.5

    def body(q_ref, k_ref, v_ref, opre_ref, lt_ref, dy_ref, g_ref, _dproj_in, out_ref, dg_ref, dq, dk, dv):
        which = pl.program_id(2)

        @pl.when(jnp.logical_and(which == 0, pl.program_id(1) == 0))
        def _():
            dg_ref[...] = jnp.zeros_like(dg_ref)

        @pl.when(which == 0)
        def _():
            upto = _tri(lambda r, c: r <= c)
            before = _tri(lambda r, c: r < c)
            dk[...] = jnp.zeros_like(dk)
            dv[...] = jnp.zeros_like(dv)

            def q_block(qi, dg):
                rows = pl.ds(pl.multiple_of(qi * SBT, SBT), SBT)
                qb = q_ref[rows, :]
                o = opre_ref[rows, :]
                d_o, dg_rows = _head_norm_bwd(o, g_ref[...], dy_ref[rows, :])
                dob = d_o.astype(BF16)
                lt = lt_ref[rows, :][:, :1]

                def tile(kj, run, grun, dq_acc, diag):
                    krows = pl.ds(pl.multiple_of(kj * SBT, SBT), SBT)
                    kb, vb = k_ref[krows, :], v_ref[krows, :]
                    log_not, log_beta, strict = _sb_tile(qb, kb, scale, diag)
                    excl = lt - (run + _scan_cols(log_not, upto, 2))
                    a = jnp.exp(log_beta + excl)
                    if diag:
                        a = jnp.where(strict, a, 0.0)
                    g_a = a * _dot(dob, vb, NT)
                    g_before = grun + _scan_cols(g_a, before, 2)
                    dz = (g_a - (g_a + g_before) * jnp.exp(log_beta)) * scale
                    if diag:
                        dz = jnp.where(strict, dz, 0.0)
                    dzb = dz.astype(BF16)
                    dk[krows, :] += _dot(dzb, qb, TN)
                    dv[krows, :] += _dot(a.astype(BF16), dob, TN)
                    return (run + jnp.sum(log_not, axis=-1, keepdims=True), grun + jnp.sum(g_a, axis=-1, keepdims=True),
                            dq_acc + _dot(dzb, kb, NN))

                zero_col = jnp.zeros((SBT, 1), F32)
                st = lax.fori_loop(0, qi, lambda kj, st: tile(kj, st[0], st[1], st[2], False),
                                   (zero_col, zero_col, jnp.zeros((SBT, HEAD_DIM), F32)))
                _, _, dq_acc = tile(qi, st[0], st[1], st[2], True)
                dq[rows, :] = dq_acc.astype(BF16)
                return dg + jnp.sum(dg_rows, axis=0, keepdims=True)

            dg_ref[...] += lax.fori_loop(0, nb, q_block, jnp.zeros((1, HEAD_DIM), F32))
            out_ref[...] = dq[...]

        @pl.when(which == 1)
        def _():
            out_ref[...] = dk[...].astype(BF16)

        @pl.when(which == 2)
        def _():
            out_ref[...] = dv[...].astype(BF16)

    hs = n_heads
    col = lambda off: pl.BlockSpec((t, HEAD_DIM), lambda h, b, w: (b, off + h))
    per_head = pl.BlockSpec((t, HEAD_DIM), lambda h, b, w: (b * hs + h, 0))
    gvec = pl.BlockSpec((1, HEAD_DIM), lambda h, b, w: (0, h))
    tb = (t, HEAD_DIM)
    return pl.pallas_call(
        body, name="sb_bwd", grid=(hs, bl, 3),
        in_specs=[col(3 * hs), col(4 * hs), col(5 * hs), col(hs), per_head, col(hs), gvec, HBM_SPEC],
        out_specs=[pl.BlockSpec((t, HEAD_DIM), lambda h, b, w: (b, (3 + w) * hs + h)), gvec],
        out_shape=[S(dproj_in.shape, BF16), S((1, hs * HEAD_DIM), F32)],
        scratch_shapes=[pltpu.VMEM(tb, BF16), pltpu.VMEM(tb, F32), pltpu.VMEM(tb, F32)],
        input_output_aliases={7: 0}, compiler_params=_cp(3),
    )(proj, proj, proj, opre, ltot, d_omix, g_sb, dproj_in)


def _all_gather(shards):
    k_w = len(shards)

    def body(*refs):
        ins, outs = refs[:k_w], refs[k_w:2 * k_w]
        send_sems, recv_sems, local_sems = refs[2 * k_w:]
        x, y, c = lax.axis_index("x"), lax.axis_index("y"), lax.axis_index("c")
        me, sibling = (x, y, c), (x, y, 1 - c)
        chips = [(1 - x, y), (x, 1 - y), (1 - x, 1 - y)]

        def slot(dev):
            return 4 * dev[0] + 2 * dev[1] + dev[2]

        def copy(w, k, block, to, src=None):
            dst = outs[w].at[slot(block)]
            return pltpu.make_async_remote_copy(
                src_ref=dst if src is None else src, dst_ref=dst, send_sem=send_sems.at[w * 7 + k], recv_sem=recv_sems.at[w * 7 + k],
                device_id=to, device_id_type=MESH)

        mine = [pltpu.make_async_copy(ins[w], outs[w].at[slot(me)], local_sems.at[w]) for w in range(k_w)]
        first = []
        for w in range(k_w):
            mine[w].start()
            first.append(copy(w, 0, me, sibling, src=ins[w]))
            first += [copy(w, 1 + j, me, (*chip, c), src=ins[w]) for j, chip in enumerate(chips)]
        for cp in first:
            cp.start()
        passed = []
        for w in range(k_w):
            for j, chip in enumerate(chips):
                copy(w, 1 + j, (*chip, c), me).wait_recv()
                fwd = copy(w, 4 + j, (*chip, c), sibling)
                fwd.start()
                passed.append(fwd)
        for w in range(k_w):
            copy(w, 0, sibling, me).wait_recv()
            for j, chip in enumerate(chips):
                copy(w, 4 + j, (*chip, 1 - c), me).wait_recv()
        for cp in first + passed:
            cp.wait_send()
        for cp in mine:
            cp.wait()

    return pl.pallas_call(
        body, name="weights_all_gather", in_specs=[HBM_SPEC] * k_w, out_specs=[HBM_SPEC] * k_w,
        out_shape=[S((N_DEV,) + s.shape, s.dtype) for s in shards],
        scratch_shapes=[pltpu.SemaphoreType.DMA((7 * k_w,)), pltpu.SemaphoreType.DMA((7 * k_w,)), pltpu.SemaphoreType.DMA((k_w,))],
    )(*shards)


def _grad_exchange(grads):
    k_w = len(grads)

    def body(*refs):
        ins, outs = refs[:k_w], refs[k_w:2 * k_w]
        send_sems, recv_sems, local_sems = refs[2 * k_w:]
        x, y, c = lax.axis_index("x"), lax.axis_index("y"), lax.axis_index("c")
        my_slot = 4 * x + 2 * y + c
        copies = []
        for w in range(k_w):
            own = pltpu.make_async_copy(ins[w].at[my_slot], outs[w].at[my_slot], local_sems.at[w])
            own.start()
            copies.append(own)
        remote = []
        for w in range(k_w):
            for k in range(1, N_DEV):
                px, py, pc = x ^ (k >> 2), y ^ ((k >> 1) & 1), c ^ (k & 1)
                cp = pltpu.make_async_remote_copy(
                    src_ref=ins[w].at[4 * px + 2 * py + pc], dst_ref=outs[w].at[my_slot],
                    send_sem=send_sems.at[w * 7 + k - 1], recv_sem=recv_sems.at[w * 7 + k - 1],
                    device_id=(px, py, pc), device_id_type=MESH)
                cp.start()
                remote.append(cp)
        for cp in remote:
            cp.wait_send()
        for w in range(k_w):
            for k in range(1, N_DEV):
                px, py, pc = x ^ (k >> 2), y ^ ((k >> 1) & 1), c ^ (k & 1)
                peer_slot = 4 * px + 2 * py + pc
                pltpu.make_async_remote_copy(
                    src_ref=ins[w].at[my_slot], dst_ref=outs[w].at[peer_slot],
                    send_sem=send_sems.at[w * 7 + k - 1], recv_sem=recv_sems.at[w * 7 + k - 1],
                    device_id=(px, py, pc), device_id_type=MESH).wait_recv()
        for cp in copies:
            cp.wait()

    return pl.pallas_call(
        body, name="grad_exchange", in_specs=[HBM_SPEC] * k_w, out_specs=[HBM_SPEC] * k_w,
        out_shape=[S(g.shape, g.dtype) for g in grads],
        scratch_shapes=[pltpu.SemaphoreType.DMA((7 * k_w,)), pltpu.SemaphoreType.DMA((7 * k_w,)), pltpu.SemaphoreType.DMA((k_w,))],
    )(*grads)


def _peer(x, y, c, k):
    px, py, pc = x ^ (k >> 2), y ^ ((k >> 1) & 1), c ^ (k & 1)
    return (px, py, pc), 4 * px + 2 * py + pc


def _exchange_copies(srcs, lands, send_sems, recv_sems, local_sems, gather):
    x, y, c = lax.axis_index("x"), lax.axis_index("y"), lax.axis_index("c")
    my_slot = 4 * x + 2 * y + c
    local, remote = [], []
    for w, (src, land) in enumerate(zip(srcs, lands)):
        local.append(pltpu.make_async_copy(src if gather else src.at[my_slot], land.at[my_slot], local_sems.at[w]))
        for k in range(1, N_DEV):
            peer, peer_slot = _peer(x, y, c, k)
            remote.append(pltpu.make_async_remote_copy(
                src_ref=src if gather else src.at[peer_slot], dst_ref=land.at[my_slot],
                send_sem=send_sems.at[w * (N_DEV - 1) + k - 1], recv_sem=recv_sems.at[w * (N_DEV - 1) + k - 1],
                device_id=peer, device_id_type=MESH))
    return local, remote


def _exchange_start(name, srcs, gather, after):
    k_w = len(srcs)
    land_shapes = [((N_DEV,) + s.shape) if gather else s.shape for s in srcs]

    def body(*refs):
        src_refs, land_refs = refs[:k_w], refs[k_w:2 * k_w]
        send_sems, recv_sems, local_sems = refs[2 * k_w + 1:2 * k_w + 4]
        token = refs[-1]
        local, remote = _exchange_copies(src_refs, land_refs, send_sems, recv_sems, local_sems, gather)
        for cp in local + remote:
            cp.start()
        token[...] = jnp.zeros_like(token)

    n_sem = (N_DEV - 1) * k_w
    hbm = lambda a: pltpu.with_memory_space_constraint(a, pltpu.HBM)
    outs = pl.pallas_call(
        body, name=name,
        in_specs=[HBM_SPEC] * (2 * k_w) + [pl.BlockSpec(memory_space=pl.ANY)],
        out_shape=(pltpu.SemaphoreType.DMA((n_sem,)), pltpu.SemaphoreType.DMA((n_sem,)), pltpu.SemaphoreType.DMA((k_w,)),
                   *[pltpu.HBM(s.shape, s.dtype) for s in srcs], *[pltpu.HBM(ls, s.dtype) for ls, s in zip(land_shapes, srcs)],
                   S((8, HEAD_DIM), F32)),
        out_specs=(SEM_SPEC, SEM_SPEC, SEM_SPEC, *[HBM_SPEC] * (2 * k_w), pl.BlockSpec(memory_space=pltpu.VMEM)),
        input_output_aliases={i: 3 + i for i in range(2 * k_w)},
        compiler_params=pltpu.CompilerParams(has_side_effects=SIDE_EFFECT),
    )(*[hbm(s) for s in srcs], *[hbm(lax.empty(ls, s.dtype)) for ls, s in zip(land_shapes, srcs)], after)
    return dict(sems=outs[:3], srcs=outs[3:3 + k_w], lands=outs[3 + k_w:3 + 2 * k_w], token_block=outs[-1], token=outs[-1][0, 0], gather=gather)


def _exchange_wait(name, handle, after):
    k_w = len(handle["srcs"])
    gather = handle["gather"]

    def body(*refs):
        src_refs, land_refs = refs[:k_w], refs[k_w:2 * k_w]
        send_sems, recv_sems, local_sems = refs[2 * k_w:2 * k_w + 3]
        local, remote = _exchange_copies(src_refs, land_refs, send_sems, recv_sems, local_sems, gather)
        for cp in local:
            cp.wait()
        for cp in remote:
            cp.wait_send()
            cp.wait_recv()

    outs = pl.pallas_call(
        body, name=name,
        in_specs=[HBM_SPEC] * (2 * k_w) + [SEM_SPEC] * 3 + [pl.BlockSpec(memory_space=pl.ANY)],
        out_shape=tuple(pltpu.HBM(a.shape, a.dtype) for a in (*handle["srcs"], *handle["lands"])),
        out_specs=tuple([HBM_SPEC] * (2 * k_w)),
        input_output_aliases={i: i for i in range(2 * k_w)},
        compiler_params=pltpu.CompilerParams(has_side_effects=SIDE_EFFECT),
    )(*handle["srcs"], *handle["lands"], *handle["sems"], after)
    return outs[k_w:]


def _adamw(name, parts, w, m, v):
    r, c = w.shape
    tr = _tile(r, max(16, (1 << 19) // c // 16 * 16))

    def body(p_ref, w_ref, m_ref, v_ref, g_ref, d_ref, nm_ref, nv_ref):
        g = p_ref[0].astype(F32)
        for s in range(1, N_DEV):
            g = g + p_ref[s].astype(F32)
        m_new = ADAM_B1 * m_ref[...] + (1.0 - ADAM_B1) * g
        v_new = ADAM_B2 * v_ref[...] + (1.0 - ADAM_B2) * jnp.square(g)
        m_hat = m_new / (1.0 - ADAM_B1 ** ADAM_STEP)
        v_hat = v_new / (1.0 - ADAM_B2 ** ADAM_STEP)
        g_ref[...] = g
        d_ref[...] = -ADAM_LR * (m_hat / (jnp.sqrt(v_hat) + ADAM_EPS) + ADAM_WD * w_ref[...])
        nm_ref[...] = m_new
        nv_ref[...] = v_new

    blk = pl.BlockSpec((tr, c), lambda i: (i, 0))
    return pl.pallas_call(body, name=name, grid=(r // tr,), in_specs=[pl.BlockSpec((N_DEV, tr, c), lambda i: (0, i, 0)), blk, blk, blk],
                          out_specs=[blk] * 4, out_shape=[S((r, c), F32)] * 4, compiler_params=_cp(1))(parts, w, m, v)


def _rope_tables(positions):
    inv_freq = jnp.power(jnp.float32(ROPE_THETA), -jnp.arange(ROPE_HALF, dtype=F32) / ROPE_HALF)
    ang = positions.astype(F32).reshape(-1, 1) * inv_freq
    cos, sin = jnp.cos(ang), jnp.sin(ang)
    n = ang.shape[0]
    rest = HEAD_DIM - 2 * ROPE_HALF
    zeros = jnp.zeros((n, ROPE_HALF), F32)
    c = jnp.concatenate([cos, cos, jnp.ones((n, rest), F32)], axis=1)
    s1 = jnp.concatenate([zeros, sin, jnp.zeros((n, rest), F32)], axis=1)
    s2 = jnp.concatenate([-sin, zeros, jnp.zeros((n, rest), F32)], axis=1)
    return c, s1, s2


def kernel(x, positions, norm_mix_g, w_in, norm_out_dil_g, norm_out_sb_g, w_out, norm_ffn_g, w_gate, w_up, w_down, norm_final_g, loss_target, m_norm_mix_g, m_w_in, m_norm_out_dil_g, m_norm_out_sb_g, m_w_out, m_norm_ffn_g, m_w_gate, m_w_up, m_w_down, m_norm_final_g, v_norm_mix_g, v_w_in, v_norm_out_dil_g, v_norm_out_sb_g, v_w_out, v_norm_ffn_g, v_w_gate, v_w_up, v_w_down, v_norm_final_g):
    bl, t, d = x.shape
    n = bl * t
    hs = d // (2 * HEAD_DIM)
    x2 = x.reshape(n, d)
    target = loss_target.reshape(n, d)
    g_final = norm_final_g.reshape(1, d)
    rc, rs1, rs2 = _rope_tables(positions)

    (win_all,) = _all_gather([w_in[0].astype(BF16)])
    ex_wout = _exchange_start("wout_gather_start", [w_out[0].astype(BF16)], True, win_all)
    ex_wgu = _exchange_start("wgu_gather_start", [w_gate[0].astype(BF16), w_up[0].astype(BF16)], True, ex_wout["token_block"])
    ex_wd = _exchange_start("wd_gather_start", [w_down[0].astype(BF16)], True, ex_wgu["token_block"])
    rc = rc + ex_wd["token"]

    hn1 = _rms_fwd("rms_mix_fwd", x2, norm_mix_g)
    proj = _proj_fwd(hn1, win_all, rc, rs1, rs2, 2 * hs)
    omix, opre, lse = _dil_fwd(proj, norm_out_dil_g, bl, t, hs)
    omix, opre, ltot = _sb_fwd(proj, norm_out_sb_g, omix, opre, bl, t, hs)
    (wout_all,) = _exchange_wait("wout_gather_wait", ex_wout, ltot)
    wout_full = wout_all.reshape(d, d)
    h1 = _dense_res("out_fwd", omix, wout_full, x2, NN)
    hn2 = _rms_fwd("rms_ffn_fwd", h1, norm_ffn_g)
    wg_all, wu_all = _exchange_wait("wgu_gather_wait", ex_wgu, hn2)
    gate, up, act = _gateup_fwd(hn2, wg_all, wu_all)
    (wd_all,) = _exchange_wait("wd_gather_wait", ex_wd, act)
    h2 = _down_fwd(act, wd_all, h1)
    dh2, dh2b, dg_final, sse = _final_loss(h2, g_final, target)
    loss = lax.psum(sse[0, 0], ("x", "y", "c")) * (0.5 / d)

    dgate, dup = _dact_bwd(dh2b, wd_all, gate, up)
    dwd = _dwd_bwd(act, dh2b)
    ex_dwd = _exchange_start("dwd_exchange_start", [dwd], False, dwd)
    dwg, dwu = _dw_cols_bwd("dwgu_bwd", hn2, (dgate, dup), ex_dwd["token_block"])
    ex_dwgu = _exchange_start("dwgu_exchange_start", [dwg, dwu], False, ex_dwd["token_block"])
    dhn2 = _dhn_from_shards("dhn2_bwd", (dgate, dup), (wg_all, wu_all), False, ex_dwgu["token_block"])
    dh1, dg_ffn = _rms_bwd("rms_ffn_bwd", h1, norm_ffn_g, dhn2, dh2)
    d_omix = _dense_res("domix_bwd", dh1, wout_full, None, NT)
    dwout = _tn_full("dwout_bwd", omix, dh1).reshape(N_DEV, d // N_DEV, d)
    ex_dwout = _exchange_start("dwout_exchange_start", [dwout], False, dwout)
    dproj, dg_dil = _dil_bwd(proj, opre, lse, d_omix, norm_out_dil_g + ex_dwout["token"], rc, rs1, rs2, bl, t, hs)
    dproj, dg_sb = _sb_bwd(proj, opre, ltot, d_omix, norm_out_sb_g, dproj, bl, t, hs)
    dwin = _dwin_bwd(hn1, dproj)
    ex_dwin = _exchange_start("dwin_exchange_start", [dwin], False, dwin)
    dhn1 = _dhn_from_shards("dhn1_bwd", (dproj,), (win_all,), True, ex_dwin["token_block"])
    dx, dg_mix = _rms_bwd("rms_mix_bwd", x2, norm_mix_g, dhn1, dh1)

    gains = [norm_mix_g, norm_out_dil_g, norm_out_sb_g, norm_ffn_g, g_final]
    m_gains = [m_norm_mix_g, m_norm_out_dil_g, m_norm_out_sb_g, m_norm_ffn_g, m_norm_final_g.reshape(1, d)]
    v_gains = [v_norm_mix_g, v_norm_out_dil_g, v_norm_out_sb_g, v_norm_ffn_g, v_norm_final_g.reshape(1, d)]
    dg_vec = jnp.concatenate([dg_mix, dg_dil, dg_sb, dg_ffn, dg_final], axis=1)
    dg_all = jnp.broadcast_to(dg_vec[None], (N_DEV,) + dg_vec.shape)
    (rg,) = _grad_exchange([dg_all])

    out_w = {}
    (rwd,) = _exchange_wait("dwd_exchange_wait", ex_dwd, rg)
    out_w["w_down"] = _adamw("adamw_w_down", rwd, w_down[0], m_w_down[0], v_w_down[0])
    rwg, rwu = _exchange_wait("dwgu_exchange_wait", ex_dwgu, out_w["w_down"][0])
    out_w["w_gate"] = _adamw("adamw_w_gate", rwg, w_gate[0], m_w_gate[0], v_w_gate[0])
    out_w["w_up"] = _adamw("adamw_w_up", rwu, w_up[0], m_w_up[0], v_w_up[0])
    (rwout,) = _exchange_wait("dwout_exchange_wait", ex_dwout, out_w["w_up"][0])
    out_w["w_out"] = _adamw("adamw_w_out", rwout, w_out[0], m_w_out[0], v_w_out[0])
    (rwin,) = _exchange_wait("dwin_exchange_wait", ex_dwin, out_w["w_out"][0])
    out_w["w_in"] = _adamw("adamw_w_in", rwin, w_in[0], m_w_in[0], v_w_in[0])
    out_w = {name: [o[None] for o in outs] for name, outs in out_w.items()}
    cat = lambda vs: jnp.concatenate(vs, axis=1)
    gain_out = _adamw("adamw_gains", rg, cat(gains), cat(m_gains), cat(v_gains))
    widths = [d, d // 2, d // 2, d]
    cuts = [sum(widths[:i + 1]) for i in range(4)]
    gain_split = [jnp.split(o, cuts, axis=1) for o in gain_out]

    def ordered(kind):
        gs = gain_split[kind]
        return (gs[0], out_w["w_in"][kind], gs[1], gs[2], out_w["w_out"][kind], gs[3], out_w["w_gate"][kind],
                out_w["w_up"][kind], out_w["w_down"][kind], gs[4].reshape(d))

    return (loss, dx.reshape(bl, t, d), *ordered(0), *ordered(1), *ordered(2), *ordered(3))
```

```python
import functools
import math

import jax
import jax.numpy as jnp
from jax import lax
from jax.experimental import pallas as pl
from jax.experimental.pallas import tpu as pltpu

F32 = jnp.float32
BF16 = jnp.bfloat16
S = jax.ShapeDtypeStruct

N_DEV = 8
HEAD_DIM = 128
BLK = 128
SBQ = 512
SBK = 256
DIL_UNROLL = 8
ROPE_HALF = 16
ROPE_THETA = 500000.0
RMS_EPS = 1e-5
DILATIONS = (1, 4, 16)
NEG = -1e30
VMEM_LIMIT = 56 * 1024 * 1024
RED_TILE = 1024

ADAM_LR = 0.001
ADAM_B1 = 0.9
ADAM_B2 = 0.999
ADAM_EPS = 1e-08
ADAM_WD = 0.01
ADAM_STEP = 10

MESH = pl.DeviceIdType.MESH
HBM_SPEC = pl.BlockSpec(memory_space=pltpu.HBM)
SEM_SPEC = pl.BlockSpec(memory_space=pltpu.SEMAPHORE)
SIDE_EFFECT = pltpu.SideEffectType.DATAFLOW_SIDE_EFFECTING


def _cp(n_axes):
    return pltpu.CompilerParams(dimension_semantics=("arbitrary",) * n_axes, vmem_limit_bytes=VMEM_LIMIT)


def _tile(n, want):
    if n <= want:
        return n
    t = want
    while t >= 16:
        if n % t == 0 and t % 16 == 0:
            return t
        t -= 16
    return n


NN = (((1,), (0,)), ((), ()))
NT = (((1,), (1,)), ((), ()))
TN = (((0,), (0,)), ((), ()))


def _matmul(name, grid, red_axis, ins, in_specs, terms, dims, acc_shapes, out_shapes, out_specs, epilogue, after=None):
    if after is not None:
        ins, in_specs = (*ins, after), [*in_specs, pl.BlockSpec(memory_space=pl.ANY)]
    n_in, n_out = len(ins), len(out_shapes)
    n_red = grid[red_axis]

    def body(*refs):
        in_refs, out_refs, acc_refs = refs[:n_in], refs[n_in:n_in + n_out], refs[n_in + n_out:]
        k = pl.program_id(red_axis)

        @pl.when(k == 0)
        def _():
            for acc in acc_refs:
                acc[...] = jnp.zeros_like(acc)

        for a_idx, b_idx, acc_idx in terms:
            a = in_refs[a_idx][...].astype(BF16)
            b = in_refs[b_idx][...].astype(BF16)
            acc_refs[acc_idx][...] += lax.dot_general(a, b, dims, preferred_element_type=F32)

        @pl.when(k == n_red - 1)
        def _():
            epilogue(acc_refs, in_refs, out_refs)

    return pl.pallas_call(
        body, name=name, grid=grid, in_specs=in_specs, out_specs=out_specs, out_shape=out_shapes,
        scratch_shapes=[pltpu.VMEM(s, F32) for s in acc_shapes], compiler_params=_cp(len(grid)),
    )(*ins)


def _store_epilogue(acc_refs, in_refs, out_refs):
    for acc, out in zip(acc_refs, out_refs):
        out[...] = acc[...].astype(out.dtype)


def _rope_fwd(a, c, s1, s2):
    return a * c + pltpu.roll(a, ROPE_HALF, 1) * s1 + pltpu.roll(a, HEAD_DIM - ROPE_HALF, 1) * s2


def _rope_bwd(d, c, s1, s2):
    return d * c + pltpu.roll(d * s1, HEAD_DIM - ROPE_HALF, 1) + pltpu.roll(d * s2, ROPE_HALF, 1)


def _proj_fwd(hn, w_all, rc, rs1, rs2, n_rope_heads):
    n, d = hn.shape
    _, _, ws = w_all.shape
    tm, tk = _tile(n, 1024), _tile(d, RED_TILE)
    heads_per_shard = ws // HEAD_DIM
    rows = _tile(tm, 256)

    def epilogue(acc_refs, in_refs, out_refs):
        acc, out = acc_refs[0], out_refs[0]
        j = pl.program_id(0)
        for r0 in range(0, tm, rows):
            c, s1, s2 = (ref[pl.ds(r0, rows), :] for ref in in_refs[2:5])
            for hh in range(heads_per_shard):
                a = acc[pl.ds(r0, rows), pl.ds(hh * HEAD_DIM, HEAD_DIM)]
                roped = _rope_fwd(a, c, s1, s2)
                a = jnp.where(j * heads_per_shard + hh < n_rope_heads, roped, a)
                out[pl.ds(r0, rows), pl.ds(hh * HEAD_DIM, HEAD_DIM)] = a.astype(out.dtype)

    tab = pl.BlockSpec((tm, HEAD_DIM), lambda j, m, k: (m, 0))
    return _matmul(
        "proj_fwd", (N_DEV, n // tm, d // tk), 2, (hn, w_all, rc, rs1, rs2),
        [pl.BlockSpec((tm, tk), lambda j, m, k: (m, k)), pl.BlockSpec((None, tk, ws), lambda j, m, k: (j, k, 0)), tab, tab, tab],
        [(0, 1, 0)], NN, [(tm, ws)], [S((n, N_DEV * ws), BF16)], [pl.BlockSpec((tm, ws), lambda j, m, k: (m, j))], epilogue)[0]


def _dense_res(name, a, b, res, dims, out_dtype=F32):
    m, kdim = a.shape
    n = b.shape[1] if dims == NN else b.shape[0]
    tm, tn, tk = _tile(m, 1024), _tile(n, 1024), _tile(kdim, RED_TILE)
    ins = [a, b] + ([res] if res is not None else [])
    b_spec = pl.BlockSpec((tk, tn), lambda i, j, k: (k, j)) if dims == NN else pl.BlockSpec((tn, tk), lambda i, j, k: (j, k))
    specs = [pl.BlockSpec((tm, tk), lambda i, j, k: (i, k)), b_spec]
    if res is not None:
        specs.append(pl.BlockSpec((tm, tn), lambda i, j, k: (i, j)))

    def epilogue(acc_refs, in_refs, out_refs):
        v = acc_refs[0][...]
        if res is not None:
            v = v + in_refs[2][...]
        out_refs[0][...] = v.astype(out_dtype)

    return _matmul(name, (m // tm, n // tn, kdim // tk), 2, ins, specs, [(0, 1, 0)], dims, [(tm, tn)],
                   [S((m, n), out_dtype)], [pl.BlockSpec((tm, tn), lambda i, j, k: (i, j))], epilogue)[0]


def _tn_full(name, a, b, out_dtype=BF16):
    m, kdim = a.shape
    n = b.shape[1]
    tm, tk, tn = _tile(m, RED_TILE), _tile(kdim, 1024), _tile(n, 1024)
    return _matmul(name, (kdim // tk, n // tn, m // tm), 2, (a, b),
                   [pl.BlockSpec((tm, tk), lambda i, j, t: (t, i)), pl.BlockSpec((tm, tn), lambda i, j, t: (t, j))],
                   [(0, 1, 0)], TN, [(tk, tn)], [S((kdim, n), out_dtype)], [pl.BlockSpec((tk, tn), lambda i, j, t: (i, j))],
                   _store_epilogue)[0]


def _gateup_fwd(hn, wg_all, wu_all):
    n, d = hn.shape
    _, _, fs = wg_all.shape
    tm, tk = _tile(n, 1024), _tile(d, RED_TILE)
    rows = _tile(tm, 256)

    def epilogue(acc_refs, in_refs, out_refs):
        for r0 in range(0, tm, rows):
            g = acc_refs[0][pl.ds(r0, rows), :]
            u = acc_refs[1][pl.ds(r0, rows), :]
            out_refs[0][pl.ds(r0, rows), :] = g.astype(BF16)
            out_refs[1][pl.ds(r0, rows), :] = u.astype(BF16)
            out_refs[2][pl.ds(r0, rows), :] = (g * jax.nn.sigmoid(g) * u).astype(BF16)

    w_spec = pl.BlockSpec((None, tk, fs), lambda j, m, k: (j, k, 0))
    o_spec = pl.BlockSpec((None, tm, fs), lambda j, m, k: (j, m, 0))
    o_shape = S((N_DEV, n, fs), BF16)
    return _matmul("gateup_fwd", (N_DEV, n // tm, d // tk), 2, (hn, wg_all, wu_all),
                   [pl.BlockSpec((tm, tk), lambda j, m, k: (m, k)), w_spec, w_spec],
                   [(0, 1, 0), (0, 2, 1)], NN, [(tm, fs), (tm, fs)], [o_shape] * 3, [o_spec] * 3, epilogue)


def _down_fwd(act, wd_all, res):
    _, n, fs = act.shape
    d = wd_all.shape[2]
    tm, tn = _tile(n, 1024), _tile(d, 1024)

    def epilogue(acc_refs, in_refs, out_refs):
        out_refs[0][...] = acc_refs[0][...] + in_refs[2][...]

    return _matmul("down_fwd", (n // tm, d // tn, N_DEV), 2, (act, wd_all, res),
                   [pl.BlockSpec((None, tm, fs), lambda i, j, s: (s, i, 0)), pl.BlockSpec((None, fs, tn), lambda i, j, s: (s, 0, j)),
                    pl.BlockSpec((tm, tn), lambda i, j, s: (i, j))],
                   [(0, 1, 0)], NN, [(tm, tn)], [S((n, d), F32)], [pl.BlockSpec((tm, tn), lambda i, j, s: (i, j))], epilogue)[0]


def _dact_bwd(dh, wd_all, gate, up):
    n, d = dh.shape
    _, fs, _ = wd_all.shape
    tm, tk = _tile(n, 1024), _tile(d, RED_TILE)
    rows = _tile(tm, 256)

    def epilogue(acc_refs, in_refs, out_refs):
        for r0 in range(0, tm, rows):
            da = acc_refs[0][pl.ds(r0, rows), :]
            g = in_refs[2][pl.ds(r0, rows), :].astype(F32)
            u = in_refs[3][pl.ds(r0, rows), :].astype(F32)
            sg = jax.nn.sigmoid(g)
            out_refs[0][pl.ds(r0, rows), :] = (da * u * (sg * (1.0 + g * (1.0 - sg)))).astype(BF16)
            out_refs[1][pl.ds(r0, rows), :] = (da * (g * sg)).astype(BF16)

    t_spec = pl.BlockSpec((None, tm, fs), lambda j, m, k: (j, m, 0))
    o_shape = S((N_DEV, n, fs), BF16)
    return _matmul("dact_bwd", (N_DEV, n // tm, d // tk), 2, (dh, wd_all, gate, up),
                   [pl.BlockSpec((tm, tk), lambda j, m, k: (m, k)), pl.BlockSpec((None, fs, tk), lambda j, m, k: (j, 0, k)), t_spec, t_spec],
                   [(0, 1, 0)], NT, [(tm, fs)], [o_shape] * 2, [t_spec] * 2, epilogue)


def _dwd_bwd(act, dh):
    _, n, fs = act.shape
    d = dh.shape[1]
    tm, tn = _tile(n, RED_TILE), _tile(d, 1024)
    return _matmul("dwd_bwd", (N_DEV, d // tn, n // tm), 2, (act, dh),
                   [pl.BlockSpec((None, tm, fs), lambda j, c, t: (j, t, 0)), pl.BlockSpec((tm, tn), lambda j, c, t: (t, c))],
                   [(0, 1, 0)], TN, [(fs, tn)], [S((N_DEV, fs, d), BF16)], [pl.BlockSpec((None, fs, tn), lambda j, c, t: (j, 0, c))],
                   _store_epilogue)[0]


def _dw_cols_bwd(name, hn, dys, after=None):
    n, d = hn.shape
    ws = dys[0].shape[2]
    tm, tk = _tile(n, RED_TILE), _tile(d, 1024)
    k_out = len(dys)
    y_spec = pl.BlockSpec((None, tm, ws), lambda j, c, t: (j, t, 0))
    o_spec = pl.BlockSpec((None, tk, ws), lambda j, c, t: (j, c, 0))
    return _matmul(name, (N_DEV, d // tk, n // tm), 2, (hn, *dys),
                   [pl.BlockSpec((tm, tk), lambda j, c, t: (t, c))] + [y_spec] * k_out,
                   [(0, 1 + i, i) for i in range(k_out)], TN, [(tk, ws)] * k_out, [S((N_DEV, d, ws), BF16)] * k_out, [o_spec] * k_out,
                   _store_epilogue, after)


def _dwin_bwd(hn, dproj):
    n, d = hn.shape
    ws = dproj.shape[1] // N_DEV
    tm, tk = _tile(n, RED_TILE), _tile(d, 1024)
    return _matmul("dwin_bwd", (N_DEV, d // tk, n // tm), 2, (hn, dproj),
                   [pl.BlockSpec((tm, tk), lambda j, c, t: (t, c)), pl.BlockSpec((tm, ws), lambda j, c, t: (t, j))],
                   [(0, 1, 0)], TN, [(tk, ws)], [S((N_DEV, d, ws), BF16)], [pl.BlockSpec((None, tk, ws), lambda j, c, t: (j, c, 0))],
                   _store_epilogue)[0]


def _dhn_from_shards(name, dys, ws_all, dy_is_flat, after=None):
    if dy_is_flat:
        n, ws = dys[0].shape[0], dys[0].shape[1] // N_DEV
    else:
        _, n, ws = dys[0].shape
    d = ws_all[0].shape[1]
    tm, tn = _tile(n, 1024), _tile(d, 1024)
    if dy_is_flat:
        y_spec = pl.BlockSpec((tm, ws), lambda i, j, s: (i, s))
    else:
        y_spec = pl.BlockSpec((None, tm, ws), lambda i, j, s: (s, i, 0))
    w_spec = pl.BlockSpec((None, tn, ws), lambda i, j, s: (s, j, 0))
    k_terms = len(dys)
    return _matmul(name, (n // tm, d // tn, N_DEV), 2, (*dys, *ws_all), [y_spec] * k_terms + [w_spec] * k_terms,
                   [(i, k_terms + i, 0) for i in range(k_terms)], NT, [(tm, tn)], [S((n, d), F32)],
                   [pl.BlockSpec((tm, tn), lambda i, j, s: (i, j))], _store_epilogue, after)[0]


def _rms_fwd(name, x, g):
    n, d = x.shape
    tm = _tile(n, 256)

    def body(x_ref, g_ref, o_ref):
        xv = x_ref[...]
        r = lax.rsqrt(jnp.mean(xv * xv, axis=-1, keepdims=True) + RMS_EPS)
        o_ref[...] = (xv * r * g_ref[...]).astype(BF16)

    return pl.pallas_call(body, name=name, grid=(n // tm,),
                          in_specs=[pl.BlockSpec((tm, d), lambda i: (i, 0)), pl.BlockSpec((1, d), lambda i: (0, 0))],
                          out_specs=pl.BlockSpec((tm, d), lambda i: (i, 0)), out_shape=S((n, d), BF16), compiler_params=_cp(1))(x, g)


def _rms_bwd(name, x, g, dy, res):
    n, d = x.shape
    tm = _tile(n, 256)

    def body(x_ref, g_ref, dy_ref, res_ref, dx_ref, dg_ref):
        xv, dyv = x_ref[...], dy_ref[...]
        r = lax.rsqrt(jnp.mean(xv * xv, axis=-1, keepdims=True) + RMS_EPS)
        xr = xv * r
        dgy = dyv * g_ref[...]
        dx_ref[...] = res_ref[...] + r * (dgy - xr * jnp.mean(dgy * xr, axis=-1, keepdims=True))

        @pl.when(pl.program_id(0) == 0)
        def _():
            dg_ref[...] = jnp.zeros_like(dg_ref)

        dg_ref[...] += jnp.sum(dyv * xr, axis=0, keepdims=True)

    row = pl.BlockSpec((tm, d), lambda i: (i, 0))
    vec = pl.BlockSpec((1, d), lambda i: (0, 0))
    return pl.pallas_call(body, name=name, grid=(n // tm,), in_specs=[row, vec, row, row], out_specs=[row, vec],
                          out_shape=[S((n, d), F32), S((1, d), F32)], compiler_params=_cp(1))(x, g, dy, res)


def _final_loss(h, g, target):
    n, d = h.shape
    tm = _tile(n, 256)

    def body(h_ref, g_ref, t_ref, dh_ref, dhb_ref, dg_ref, sse_ref):
        hv, gv = h_ref[...], g_ref[...]
        r = lax.rsqrt(jnp.mean(hv * hv, axis=-1, keepdims=True) + RMS_EPS)
        hr = hv * r
        err = hr * gv - t_ref[...]
        dy = err * (1.0 / d)
        dgy = dy * gv
        dh = r * (dgy - hr * jnp.mean(dgy * hr, axis=-1, keepdims=True))
        dh_ref[...] = dh
        dhb_ref[...] = dh.astype(BF16)

        @pl.when(pl.program_id(0) == 0)
        def _():
            dg_ref[...] = jnp.zeros_like(dg_ref)
            sse_ref[...] = jnp.zeros_like(sse_ref)

        dg_ref[...] += jnp.sum(dy * hr, axis=0, keepdims=True)
        sse_ref[...] += jnp.sum(err * err)

    row = pl.BlockSpec((tm, d), lambda i: (i, 0))
    vec = pl.BlockSpec((1, d), lambda i: (0, 0))
    one = pl.BlockSpec((8, HEAD_DIM), lambda i: (0, 0))
    return pl.pallas_call(body, name="final_loss", grid=(n // tm,), in_specs=[row, vec, row], out_specs=[row, row, vec, one],
                          out_shape=[S((n, d), F32), S((n, d), BF16), S((1, d), F32), S((8, HEAD_DIM), F32)],
                          compiler_params=_cp(1))(h, g, target)


def _dot(a, b, dims):
    return lax.dot_general(a, b, dims, preferred_element_type=F32)


def _split3(x):
    hi = x.astype(BF16)
    r1 = x - hi.astype(F32)
    mid = r1.astype(BF16)
    lo = (r1 - mid.astype(F32)).astype(BF16)
    return hi, mid, lo


def _scan_cols(x, tri, terms):
    parts = _split3(x)[:terms]
    out = _dot(parts[0], tri, NN)
    for p in parts[1:]:
        out = out + _dot(p, tri, NN)
    return out


def _head_norm_fwd(o, g):
    r = lax.rsqrt(jnp.mean(o * o, axis=-1, keepdims=True) + RMS_EPS)
    return o * r * g


def _head_norm_bwd(o, g, dy):
    r = lax.rsqrt(jnp.mean(o * o, axis=-1, keepdims=True) + RMS_EPS)
    orr = o * r
    dgy = dy * g
    return r * (dgy - orr * jnp.mean(dgy * orr, axis=-1, keepdims=True)), dy * orr


def _interleave_plan(t):
    return [(i, dil, t // dil) for i, dil in enumerate(DILATIONS)]


def _band_mask(u, blocks_per_seq):
    row = lax.broadcasted_iota(jnp.int32, (BLK, 2 * BLK), 0)
    col = lax.broadcasted_iota(jnp.int32, (BLK, 2 * BLK), 1)
    dist = row + BLK - col
    has_prev = (u % blocks_per_seq) != 0
    return (dist >= 0) & (dist <= BLK) & ((col >= BLK) | has_prev)


def _dil_fwd(proj, g_dil, bl, t, n_heads):
    n = bl * t
    nb = t // BLK
    scale = HEAD_DIM ** -0.5
    plan = _interleave_plan(t)
    chunk = _tile(t, 256)

    def body(q_ref, k_ref, v_ref, g_ref, omix_ref, opre_ref, lse_ref, stg, qd, kd, vd, ob, lb, on, ln):
        for src, dst, pad in ((q_ref, qd, 0), (k_ref, kd, BLK), (v_ref, vd, BLK)):
            stg[...] = src[...].astype(F32)
            for bi, dil, sub in plan:
                if pad:
                    dst[bi, pl.ds(0, BLK), :] = jnp.zeros((BLK, HEAD_DIM), BF16)
                if dil == 1:
                    dst[bi, pl.ds(pad, t), :] = src[...]
                else:
                    for r in range(dil):
                        dst[bi, pl.ds(pad + r * sub, sub), :] = stg[pl.ds(r, sub, stride=dil), :].astype(BF16)

        for bi, dil, sub in plan:
            def blk(u, carry, bi=bi, sub=sub):
                rows = pl.ds(pl.multiple_of(u * BLK, BLK), BLK)
                win = pl.ds(pl.multiple_of(u * BLK, BLK), 2 * BLK)
                sc = _dot(qd[bi, rows, :], kd[bi, win, :], NT) * scale
                sc = jnp.where(_band_mask(u, sub // BLK), sc, NEG)
                m = jnp.max(sc, axis=-1, keepdims=True)
                p = jnp.exp(sc - m)
                den = jnp.sum(p, axis=-1, keepdims=True)
                ob[bi, rows, :] = _dot((p / den).astype(BF16), vd[bi, win, :], NN)
                lb[bi, rows, :] = jnp.broadcast_to(m + jnp.log(den), (BLK, HEAD_DIM))
                return carry
            lax.fori_loop(0, nb // DIL_UNROLL, lambda i, c, blk=blk: [blk(i * DIL_UNROLL + s, c) for s in range(DIL_UNROLL)][-1], 0)

        for bi, dil, sub in plan[1:]:
            for r in range(dil):
                on[bi - 1, pl.ds(r, sub, stride=dil), :] = ob[bi, pl.ds(r * sub, sub), :]
                ln[bi - 1, pl.ds(r, sub, stride=dil), :] = lb[bi, pl.ds(r * sub, sub), :]

        def merge(i, carry):
            rows = pl.ds(pl.multiple_of(i * chunk, chunk), chunk)
            l0, l1, l2 = lb[0, rows, :], ln[0, rows, :], ln[1, rows, :]
            mx = jnp.maximum(jnp.maximum(l0, l1), l2)
            w0, w1, w2 = jnp.exp(l0 - mx), jnp.exp(l1 - mx), jnp.exp(l2 - mx)
            tot = w0 + w1 + w2
            o = (w0 / tot) * ob[0, rows, :] + (w1 / tot) * on[0, rows, :] + (w2 / tot) * on[1, rows, :]
            lse_ref[rows, :] = mx + jnp.log(tot)
            opre_ref[rows, :] = o
            omix_ref[rows, :] = _head_norm_fwd(o, g_ref[...]).astype(BF16)
            return carry
        lax.fori_loop(0, t // chunk, merge, 0)

    hs = n_heads
    col = lambda off: pl.BlockSpec((t, HEAD_DIM), lambda b, h: (b, off + h))
    return pl.pallas_call(
        body, name="dil_fwd", grid=(bl, hs),
        in_specs=[col(0), col(hs), col(2 * hs), pl.BlockSpec((1, HEAD_DIM), lambda b, h: (0, h))],
        out_specs=[col(0), col(0), pl.BlockSpec((t, HEAD_DIM), lambda b, h: (b * hs + h, 0))],
        out_shape=[S((n, 2 * hs * HEAD_DIM), BF16), S((n, 2 * hs * HEAD_DIM), F32), S((bl * hs * t, HEAD_DIM), F32)],
        scratch_shapes=[pltpu.VMEM((t, HEAD_DIM), F32), pltpu.VMEM((3, t, HEAD_DIM), BF16),
                        pltpu.VMEM((3, t + BLK, HEAD_DIM), BF16), pltpu.VMEM((3, t + BLK, HEAD_DIM), BF16),
                        pltpu.VMEM((3, t, HEAD_DIM), F32), pltpu.VMEM((3, t, HEAD_DIM), F32),
                        pltpu.VMEM((2, t, HEAD_DIM), F32), pltpu.VMEM((2, t, HEAD_DIM), F32)],
        compiler_params=_cp(2),
    )(proj, proj, proj, g_dil)


def _dil_bwd(proj, opre, lse, d_omix, g_dil, rc, rs1, rs2, bl, t, n_heads):
    n = bl * t
    nb = t // BLK
    scale = HEAD_DIM ** -0.5
    plan = _interleave_plan(t)
    chunk = _tile(t, 256)

    def body(q_ref, k_ref, v_ref, opre_ref, lse_ref, dy_ref, g_ref, c_ref, s1_ref, s2_ref, out_ref, dg_ref,
             stg, qd, kd, vd, dod, ldd, dqd, dkc, dkp, dvc, dvp, sk, sv):
        which = pl.program_id(2)

        @pl.when(jnp.logical_and(which == 0, pl.program_id(1) == 0))
        def _():
            dg_ref[...] = jnp.zeros_like(dg_ref)

        @pl.when(which == 0)
        def _():
            def prep(i, dg):
                rows = pl.ds(pl.multiple_of(i * chunk, chunk), chunk)
                o = opre_ref[rows, :]
                d_o, dg_rows = _head_norm_bwd(o, g_ref[...], dy_ref[rows, :])
                stg[rows, :] = d_o
                lane = lax.broadcasted_iota(jnp.int32, (chunk, HEAD_DIM), 1)
                ldd[0, rows, :] = jnp.where(lane < HEAD_DIM // 2, lse_ref[rows, :], jnp.sum(d_o * o, axis=-1, keepdims=True))
                return dg + jnp.sum(dg_rows, axis=0, keepdims=True)
            dg_ref[...] += lax.fori_loop(0, t // chunk, prep, jnp.zeros((1, HEAD_DIM), F32))

            dod[0] = stg[...].astype(BF16)
            for bi, dil, sub in plan[1:]:
                for r in range(dil):
                    dst = pl.ds(r * sub, sub)
                    dod[bi, dst, :] = stg[pl.ds(r, sub, stride=dil), :].astype(BF16)
                    ldd[bi, dst, :] = ldd[0, pl.ds(r, sub, stride=dil), :]
            for src, dst, pad in ((q_ref, qd, 0), (k_ref, kd, BLK), (v_ref, vd, BLK)):
                stg[...] = src[...].astype(F32)
                for bi, dil, sub in plan:
                    if pad:
                        dst[bi, pl.ds(0, BLK), :] = jnp.zeros((BLK, HEAD_DIM), BF16)
                    if dil == 1:
                        dst[bi, pl.ds(pad, t), :] = src[...]
                    else:
                        for r in range(dil):
                            dst[bi, pl.ds(pad + r * sub, sub), :] = stg[pl.ds(r, sub, stride=dil), :].astype(BF16)

            for bi, dil, sub in plan:
                def blk(u, carry, bi=bi, sub=sub):
                    rows = pl.ds(pl.multiple_of(u * BLK, BLK), BLK)
                    win = pl.ds(pl.multiple_of(u * BLK, BLK), 2 * BLK)
                    qb, kw, vw, dob = qd[bi, rows, :], kd[bi, win, :], vd[bi, win, :], dod[bi, rows, :]
                    sc = _dot(qb, kw, NT) * scale
                    stats = ldd[bi, rows, :]
                    p = jnp.where(_band_mask(u, sub // BLK), jnp.exp(sc - stats[:, :1]), 0.0)
                    dp = _dot(dob, vw, NT)
                    ds = (p * (dp - stats[:, HEAD_DIM // 2:HEAD_DIM // 2 + 1]) * scale).astype(BF16)
                    dqd[bi, rows, :] = _dot(ds, kw, NN)
                    dk_win = _dot(ds, qb, TN)
                    dv_win = _dot(p.astype(BF16), dob, TN)
                    dkp[bi, rows, :] = dk_win[:BLK]
                    dkc[bi, rows, :] = dk_win[BLK:]
                    dvp[bi, rows, :] = dv_win[:BLK]
                    dvc[bi, rows, :] = dv_win[BLK:]
                    return carry
                lax.fori_loop(0, nb // DIL_UNROLL, lambda i, c, blk=blk: [blk(i * DIL_UNROLL + s, c) for s in range(DIL_UNROLL)][-1], 0)

            for cur, prev, undo_rope, dst in ((dqd, None, True, out_ref), (dkc, dkp, True, sk), (dvc, dvp, False, sv)):
                def summed(bi, start, size, cur=cur, prev=prev):
                    v = cur[bi, pl.ds(start, size), :]
                    if prev is None:
                        return v
                    if start + size < t:
                        return v + prev[bi, pl.ds(start + BLK, size), :]
                    if size == BLK:
                        return v
                    return v + jnp.concatenate([prev[bi, pl.ds(start + BLK, size - BLK), :], jnp.zeros((BLK, HEAD_DIM), F32)], axis=0)
                stg[...] = summed(0, 0, t)
                for bi, dil, sub in plan[1:]:
                    for r in range(dil):
                        stg[pl.ds(r, sub, stride=dil), :] += summed(bi, r * sub, sub)
                if undo_rope:
                    dst[...] = _rope_bwd(stg[...], c_ref[...], s1_ref[...], s2_ref[...]).astype(BF16)
                else:
                    dst[...] = stg[...].astype(BF16)

        @pl.when(which == 1)
        def _():
            out_ref[...] = sk[...]

        @pl.when(which == 2)
        def _():
            out_ref[...] = sv[...]

    hs = n_heads
    col = lambda off: pl.BlockSpec((t, HEAD_DIM), lambda h, b, w: (b, off + h))
    per_head = pl.BlockSpec((t, HEAD_DIM), lambda h, b, w: (b * hs + h, 0))
    tab = pl.BlockSpec((t, HEAD_DIM), lambda h, b, w: (b, 0))
    gvec = pl.BlockSpec((1, HEAD_DIM), lambda h, b, w: (0, h))
    tb = (t, HEAD_DIM)
    tp = (t + BLK, HEAD_DIM)
    return pl.pallas_call(
        body, name="dil_bwd", grid=(hs, bl, 3),
        in_specs=[col(0), col(hs), col(2 * hs), col(0), per_head, col(0), gvec, tab, tab, tab],
        out_specs=[pl.BlockSpec((t, HEAD_DIM), lambda h, b, w: (b, w * hs + h)), gvec],
        out_shape=[S((n, 6 * hs * HEAD_DIM), BF16), S((1, hs * HEAD_DIM), F32)],
        scratch_shapes=[pltpu.VMEM(tb, F32), pltpu.VMEM((3,) + tb, BF16), pltpu.VMEM((3,) + tp, BF16), pltpu.VMEM((3,) + tp, BF16),
                        pltpu.VMEM((3,) + tb, BF16), pltpu.VMEM((3,) + tb, F32), pltpu.VMEM((3,) + tb, F32),
                        pltpu.VMEM((3,) + tb, F32), pltpu.VMEM((3,) + tb, F32), pltpu.VMEM((3,) + tb, F32), pltpu.VMEM((3,) + tb, F32),
                        pltpu.VMEM(tb, BF16), pltpu.VMEM(tb, BF16)],
        compiler_params=_cp(3),
    )(proj, proj, proj, opre, lse, d_omix, g_dil, rc, rs1, rs2)


def _sb_tile(qb, kb, scale, shift):
    z = _dot(qb, kb, NT) * scale
    tl = jnp.log(1.0 + jnp.exp(-jnp.abs(z)))
    log_not = -(jnp.maximum(z, 0.0) + tl)
    strict = None
    if shift is not None:
        row = lax.broadcasted_iota(jnp.int32, (SBQ, SBK), 0)
        col = lax.broadcasted_iota(jnp.int32, (SBQ, SBK), 1)
        strict = col + shift < row
        log_not = jnp.where(strict, log_not, 0.0)
    return log_not, jnp.minimum(z, 0.0) - tl, strict


def _tri(cmp):
    row = lax.broadcasted_iota(jnp.int32, (SBK, SBK), 0)
    col = lax.broadcasted_iota(jnp.int32, (SBK, SBK), 1)
    return jnp.where(cmp(row, col), 1.0, 0.0).astype(BF16)


SB_EDGE = tuple(range(0, SBQ, SBK))


def _sb_fwd(proj, g_sb, omix_in, opre_in, bl, t, n_heads):
    nb = t // SBQ
    per_q = SBQ // SBK
    scale = HEAD_DIM ** -0.5

    def body(q_ref, k_ref, v_ref, g_ref, _omix_in, _opre_in, omix_ref, opre_ref, lt_ref):
        later = _tri(lambda r, c: r > c)

        def q_block(qi, carry):
            rows = pl.ds(pl.multiple_of(qi * SBQ, SBQ), SBQ)
            qb = q_ref[rows, :]

            def tile(kj, st, shift):
                run, acc = st
                krows = pl.ds(pl.multiple_of(kj * SBK, SBK), SBK)
                log_not, log_beta, strict = _sb_tile(qb, k_ref[krows, :], scale, shift)
                a = jnp.exp(log_beta + _scan_cols(log_not, later, 2) + run)
                if shift is not None:
                    a = jnp.where(strict, a, 0.0)
                return run + jnp.sum(log_not, axis=-1, keepdims=True), acc + _dot(a.astype(BF16), v_ref[krows, :], NN)

            st = (jnp.zeros((SBQ, 1), F32), jnp.zeros((SBQ, HEAD_DIM), F32))
            for shift in reversed(SB_EDGE):
                st = tile(qi * per_q + shift // SBK, st, shift)
            run, acc = lax.fori_loop(0, qi * per_q, lambda it, st: tile(qi * per_q - 1 - it, st, None), st)
            lt_ref[rows, :] = jnp.broadcast_to(run, (SBQ, HEAD_DIM))
            opre_ref[rows, :] = acc
            omix_ref[rows, :] = _head_norm_fwd(acc, g_ref[...]).astype(BF16)
            return carry

        lax.fori_loop(0, nb, q_block, 0)

    hs = n_heads
    col = lambda off: pl.BlockSpec((t, HEAD_DIM), lambda b, h: (b, off + h))
    return pl.pallas_call(
        body, name="sb_fwd", grid=(bl, hs),
        in_specs=[col(3 * hs), col(4 * hs), col(5 * hs), pl.BlockSpec((1, HEAD_DIM), lambda b, h: (0, h)), HBM_SPEC, HBM_SPEC],
        out_specs=[col(hs), col(hs), pl.BlockSpec((t, HEAD_DIM), lambda b, h: (b * hs + h, 0))],
        out_shape=[S(omix_in.shape, BF16), S(opre_in.shape, F32), S((bl * hs * t, HEAD_DIM), F32)],
        input_output_aliases={4: 0, 5: 1}, compiler_params=_cp(2),
    )(proj, proj, proj, g_sb, omix_in, opre_in)


def _sb_bwd(proj, opre, ltot, d_omix, g_sb, dproj_in, bl, t, n_heads):
    nb = t // SBQ
    per_q = SBQ // SBK
    scale = HEAD_DIM ** -0.5

    def body(q_ref, k_ref, v_ref, opre_ref, lt_ref, dy_ref, g_ref, _dproj_in, out_ref, dg_ref, dq, dk, dv):
        which = pl.program_id(2)

        @pl.when(jnp.logical_and(which == 0, pl.program_id(1) == 0))
        def _():
            dg_ref[...] = jnp.zeros_like(dg_ref)

        @pl.when(which == 0)
        def _():
            upto = _tri(lambda r, c: r <= c)
            before = _tri(lambda r, c: r < c)
            dk[...] = jnp.zeros_like(dk)
            dv[...] = jnp.zeros_like(dv)

            def q_block(qi, dg):
                rows = pl.ds(pl.multiple_of(qi * SBQ, SBQ), SBQ)
                qb = q_ref[rows, :]
                o = opre_ref[rows, :]
                d_o, dg_rows = _head_norm_bwd(o, g_ref[...], dy_ref[rows, :])
                dob = d_o.astype(BF16)
                lt = lt_ref[rows, :][:, :1]

                def tile(kj, st, shift):
                    run, grun, dq_acc = st
                    krows = pl.ds(pl.multiple_of(kj * SBK, SBK), SBK)
                    kb, vb = k_ref[krows, :], v_ref[krows, :]
                    log_not, log_beta, strict = _sb_tile(qb, kb, scale, shift)
                    excl = lt - (run + _scan_cols(log_not, upto, 2))
                    a = jnp.exp(log_beta + excl)
                    if shift is not None:
                        a = jnp.where(strict, a, 0.0)
                    g_a = a * _dot(dob, vb, NT)
                    g_before = grun + _scan_cols(g_a, before, 2)
                    dz = (g_a - (g_a + g_before) * jnp.exp(log_beta)) * scale
                    if shift is not None:
                        dz = jnp.where(strict, dz, 0.0)
                    dzb = dz.astype(BF16)
                    dk[krows, :] += _dot(dzb, qb, TN)
                    dv[krows, :] += _dot(a.astype(BF16), dob, TN)
                    return (run + jnp.sum(log_not, axis=-1, keepdims=True), grun + jnp.sum(g_a, axis=-1, keepdims=True),
                            dq_acc + _dot(dzb, kb, NN))

                zero_col = jnp.zeros((SBQ, 1), F32)
                st = lax.fori_loop(0, qi * per_q, lambda kj, st: tile(kj, st, None),
                                   (zero_col, zero_col, jnp.zeros((SBQ, HEAD_DIM), F32)))
                for shift in SB_EDGE:
                    st = tile(qi * per_q + shift // SBK, st, shift)
                dq_acc = st[2]
                dq[rows, :] = dq_acc.astype(BF16)
                return dg + jnp.sum(dg_rows, axis=0, keepdims=True)

            dg_ref[...] += lax.fori_loop(0, nb, q_block, jnp.zeros((1, HEAD_DIM), F32))
            out_ref[...] = dq[...]

        @pl.when(which == 1)
        def _():
            out_ref[...] = dk[...].astype(BF16)

        @pl.when(which == 2)
        def _():
            out_ref[...] = dv[...].astype(BF16)

    hs = n_heads
    col = lambda off: pl.BlockSpec((t, HEAD_DIM), lambda h, b, w: (b, off + h))
    per_head = pl.BlockSpec((t, HEAD_DIM), lambda h, b, w: (b * hs + h, 0))
    gvec = pl.BlockSpec((1, HEAD_DIM), lambda h, b, w: (0, h))
    tb = (t, HEAD_DIM)
    return pl.pallas_call(
        body, name="sb_bwd", grid=(hs, bl, 3),
        in_specs=[col(3 * hs), col(4 * hs), col(5 * hs), col(hs), per_head, col(hs), gvec, HBM_SPEC],
        out_specs=[pl.BlockSpec((t, HEAD_DIM), lambda h, b, w: (b, (3 + w) * hs + h)), gvec],
        out_shape=[S(dproj_in.shape, BF16), S((1, hs * HEAD_DIM), F32)],
        scratch_shapes=[pltpu.VMEM(tb, BF16), pltpu.VMEM(tb, F32), pltpu.VMEM(tb, F32)],
        input_output_aliases={7: 0}, compiler_params=_cp(3),
    )(proj, proj, proj, opre, ltot, d_omix, g_sb, dproj_in)


def _all_gather(shards):
    k_w = len(shards)

    def body(*refs):
        ins, outs = refs[:k_w], refs[k_w:2 * k_w]
        send_sems, recv_sems, local_sems = refs[2 * k_w:]
        x, y, c = lax.axis_index("x"), lax.axis_index("y"), lax.axis_index("c")
        me, sibling = (x, y, c), (x, y, 1 - c)
        chips = [(1 - x, y), (x, 1 - y), (1 - x, 1 - y)]

        def slot(dev):
            return 4 * dev[0] + 2 * dev[1] + dev[2]

        def copy(w, k, block, to, src=None):
            dst = outs[w].at[slot(block)]
            return pltpu.make_async_remote_copy(
                src_ref=dst if src is None else src, dst_ref=dst, send_sem=send_sems.at[w * 7 + k], recv_sem=recv_sems.at[w * 7 + k],
                device_id=to, device_id_type=MESH)

        mine = [pltpu.make_async_copy(ins[w], outs[w].at[slot(me)], local_sems.at[w]) for w in range(k_w)]
        first = []
        for w in range(k_w):
            mine[w].start()
            first.append(copy(w, 0, me, sibling, src=ins[w]))
            first += [copy(w, 1 + j, me, (*chip, c), src=ins[w]) for j, chip in enumerate(chips)]
        for cp in first:
            cp.start()
        passed = []
        for w in range(k_w):
            for j, chip in enumerate(chips):
                copy(w, 1 + j, (*chip, c), me).wait_recv()
                fwd = copy(w, 4 + j, (*chip, c), sibling)
                fwd.start()
                passed.append(fwd)
        for w in range(k_w):
            copy(w, 0, sibling, me).wait_recv()
            for j, chip in enumerate(chips):
                copy(w, 4 + j, (*chip, 1 - c), me).wait_recv()
        for cp in first + passed:
            cp.wait_send()
        for cp in mine:
            cp.wait()

    return pl.pallas_call(
        body, name="weights_all_gather", in_specs=[HBM_SPEC] * k_w, out_specs=[HBM_SPEC] * k_w,
        out_shape=[S((N_DEV,) + s.shape, s.dtype) for s in shards],
        scratch_shapes=[pltpu.SemaphoreType.DMA((7 * k_w,)), pltpu.SemaphoreType.DMA((7 * k_w,)), pltpu.SemaphoreType.DMA((k_w,))],
    )(*shards)


def _grad_exchange(grads, after):
    k_w = len(grads)

    def body(*refs):
        ins, outs = refs[:k_w], refs[k_w + 1:2 * k_w + 1]
        send_sems, recv_sems, local_sems = refs[2 * k_w + 1:]
        local, remote = _exchange_copies(ins, outs, send_sems, recv_sems, local_sems, False)
        for cp in local + remote:
            cp.start()
        for cp in remote:
            cp.wait_send()
            cp.wait_recv()
        for cp in local:
            cp.wait()

    n_sem = (N_DEV - 1) * k_w
    return pl.pallas_call(
        body, name="grad_exchange", in_specs=[HBM_SPEC] * k_w + [pl.BlockSpec(memory_space=pl.ANY)], out_specs=[HBM_SPEC] * k_w,
        out_shape=[S(g.shape, g.dtype) for g in grads],
        scratch_shapes=[pltpu.SemaphoreType.DMA((n_sem,)), pltpu.SemaphoreType.DMA((n_sem,)), pltpu.SemaphoreType.DMA((k_w,))],
    )(*grads, after)


def _peer(x, y, c, k):
    px, py, pc = x ^ (k >> 2), y ^ ((k >> 1) & 1), c ^ (k & 1)
    return (px, py, pc), 4 * px + 2 * py + pc


def _exchange_copies(srcs, lands, send_sems, recv_sems, local_sems, gather):
    x, y, c = lax.axis_index("x"), lax.axis_index("y"), lax.axis_index("c")
    my_slot = 4 * x + 2 * y + c
    local, remote = [], []
    for w, (src, land) in enumerate(zip(srcs, lands)):
        local.append(pltpu.make_async_copy(src if gather else src.at[my_slot], land.at[my_slot], local_sems.at[w]))
        for k in range(1, N_DEV):
            peer, peer_slot = _peer(x, y, c, k)
            remote.append(pltpu.make_async_remote_copy(
                src_ref=src if gather else src.at[peer_slot], dst_ref=land.at[my_slot],
                send_sem=send_sems.at[w * (N_DEV - 1) + k - 1], recv_sem=recv_sems.at[w * (N_DEV - 1) + k - 1],
                device_id=peer, device_id_type=MESH))
    return local, remote


def _exchange_start(name, srcs, gather, after):
    k_w = len(srcs)
    land_shapes = [((N_DEV,) + s.shape) if gather else s.shape for s in srcs]

    def body(*refs):
        src_refs, land_refs = refs[:k_w], refs[k_w:2 * k_w]
        send_sems, recv_sems, local_sems = refs[2 * k_w + 1:2 * k_w + 4]
        token = refs[-1]
        local, remote = _exchange_copies(src_refs, land_refs, send_sems, recv_sems, local_sems, gather)
        for cp in local + remote:
            cp.start()
        token[...] = jnp.zeros_like(token)

    n_sem = (N_DEV - 1) * k_w
    hbm = lambda a: pltpu.with_memory_space_constraint(a, pltpu.HBM)
    outs = pl.pallas_call(
        body, name=name,
        in_specs=[HBM_SPEC] * (2 * k_w) + [pl.BlockSpec(memory_space=pl.ANY)],
        out_shape=(pltpu.SemaphoreType.DMA((n_sem,)), pltpu.SemaphoreType.DMA((n_sem,)), pltpu.SemaphoreType.DMA((k_w,)),
                   *[pltpu.HBM(s.shape, s.dtype) for s in srcs], *[pltpu.HBM(ls, s.dtype) for ls, s in zip(land_shapes, srcs)],
                   S((8, HEAD_DIM), F32)),
        out_specs=(SEM_SPEC, SEM_SPEC, SEM_SPEC, *[HBM_SPEC] * (2 * k_w), pl.BlockSpec(memory_space=pltpu.VMEM)),
        input_output_aliases={i: 3 + i for i in range(2 * k_w)},
        compiler_params=pltpu.CompilerParams(has_side_effects=SIDE_EFFECT),
    )(*[hbm(s) for s in srcs], *[hbm(lax.empty(ls, s.dtype)) for ls, s in zip(land_shapes, srcs)], after)
    return dict(sems=outs[:3], srcs=outs[3:3 + k_w], lands=outs[3 + k_w:3 + 2 * k_w], token_block=outs[-1], token=outs[-1][0, 0], gather=gather)


def _exchange_wait(name, handle, after):
    k_w = len(handle["srcs"])
    gather = handle["gather"]

    def body(*refs):
        src_refs, land_refs = refs[:k_w], refs[k_w:2 * k_w]
        send_sems, recv_sems, local_sems = refs[2 * k_w:2 * k_w + 3]
        local, remote = _exchange_copies(src_refs, land_refs, send_sems, recv_sems, local_sems, gather)
        for cp in local:
            cp.wait()
        for cp in remote:
            cp.wait_send()
            cp.wait_recv()

    outs = pl.pallas_call(
        body, name=name,
        in_specs=[HBM_SPEC] * (2 * k_w) + [SEM_SPEC] * 3 + [pl.BlockSpec(memory_space=pl.ANY)],
        out_shape=tuple(pltpu.HBM(a.shape, a.dtype) for a in (*handle["srcs"], *handle["lands"])),
        out_specs=tuple([HBM_SPEC] * (2 * k_w)),
        input_output_aliases={i: i for i in range(2 * k_w)},
        compiler_params=pltpu.CompilerParams(has_side_effects=SIDE_EFFECT),
    )(*handle["srcs"], *handle["lands"], *handle["sems"], after)
    return outs[k_w:]


def _adamw(name, parts, w, m, v):
    r, c = w.shape
    tr = _tile(r, max(16, (1 << 19) // c // 16 * 16))

    def body(p_ref, w_ref, m_ref, v_ref, g_ref, d_ref, nm_ref, nv_ref):
        g = p_ref[0].astype(F32)
        for s in range(1, N_DEV):
            g = g + p_ref[s].astype(F32)
        m_new = ADAM_B1 * m_ref[...] + (1.0 - ADAM_B1) * g
        v_new = ADAM_B2 * v_ref[...] + (1.0 - ADAM_B2) * jnp.square(g)
        m_hat = m_new / (1.0 - ADAM_B1 ** ADAM_STEP)
        v_hat = v_new / (1.0 - ADAM_B2 ** ADAM_STEP)
        g_ref[...] = g
        d_ref[...] = -ADAM_LR * (m_hat / (jnp.sqrt(v_hat) + ADAM_EPS) + ADAM_WD * w_ref[...])
        nm_ref[...] = m_new
        nv_ref[...] = v_new

    blk = pl.BlockSpec((tr, c), lambda i: (i, 0))
    return pl.pallas_call(body, name=name, grid=(r // tr,), in_specs=[pl.BlockSpec((N_DEV, tr, c), lambda i: (0, i, 0)), blk, blk, blk],
                          out_specs=[blk] * 4, out_shape=[S((r, c), F32)] * 4, compiler_params=_cp(1))(parts, w, m, v)


def _rope_tables(positions):
    inv_freq = jnp.power(jnp.float32(ROPE_THETA), -jnp.arange(ROPE_HALF, dtype=F32) / ROPE_HALF)
    ang = positions.astype(F32).reshape(-1, 1) * inv_freq
    cos, sin = jnp.cos(ang), jnp.sin(ang)
    n = ang.shape[0]
    rest = HEAD_DIM - 2 * ROPE_HALF
    zeros = jnp.zeros((n, ROPE_HALF), F32)
    c = jnp.concatenate([cos, cos, jnp.ones((n, rest), F32)], axis=1)
    s1 = jnp.concatenate([zeros, sin, jnp.zeros((n, rest), F32)], axis=1)
    s2 = jnp.concatenate([-sin, zeros, jnp.zeros((n, rest), F32)], axis=1)
    return c, s1, s2


def kernel(x, positions, norm_mix_g, w_in, norm_out_dil_g, norm_out_sb_g, w_out, norm_ffn_g, w_gate, w_up, w_down, norm_final_g, loss_target, m_norm_mix_g, m_w_in, m_norm_out_dil_g, m_norm_out_sb_g, m_w_out, m_norm_ffn_g, m_w_gate, m_w_up, m_w_down, m_norm_final_g, v_norm_mix_g, v_w_in, v_norm_out_dil_g, v_norm_out_sb_g, v_w_out, v_norm_ffn_g, v_w_gate, v_w_up, v_w_down, v_norm_final_g):
    bl, t, d = x.shape
    n = bl * t
    hs = d // (2 * HEAD_DIM)
    x2 = x.reshape(n, d)
    target = loss_target.reshape(n, d)
    g_final = norm_final_g.reshape(1, d)
    rc, rs1, rs2 = _rope_tables(positions)

    (win_all,) = _all_gather([w_in[0].astype(BF16)])
    ex_wout = _exchange_start("wout_gather_start", [w_out[0].astype(BF16)], True, win_all)
    ex_wgu = _exchange_start("wgu_gather_start", [w_gate[0].astype(BF16), w_up[0].astype(BF16)], True, ex_wout["token_block"])
    ex_wd = _exchange_start("wd_gather_start", [w_down[0].astype(BF16)], True, ex_wgu["token_block"])
    rc = rc + ex_wd["token"]

    hn1 = _rms_fwd("rms_mix_fwd", x2, norm_mix_g)
    proj = _proj_fwd(hn1, win_all, rc, rs1, rs2, 2 * hs)
    omix, opre, lse = _dil_fwd(proj, norm_out_dil_g, bl, t, hs)
    omix, opre, ltot = _sb_fwd(proj, norm_out_sb_g, omix, opre, bl, t, hs)
    (wout_all,) = _exchange_wait("wout_gather_wait", ex_wout, ltot)
    wout_full = wout_all.reshape(d, d)
    h1 = _dense_res("out_fwd", omix, wout_full, x2, NN)
    hn2 = _rms_fwd("rms_ffn_fwd", h1, norm_ffn_g)
    wg_all, wu_all = _exchange_wait("wgu_gather_wait", ex_wgu, hn2)
    gate, up, act = _gateup_fwd(hn2, wg_all, wu_all)
    (wd_all,) = _exchange_wait("wd_gather_wait", ex_wd, act)
    h2 = _down_fwd(act, wd_all, h1)
    dh2, dh2b, dg_final, sse = _final_loss(h2, g_final, target)
    loss = lax.psum(sse[0, 0], ("x", "y", "c")) * (0.5 / d)

    dgate, dup = _dact_bwd(dh2b, wd_all, gate, up)
    dwd = _dwd_bwd(act, dh2b)
    ex_dwd = _exchange_start("dwd_exchange_start", [dwd], False, dwd)
    dwg, dwu = _dw_cols_bwd("dwgu_bwd", hn2, (dgate, dup), ex_dwd["token_block"])
    ex_dwgu = _exchange_start("dwgu_exchange_start", [dwg, dwu], False, ex_dwd["token_block"])
    dhn2 = _dhn_from_shards("dhn2_bwd", (dgate, dup), (wg_all, wu_all), False, ex_dwgu["token_block"])
    dh1, dg_ffn = _rms_bwd("rms_ffn_bwd", h1, norm_ffn_g, dhn2, dh2)
    d_omix = _dense_res("domix_bwd", dh1, wout_full, None, NT)
    dwout = _tn_full("dwout_bwd", omix, dh1).reshape(N_DEV, d // N_DEV, d)
    ex_dwout = _exchange_start("dwout_exchange_start", [dwout], False, dwout)
    dproj, dg_dil = _dil_bwd(proj, opre, lse, d_omix, norm_out_dil_g + ex_dwout["token"], rc, rs1, rs2, bl, t, hs)
    dproj, dg_sb = _sb_bwd(proj, opre, ltot, d_omix, norm_out_sb_g, dproj, bl, t, hs)
    dwin = _dwin_bwd(hn1, dproj)
    ex_dwin = _exchange_start("dwin_exchange_start", [dwin], False, dwin)
    dhn1 = _dhn_from_shards("dhn1_bwd", (dproj,), (win_all,), True, ex_dwin["token_block"])
    dx, dg_mix = _rms_bwd("rms_mix_bwd", x2, norm_mix_g, dhn1, dh1)

    gains = [norm_mix_g, norm_out_dil_g, norm_out_sb_g, norm_ffn_g, g_final]
    m_gains = [m_norm_mix_g, m_norm_out_dil_g, m_norm_out_sb_g, m_norm_ffn_g, m_norm_final_g.reshape(1, d)]
    v_gains = [v_norm_mix_g, v_norm_out_dil_g, v_norm_out_sb_g, v_norm_ffn_g, v_norm_final_g.reshape(1, d)]
    dg_vec = jnp.concatenate([dg_mix, dg_dil, dg_sb, dg_ffn, dg_final], axis=1)
    dg_all = jnp.broadcast_to(dg_vec[None], (N_DEV,) + dg_vec.shape)

    out_w = {}
    (rwd,) = _exchange_wait("dwd_exchange_wait", ex_dwd, dx)
    out_w["w_down"] = _adamw("adamw_w_down", rwd, w_down[0], m_w_down[0], v_w_down[0])
    rwg, rwu = _exchange_wait("dwgu_exchange_wait", ex_dwgu, out_w["w_down"][0])
    out_w["w_gate"] = _adamw("adamw_w_gate", rwg, w_gate[0], m_w_gate[0], v_w_gate[0])
    out_w["w_up"] = _adamw("adamw_w_up", rwu, w_up[0], m_w_up[0], v_w_up[0])
    (rwout,) = _exchange_wait("dwout_exchange_wait", ex_dwout, out_w["w_up"][0])
    out_w["w_out"] = _adamw("adamw_w_out", rwout, w_out[0], m_w_out[0], v_w_out[0])
    (rwin,) = _exchange_wait("dwin_exchange_wait", ex_dwin, out_w["w_out"][0])
    out_w["w_in"] = _adamw("adamw_w_in", rwin, w_in[0], m_w_in[0], v_w_in[0])
    (rg,) = _grad_exchange([dg_all], out_w["w_in"][0])
    out_w = {name: [o[None] for o in outs] for name, outs in out_w.items()}
    cat = lambda vs: jnp.concatenate(vs, axis=1)
    gain_out = _adamw("adamw_gains", rg, cat(gains), cat(m_gains), cat(v_gains))
    widths = [d, d // 2, d // 2, d]
    cuts = [sum(widths[:i + 1]) for i in range(4)]
    gain_split = [jnp.split(o, cuts, axis=1) for o in gain_out]

    def ordered(kind):
        gs = gain_split[kind]
        return (gs[0], out_w["w_in"][kind], gs[1], gs[2], out_w["w_out"][kind], gs[3], out_w["w_gate"][kind],
                out_w["w_up"][kind], out_w["w_down"][kind], gs[4].reshape(d))

    return (loss, dx.reshape(bl, t, d), *ordered(0), *ordered(1), *ordered(2), *ordered(3))
```

```python
import functools
import math

import jax
import jax.numpy as jnp
from jax import lax
from jax.experimental import pallas as pl
from jax.experimental.pallas import tpu as pltpu

F32 = jnp.float32
BF16 = jnp.bfloat16
S = jax.ShapeDtypeStruct

N_DEV = 8
HEAD_DIM = 128
BLK = 128
SBQ = 512
SBK = 256
DIL_UNROLL = 8
ROPE_HALF = 16
ROPE_THETA = 500000.0
RMS_EPS = 1e-5
DILATIONS = (1, 4, 16)
NEG = -1e30
VMEM_LIMIT = 56 * 1024 * 1024
RED_TILE = 1024

ADAM_LR = 0.001
ADAM_B1 = 0.9
ADAM_B2 = 0.999
ADAM_EPS = 1e-08
ADAM_WD = 0.01
ADAM_STEP = 10

MESH = pl.DeviceIdType.MESH
HBM_SPEC = pl.BlockSpec(memory_space=pltpu.HBM)
SEM_SPEC = pl.BlockSpec(memory_space=pltpu.SEMAPHORE)
SIDE_EFFECT = pltpu.SideEffectType.DATAFLOW_SIDE_EFFECTING


def _cp(n_axes):
    return pltpu.CompilerParams(dimension_semantics=("arbitrary",) * n_axes, vmem_limit_bytes=VMEM_LIMIT)


def _tile(n, want):
    if n <= want:
        return n
    t = want
    while t >= 16:
        if n % t == 0 and t % 16 == 0:
            return t
        t -= 16
    return n


NN = (((1,), (0,)), ((), ()))
NT = (((1,), (1,)), ((), ()))
TN = (((0,), (0,)), ((), ()))


def _matmul(name, grid, red_axis, ins, in_specs, terms, dims, acc_shapes, out_shapes, out_specs, epilogue, after=None):
    if after is not None:
        ins, in_specs = (*ins, after), [*in_specs, pl.BlockSpec(memory_space=pl.ANY)]
    n_in, n_out = len(ins), len(out_shapes)
    n_red = grid[red_axis]

    def body(*refs):
        in_refs, out_refs, acc_refs = refs[:n_in], refs[n_in:n_in + n_out], refs[n_in + n_out:]
        k = pl.program_id(red_axis)

        @pl.when(k == 0)
        def _():
            for acc in acc_refs:
                acc[...] = jnp.zeros_like(acc)

        for a_idx, b_idx, acc_idx in terms:
            a = in_refs[a_idx][...].astype(BF16)
            b = in_refs[b_idx][...].astype(BF16)
            acc_refs[acc_idx][...] += lax.dot_general(a, b, dims, preferred_element_type=F32)

        @pl.when(k == n_red - 1)
        def _():
            epilogue(acc_refs, in_refs, out_refs)

    return pl.pallas_call(
        body, name=name, grid=grid, in_specs=in_specs, out_specs=out_specs, out_shape=out_shapes,
        scratch_shapes=[pltpu.VMEM(s, F32) for s in acc_shapes], compiler_params=_cp(len(grid)),
    )(*ins)


def _store_epilogue(acc_refs, in_refs, out_refs):
    for acc, out in zip(acc_refs, out_refs):
        out[...] = acc[...].astype(out.dtype)


def _rope_fwd(a, c, s1, s2):
    return a * c + pltpu.roll(a, ROPE_HALF, 1) * s1 + pltpu.roll(a, HEAD_DIM - ROPE_HALF, 1) * s2


def _rope_bwd(d, c, s1, s2):
    return d * c + pltpu.roll(d * s1, HEAD_DIM - ROPE_HALF, 1) + pltpu.roll(d * s2, ROPE_HALF, 1)


def _proj_fwd(hn, w_all, rc, rs1, rs2, n_rope_heads):
    n, d = hn.shape
    _, _, ws = w_all.shape
    tm, tk = _tile(n, 1024), _tile(d, RED_TILE)
    heads_per_shard = ws // HEAD_DIM
    rows = _tile(tm, 256)

    def epilogue(acc_refs, in_refs, out_refs):
        acc, out = acc_refs[0], out_refs[0]
        j = pl.program_id(0)
        for r0 in range(0, tm, rows):
            c, s1, s2 = (ref[pl.ds(r0, rows), :] for ref in in_refs[2:5])
            for hh in range(heads_per_shard):
                a = acc[pl.ds(r0, rows), pl.ds(hh * HEAD_DIM, HEAD_DIM)]
                roped = _rope_fwd(a, c, s1, s2)
                a = jnp.where(j * heads_per_shard + hh < n_rope_heads, roped, a)
                out[pl.ds(r0, rows), pl.ds(hh * HEAD_DIM, HEAD_DIM)] = a.astype(out.dtype)

    tab = pl.BlockSpec((tm, HEAD_DIM), lambda j, m, k: (m, 0))
    return _matmul(
        "proj_fwd", (N_DEV, n // tm, d // tk), 2, (hn, w_all, rc, rs1, rs2),
        [pl.BlockSpec((tm, tk), lambda j, m, k: (m, k)), pl.BlockSpec((None, tk, ws), lambda j, m, k: (j, k, 0)), tab, tab, tab],
        [(0, 1, 0)], NN, [(tm, ws)], [S((n, N_DEV * ws), BF16)], [pl.BlockSpec((tm, ws), lambda j, m, k: (m, j))], epilogue)[0]


def _dense_res(name, a, b, res, dims, out_dtype=F32):
    m, kdim = a.shape
    n = b.shape[1] if dims == NN else b.shape[0]
    tm, tn, tk = _tile(m, 1024), _tile(n, 1024), _tile(kdim, RED_TILE)
    ins = [a, b] + ([res] if res is not None else [])
    b_spec = pl.BlockSpec((tk, tn), lambda i, j, k: (k, j)) if dims == NN else pl.BlockSpec((tn, tk), lambda i, j, k: (j, k))
    specs = [pl.BlockSpec((tm, tk), lambda i, j, k: (i, k)), b_spec]
    if res is not None:
        specs.append(pl.BlockSpec((tm, tn), lambda i, j, k: (i, j)))

    def epilogue(acc_refs, in_refs, out_refs):
        v = acc_refs[0][...]
        if res is not None:
            v = v + in_refs[2][...]
        out_refs[0][...] = v.astype(out_dtype)

    return _matmul(name, (m // tm, n // tn, kdim // tk), 2, ins, specs, [(0, 1, 0)], dims, [(tm, tn)],
                   [S((m, n), out_dtype)], [pl.BlockSpec((tm, tn), lambda i, j, k: (i, j))], epilogue)[0]


def _tn_full(name, a, b, out_dtype=BF16):
    m, kdim = a.shape
    n = b.shape[1]
    tm, tk, tn = _tile(m, RED_TILE), _tile(kdim, 1024), _tile(n, 1024)
    return _matmul(name, (kdim // tk, n // tn, m // tm), 2, (a, b),
                   [pl.BlockSpec((tm, tk), lambda i, j, t: (t, i)), pl.BlockSpec((tm, tn), lambda i, j, t: (t, j))],
                   [(0, 1, 0)], TN, [(tk, tn)], [S((kdim, n), out_dtype)], [pl.BlockSpec((tk, tn), lambda i, j, t: (i, j))],
                   _store_epilogue)[0]


def _gateup_fwd(hn, wg_all, wu_all, after=None):
    n, d = hn.shape
    _, _, fs = wg_all.shape
    tm, tk = _tile(n, 1024), _tile(d, RED_TILE)
    rows = _tile(tm, 256)

    def epilogue(acc_refs, in_refs, out_refs):
        for r0 in range(0, tm, rows):
            g = acc_refs[0][pl.ds(r0, rows), :]
            u = acc_refs[1][pl.ds(r0, rows), :]
            out_refs[0][pl.ds(r0, rows), :] = g.astype(BF16)
            out_refs[1][pl.ds(r0, rows), :] = u.astype(BF16)
            out_refs[2][pl.ds(r0, rows), :] = (g * jax.nn.sigmoid(g) * u).astype(BF16)

    w_spec = pl.BlockSpec((None, tk, fs), lambda j, m, k: (j, k, 0))
    o_spec = pl.BlockSpec((None, tm, fs), lambda j, m, k: (j, m, 0))
    o_shape = S((N_DEV, n, fs), BF16)
    return _matmul("gateup_fwd", (N_DEV, n // tm, d // tk), 2, (hn, wg_all, wu_all),
                   [pl.BlockSpec((tm, tk), lambda j, m, k: (m, k)), w_spec, w_spec],
                   [(0, 1, 0), (0, 2, 1)], NN, [(tm, fs), (tm, fs)], [o_shape] * 3, [o_spec] * 3, epilogue, after)


def _down_fwd(act, wd_all, res):
    _, n, fs = act.shape
    d = wd_all.shape[2]
    tm, tn = _tile(n, 1024), _tile(d, 1024)

    def epilogue(acc_refs, in_refs, out_refs):
        out_refs[0][...] = acc_refs[0][...] + in_refs[2][...]

    return _matmul("down_fwd", (n // tm, d // tn, N_DEV), 2, (act, wd_all, res),
                   [pl.BlockSpec((None, tm, fs), lambda i, j, s: (s, i, 0)), pl.BlockSpec((None, fs, tn), lambda i, j, s: (s, 0, j)),
                    pl.BlockSpec((tm, tn), lambda i, j, s: (i, j))],
                   [(0, 1, 0)], NN, [(tm, tn)], [S((n, d), F32)], [pl.BlockSpec((tm, tn), lambda i, j, s: (i, j))], epilogue)[0]


def _dact_bwd(dh, wd_all, gate, up):
    n, d = dh.shape
    _, fs, _ = wd_all.shape
    tm, tk = _tile(n, 1024), _tile(d, RED_TILE)
    rows = _tile(tm, 256)

    def epilogue(acc_refs, in_refs, out_refs):
        for r0 in range(0, tm, rows):
            da = acc_refs[0][pl.ds(r0, rows), :]
            g = in_refs[2][pl.ds(r0, rows), :].astype(F32)
            u = in_refs[3][pl.ds(r0, rows), :].astype(F32)
            sg = jax.nn.sigmoid(g)
            out_refs[0][pl.ds(r0, rows), :] = (da * u * (sg * (1.0 + g * (1.0 - sg)))).astype(BF16)
            out_refs[1][pl.ds(r0, rows), :] = (da * (g * sg)).astype(BF16)

    t_spec = pl.BlockSpec((None, tm, fs), lambda j, m, k: (j, m, 0))
    o_shape = S((N_DEV, n, fs), BF16)
    return _matmul("dact_bwd", (N_DEV, n // tm, d // tk), 2, (dh, wd_all, gate, up),
                   [pl.BlockSpec((tm, tk), lambda j, m, k: (m, k)), pl.BlockSpec((None, fs, tk), lambda j, m, k: (j, 0, k)), t_spec, t_spec],
                   [(0, 1, 0)], NT, [(tm, fs)], [o_shape] * 2, [t_spec] * 2, epilogue)


def _dwd_bwd(act, dh):
    _, n, fs = act.shape
    d = dh.shape[1]
    tm, tn = _tile(n, RED_TILE), _tile(d, 1024)
    return _matmul("dwd_bwd", (N_DEV, d // tn, n // tm), 2, (act, dh),
                   [pl.BlockSpec((None, tm, fs), lambda j, c, t: (j, t, 0)), pl.BlockSpec((tm, tn), lambda j, c, t: (t, c))],
                   [(0, 1, 0)], TN, [(fs, tn)], [S((N_DEV, fs, d), BF16)], [pl.BlockSpec((None, fs, tn), lambda j, c, t: (j, 0, c))],
                   _store_epilogue)[0]


def _dw_cols_bwd(name, hn, dys, after=None):
    n, d = hn.shape
    ws = dys[0].shape[2]
    tm, tk = _tile(n, RED_TILE), _tile(d, 1024)
    k_out = len(dys)
    y_spec = pl.BlockSpec((None, tm, ws), lambda j, c, t: (j, t, 0))
    o_spec = pl.BlockSpec((None, tk, ws), lambda j, c, t: (j, c, 0))
    return _matmul(name, (N_DEV, d // tk, n // tm), 2, (hn, *dys),
                   [pl.BlockSpec((tm, tk), lambda j, c, t: (t, c))] + [y_spec] * k_out,
                   [(0, 1 + i, i) for i in range(k_out)], TN, [(tk, ws)] * k_out, [S((N_DEV, d, ws), BF16)] * k_out, [o_spec] * k_out,
                   _store_epilogue, after)


def _dwin_bwd(hn, dproj):
    n, d = hn.shape
    ws = dproj.shape[1] // N_DEV
    tm, tk = _tile(n, RED_TILE), _tile(d, 1024)
    return _matmul("dwin_bwd", (N_DEV, d // tk, n // tm), 2, (hn, dproj),
                   [pl.BlockSpec((tm, tk), lambda j, c, t: (t, c)), pl.BlockSpec((tm, ws), lambda j, c, t: (t, j))],
                   [(0, 1, 0)], TN, [(tk, ws)], [S((N_DEV, d, ws), BF16)], [pl.BlockSpec((None, tk, ws), lambda j, c, t: (j, c, 0))],
                   _store_epilogue)[0]


def _dhn_from_shards(name, dys, ws_all, dy_is_flat, after=None):
    if dy_is_flat:
        n, ws = dys[0].shape[0], dys[0].shape[1] // N_DEV
    else:
        _, n, ws = dys[0].shape
    d = ws_all[0].shape[1]
    tm, tn = _tile(n, 1024), _tile(d, 1024)
    if dy_is_flat:
        y_spec = pl.BlockSpec((tm, ws), lambda i, j, s: (i, s))
    else:
        y_spec = pl.BlockSpec((None, tm, ws), lambda i, j, s: (s, i, 0))
    w_spec = pl.BlockSpec((None, tn, ws), lambda i, j, s: (s, j, 0))
    k_terms = len(dys)
    return _matmul(name, (n // tm, d // tn, N_DEV), 2, (*dys, *ws_all), [y_spec] * k_terms + [w_spec] * k_terms,
                   [(i, k_terms + i, 0) for i in range(k_terms)], NT, [(tm, tn)], [S((n, d), F32)],
                   [pl.BlockSpec((tm, tn), lambda i, j, s: (i, j))], _store_epilogue, after)[0]


def _rms_fwd(name, x, g):
    n, d = x.shape
    tm = _tile(n, 256)

    def body(x_ref, g_ref, o_ref):
        xv = x_ref[...]
        r = lax.rsqrt(jnp.mean(xv * xv, axis=-1, keepdims=True) + RMS_EPS)
        o_ref[...] = (xv * r * g_ref[...]).astype(BF16)

    return pl.pallas_call(body, name=name, grid=(n // tm,),
                          in_specs=[pl.BlockSpec((tm, d), lambda i: (i, 0)), pl.BlockSpec((1, d), lambda i: (0, 0))],
                          out_specs=pl.BlockSpec((tm, d), lambda i: (i, 0)), out_shape=S((n, d), BF16), compiler_params=_cp(1))(x, g)


def _rms_bwd(name, x, g, dy, res):
    n, d = x.shape
    tm = _tile(n, 256)

    def body(x_ref, g_ref, dy_ref, res_ref, dx_ref, dg_ref):
        xv, dyv = x_ref[...], dy_ref[...]
        r = lax.rsqrt(jnp.mean(xv * xv, axis=-1, keepdims=True) + RMS_EPS)
        xr = xv * r
        dgy = dyv * g_ref[...]
        dx_ref[...] = res_ref[...] + r * (dgy - xr * jnp.mean(dgy * xr, axis=-1, keepdims=True))

        @pl.when(pl.program_id(0) == 0)
        def _():
            dg_ref[...] = jnp.zeros_like(dg_ref)

        dg_ref[...] += jnp.sum(dyv * xr, axis=0, keepdims=True)

    row = pl.BlockSpec((tm, d), lambda i: (i, 0))
    vec = pl.BlockSpec((1, d), lambda i: (0, 0))
    return pl.pallas_call(body, name=name, grid=(n // tm,), in_specs=[row, vec, row, row], out_specs=[row, vec],
                          out_shape=[S((n, d), F32), S((1, d), F32)], compiler_params=_cp(1))(x, g, dy, res)


def _final_loss(h, g, target):
    n, d = h.shape
    tm = _tile(n, 256)

    def body(h_ref, g_ref, t_ref, dh_ref, dhb_ref, dg_ref, sse_ref):
        hv, gv = h_ref[...], g_ref[...]
        r = lax.rsqrt(jnp.mean(hv * hv, axis=-1, keepdims=True) + RMS_EPS)
        hr = hv * r
        err = hr * gv - t_ref[...]
        dy = err * (1.0 / d)
        dgy = dy * gv
        dh = r * (dgy - hr * jnp.mean(dgy * hr, axis=-1, keepdims=True))
        dh_ref[...] = dh
        dhb_ref[...] = dh.astype(BF16)

        @pl.when(pl.program_id(0) == 0)
        def _():
            dg_ref[...] = jnp.zeros_like(dg_ref)
            sse_ref[...] = jnp.zeros_like(sse_ref)

        dg_ref[...] += jnp.sum(dy * hr, axis=0, keepdims=True)
        sse_ref[...] += jnp.sum(err * err)

    row = pl.BlockSpec((tm, d), lambda i: (i, 0))
    vec = pl.BlockSpec((1, d), lambda i: (0, 0))
    one = pl.BlockSpec((8, HEAD_DIM), lambda i: (0, 0))
    return pl.pallas_call(body, name="final_loss", grid=(n // tm,), in_specs=[row, vec, row], out_specs=[row, row, vec, one],
                          out_shape=[S((n, d), F32), S((n, d), BF16), S((1, d), F32), S((8, HEAD_DIM), F32)],
                          compiler_params=_cp(1))(h, g, target)


def _dot(a, b, dims):
    return lax.dot_general(a, b, dims, preferred_element_type=F32)


def _split3(x):
    hi = x.astype(BF16)
    r1 = x - hi.astype(F32)
    mid = r1.astype(BF16)
    lo = (r1 - mid.astype(F32)).astype(BF16)
    return hi, mid, lo


def _scan_cols(x, tri, terms):
    parts = _split3(x)[:terms]
    out = _dot(parts[0], tri, NN)
    for p in parts[1:]:
        out = out + _dot(p, tri, NN)
    return out


def _head_norm_fwd(o, g):
    r = lax.rsqrt(jnp.mean(o * o, axis=-1, keepdims=True) + RMS_EPS)
    return o * r * g


def _head_norm_bwd(o, g, dy):
    r = lax.rsqrt(jnp.mean(o * o, axis=-1, keepdims=True) + RMS_EPS)
    orr = o * r
    dgy = dy * g
    return r * (dgy - orr * jnp.mean(dgy * orr, axis=-1, keepdims=True)), dy * orr


def _interleave_plan(t):
    return [(i, dil, t // dil) for i, dil in enumerate(DILATIONS)]


def _band_mask(u, blocks_per_seq):
    row = lax.broadcasted_iota(jnp.int32, (BLK, 2 * BLK), 0)
    col = lax.broadcasted_iota(jnp.int32, (BLK, 2 * BLK), 1)
    dist = row + BLK - col
    has_prev = (u % blocks_per_seq) != 0
    return (dist >= 0) & (dist <= BLK) & ((col >= BLK) | has_prev)


def _dil_fwd(proj, g_dil, bl, t, n_heads):
    n = bl * t
    nb = t // BLK
    scale = HEAD_DIM ** -0.5
    plan = _interleave_plan(t)
    chunk = _tile(t, 256)

    def body(q_ref, k_ref, v_ref, g_ref, omix_ref, opre_ref, lse_ref, stg, qd, kd, vd, ob, lb, on, ln):
        for src, dst, pad in ((q_ref, qd, 0), (k_ref, kd, BLK), (v_ref, vd, BLK)):
            stg[...] = src[...].astype(F32)
            for bi, dil, sub in plan:
                if pad:
                    dst[bi, pl.ds(0, BLK), :] = jnp.zeros((BLK, HEAD_DIM), BF16)
                if dil == 1:
                    dst[bi, pl.ds(pad, t), :] = src[...]
                else:
                    for r in range(dil):
                        dst[bi, pl.ds(pad + r * sub, sub), :] = stg[pl.ds(r, sub, stride=dil), :].astype(BF16)

        for bi, dil, sub in plan:
            def blk(u, carry, bi=bi, sub=sub):
                rows = pl.ds(pl.multiple_of(u * BLK, BLK), BLK)
                win = pl.ds(pl.multiple_of(u * BLK, BLK), 2 * BLK)
                sc = _dot(qd[bi, rows, :], kd[bi, win, :], NT) * scale
                sc = jnp.where(_band_mask(u, sub // BLK), sc, NEG)
                m = jnp.max(sc, axis=-1, keepdims=True)
                p = jnp.exp(sc - m)
                den = jnp.sum(p, axis=-1, keepdims=True)
                ob[bi, rows, :] = _dot((p / den).astype(BF16), vd[bi, win, :], NN)
                lb[bi, rows, :] = jnp.broadcast_to(m + jnp.log(den), (BLK, HEAD_DIM))
                return carry
            lax.fori_loop(0, nb // DIL_UNROLL, lambda i, c, blk=blk: [blk(i * DIL_UNROLL + s, c) for s in range(DIL_UNROLL)][-1], 0)

        for bi, dil, sub in plan[1:]:
            for r in range(dil):
                on[bi - 1, pl.ds(r, sub, stride=dil), :] = ob[bi, pl.ds(r * sub, sub), :]
                ln[bi - 1, pl.ds(r, sub, stride=dil), :] = lb[bi, pl.ds(r * sub, sub), :]

        def merge(i, carry):
            rows = pl.ds(pl.multiple_of(i * chunk, chunk), chunk)
            l0, l1, l2 = lb[0, rows, :], ln[0, rows, :], ln[1, rows, :]
            mx = jnp.maximum(jnp.maximum(l0, l1), l2)
            w0, w1, w2 = jnp.exp(l0 - mx), jnp.exp(l1 - mx), jnp.exp(l2 - mx)
            tot = w0 + w1 + w2
            o = (w0 / tot) * ob[0, rows, :] + (w1 / tot) * on[0, rows, :] + (w2 / tot) * on[1, rows, :]
            lse_ref[rows, :] = mx + jnp.log(tot)
            opre_ref[rows, :] = o
            omix_ref[rows, :] = _head_norm_fwd(o, g_ref[...]).astype(BF16)
            return carry
        lax.fori_loop(0, t // chunk, merge, 0)

    hs = n_heads
    col = lambda off: pl.BlockSpec((t, HEAD_DIM), lambda b, h: (b, off + h))
    return pl.pallas_call(
        body, name="dil_fwd", grid=(bl, hs),
        in_specs=[col(0), col(hs), col(2 * hs), pl.BlockSpec((1, HEAD_DIM), lambda b, h: (0, h))],
        out_specs=[col(0), col(0), pl.BlockSpec((t, HEAD_DIM), lambda b, h: (b * hs + h, 0))],
        out_shape=[S((n, 2 * hs * HEAD_DIM), BF16), S((n, 2 * hs * HEAD_DIM), F32), S((bl * hs * t, HEAD_DIM), F32)],
        scratch_shapes=[pltpu.VMEM((t, HEAD_DIM), F32), pltpu.VMEM((3, t, HEAD_DIM), BF16),
                        pltpu.VMEM((3, t + BLK, HEAD_DIM), BF16), pltpu.VMEM((3, t + BLK, HEAD_DIM), BF16),
                        pltpu.VMEM((3, t, HEAD_DIM), F32), pltpu.VMEM((3, t, HEAD_DIM), F32),
                        pltpu.VMEM((2, t, HEAD_DIM), F32), pltpu.VMEM((2, t, HEAD_DIM), F32)],
        compiler_params=_cp(2),
    )(proj, proj, proj, g_dil)


def _dil_bwd(proj, opre, lse, d_omix, g_dil, rc, rs1, rs2, bl, t, n_heads):
    n = bl * t
    nb = t // BLK
    scale = HEAD_DIM ** -0.5
    plan = _interleave_plan(t)
    chunk = _tile(t, 256)

    def body(q_ref, k_ref, v_ref, opre_ref, lse_ref, dy_ref, g_ref, c_ref, s1_ref, s2_ref, out_ref, dg_ref,
             stg, qd, kd, vd, dod, ldd, dqd, dkc, dkp, dvc, dvp, sk, sv):
        which = pl.program_id(2)

        @pl.when(jnp.logical_and(which == 0, pl.program_id(1) == 0))
        def _():
            dg_ref[...] = jnp.zeros_like(dg_ref)

        @pl.when(which == 0)
        def _():
            def prep(i, dg):
                rows = pl.ds(pl.multiple_of(i * chunk, chunk), chunk)
                o = opre_ref[rows, :]
                d_o, dg_rows = _head_norm_bwd(o, g_ref[...], dy_ref[rows, :])
                stg[rows, :] = d_o
                lane = lax.broadcasted_iota(jnp.int32, (chunk, HEAD_DIM), 1)
                ldd[0, rows, :] = jnp.where(lane < HEAD_DIM // 2, lse_ref[rows, :], jnp.sum(d_o * o, axis=-1, keepdims=True))
                return dg + jnp.sum(dg_rows, axis=0, keepdims=True)
            dg_ref[...] += lax.fori_loop(0, t // chunk, prep, jnp.zeros((1, HEAD_DIM), F32))

            dod[0] = stg[...].astype(BF16)
            for bi, dil, sub in plan[1:]:
                for r in range(dil):
                    dst = pl.ds(r * sub, sub)
                    dod[bi, dst, :] = stg[pl.ds(r, sub, stride=dil), :].astype(BF16)
                    ldd[bi, dst, :] = ldd[0, pl.ds(r, sub, stride=dil), :]
            for src, dst, pad in ((q_ref, qd, 0), (k_ref, kd, BLK), (v_ref, vd, BLK)):
                stg[...] = src[...].astype(F32)
                for bi, dil, sub in plan:
                    if pad:
                        dst[bi, pl.ds(0, BLK), :] = jnp.zeros((BLK, HEAD_DIM), BF16)
                    if dil == 1:
                        dst[bi, pl.ds(pad, t), :] = src[...]
                    else:
                        for r in range(dil):
                            dst[bi, pl.ds(pad + r * sub, sub), :] = stg[pl.ds(r, sub, stride=dil), :].astype(BF16)

            for bi, dil, sub in plan:
                def blk(u, carry, bi=bi, sub=sub):
                    rows = pl.ds(pl.multiple_of(u * BLK, BLK), BLK)
                    win = pl.ds(pl.multiple_of(u * BLK, BLK), 2 * BLK)
                    qb, kw, vw, dob = qd[bi, rows, :], kd[bi, win, :], vd[bi, win, :], dod[bi, rows, :]
                    sc = _dot(qb, kw, NT) * scale
                    stats = ldd[bi, rows, :]
                    p = jnp.where(_band_mask(u, sub // BLK), jnp.exp(sc - stats[:, :1]), 0.0)
                    dp = _dot(dob, vw, NT)
                    ds = (p * (dp - stats[:, HEAD_DIM // 2:HEAD_DIM // 2 + 1]) * scale).astype(BF16)
                    dqd[bi, rows, :] = _dot(ds, kw, NN)
                    dk_win = _dot(ds, qb, TN)
                    dv_win = _dot(p.astype(BF16), dob, TN)
                    dkp[bi, rows, :] = dk_win[:BLK]
                    dkc[bi, rows, :] = dk_win[BLK:]
                    dvp[bi, rows, :] = dv_win[:BLK]
                    dvc[bi, rows, :] = dv_win[BLK:]
                    return carry
                lax.fori_loop(0, nb // DIL_UNROLL, lambda i, c, blk=blk: [blk(i * DIL_UNROLL + s, c) for s in range(DIL_UNROLL)][-1], 0)

            for cur, prev, undo_rope, dst in ((dqd, None, True, out_ref), (dkc, dkp, True, sk), (dvc, dvp, False, sv)):
                def summed(bi, start, size, cur=cur, prev=prev):
                    v = cur[bi, pl.ds(start, size), :]
                    if prev is None:
                        return v
                    if start + size < t:
                        return v + prev[bi, pl.ds(start + BLK, size), :]
                    if size == BLK:
                        return v
                    return v + jnp.concatenate([prev[bi, pl.ds(start + BLK, size - BLK), :], jnp.zeros((BLK, HEAD_DIM), F32)], axis=0)
                stg[...] = summed(0, 0, t)
                for bi, dil, sub in plan[1:]:
                    for r in range(dil):
                        stg[pl.ds(r, sub, stride=dil), :] += summed(bi, r * sub, sub)
                if undo_rope:
                    dst[...] = _rope_bwd(stg[...], c_ref[...], s1_ref[...], s2_ref[...]).astype(BF16)
                else:
                    dst[...] = stg[...].astype(BF16)

        @pl.when(which == 1)
        def _():
            out_ref[...] = sk[...]

        @pl.when(which == 2)
        def _():
            out_ref[...] = sv[...]

    hs = n_heads
    col = lambda off: pl.BlockSpec((t, HEAD_DIM), lambda h, b, w: (b, off + h))
    per_head = pl.BlockSpec((t, HEAD_DIM), lambda h, b, w: (b * hs + h, 0))
    tab = pl.BlockSpec((t, HEAD_DIM), lambda h, b, w: (b, 0))
    gvec = pl.BlockSpec((1, HEAD_DIM), lambda h, b, w: (0, h))
    tb = (t, HEAD_DIM)
    tp = (t + BLK, HEAD_DIM)
    return pl.pallas_call(
        body, name="dil_bwd", grid=(hs, bl, 3),
        in_specs=[col(0), col(hs), col(2 * hs), col(0), per_head, col(0), gvec, tab, tab, tab],
        out_specs=[pl.BlockSpec((t, HEAD_DIM), lambda h, b, w: (b, w * hs + h)), gvec],
        out_shape=[S((n, 6 * hs * HEAD_DIM), BF16), S((1, hs * HEAD_DIM), F32)],
        scratch_shapes=[pltpu.VMEM(tb, F32), pltpu.VMEM((3,) + tb, BF16), pltpu.VMEM((3,) + tp, BF16), pltpu.VMEM((3,) + tp, BF16),
                        pltpu.VMEM((3,) + tb, BF16), pltpu.VMEM((3,) + tb, F32), pltpu.VMEM((3,) + tb, F32),
                        pltpu.VMEM((3,) + tb, F32), pltpu.VMEM((3,) + tb, F32), pltpu.VMEM((3,) + tb, F32), pltpu.VMEM((3,) + tb, F32),
                        pltpu.VMEM(tb, BF16), pltpu.VMEM(tb, BF16)],
        compiler_params=_cp(3),
    )(proj, proj, proj, opre, lse, d_omix, g_dil, rc, rs1, rs2)


def _sb_tile(qb, kb, scale, shift):
    z = _dot(qb, kb, NT) * scale
    tl = jnp.log(1.0 + jnp.exp(-jnp.abs(z)))
    log_not = -(jnp.maximum(z, 0.0) + tl)
    strict = None
    if shift is not None:
        row = lax.broadcasted_iota(jnp.int32, (SBQ, SBK), 0)
        col = lax.broadcasted_iota(jnp.int32, (SBQ, SBK), 1)
        strict = col + shift < row
        log_not = jnp.where(strict, log_not, 0.0)
    return log_not, jnp.minimum(z, 0.0) - tl, strict


def _tri(cmp):
    row = lax.broadcasted_iota(jnp.int32, (SBK, SBK), 0)
    col = lax.broadcasted_iota(jnp.int32, (SBK, SBK), 1)
    return jnp.where(cmp(row, col), 1.0, 0.0).astype(BF16)


SB_EDGE = tuple(range(0, SBQ, SBK))


def _sb_fwd(proj, g_sb, omix_in, opre_in, bl, t, n_heads):
    nb = t // SBQ
    per_q = SBQ // SBK
    scale = HEAD_DIM ** -0.5

    def body(q_ref, k_ref, v_ref, g_ref, _omix_in, _opre_in, omix_ref, opre_ref, lt_ref):
        later = _tri(lambda r, c: r > c)

        def q_block(qi, carry):
            rows = pl.ds(pl.multiple_of(qi * SBQ, SBQ), SBQ)
            qb = q_ref[rows, :]

            def tile(kj, st, shift):
                run, acc = st
                krows = pl.ds(pl.multiple_of(kj * SBK, SBK), SBK)
                log_not, log_beta, strict = _sb_tile(qb, k_ref[krows, :], scale, shift)
                a = jnp.exp(log_beta + _scan_cols(log_not, later, 2) + run)
                if shift is not None:
                    a = jnp.where(strict, a, 0.0)
                return run + jnp.sum(log_not, axis=-1, keepdims=True), acc + _dot(a.astype(BF16), v_ref[krows, :], NN)

            st = (jnp.zeros((SBQ, 1), F32), jnp.zeros((SBQ, HEAD_DIM), F32))
            for shift in reversed(SB_EDGE):
                st = tile(qi * per_q + shift // SBK, st, shift)
            run, acc = lax.fori_loop(0, qi * per_q, lambda it, st: tile(qi * per_q - 1 - it, st, None), st)
            lt_ref[rows, :] = jnp.broadcast_to(run, (SBQ, HEAD_DIM))
            opre_ref[rows, :] = acc
            omix_ref[rows, :] = _head_norm_fwd(acc, g_ref[...]).astype(BF16)
            return carry

        lax.fori_loop(0, nb, q_block, 0)

    hs = n_heads
    col = lambda off: pl.BlockSpec((t, HEAD_DIM), lambda b, h: (b, off + h))
    return pl.pallas_call(
        body, name="sb_fwd", grid=(bl, hs),
        in_specs=[col(3 * hs), col(4 * hs), col(5 * hs), pl.BlockSpec((1, HEAD_DIM), lambda b, h: (0, h)), HBM_SPEC, HBM_SPEC],
        out_specs=[col(hs), col(hs), pl.BlockSpec((t, HEAD_DIM), lambda b, h: (b * hs + h, 0))],
        out_shape=[S(omix_in.shape, BF16), S(opre_in.shape, F32), S((bl * hs * t, HEAD_DIM), F32)],
        input_output_aliases={4: 0, 5: 1}, compiler_params=_cp(2),
    )(proj, proj, proj, g_sb, omix_in, opre_in)


def _sb_bwd(proj, opre, ltot, d_omix, g_sb, dproj_in, bl, t, n_heads):
    nb = t // SBQ
    per_q = SBQ // SBK
    scale = HEAD_DIM ** -0.5

    def body(q_ref, k_ref, v_ref, opre_ref, lt_ref, dy_ref, g_ref, _dproj_in, out_ref, dg_ref, dq, dk, dv):
        which = pl.program_id(2)

        @pl.when(jnp.logical_and(which == 0, pl.program_id(1) == 0))
        def _():
            dg_ref[...] = jnp.zeros_like(dg_ref)

        @pl.when(which == 0)
        def _():
            upto = _tri(lambda r, c: r <= c)
            before = _tri(lambda r, c: r < c)
            dk[...] = jnp.zeros_like(dk)
            dv[...] = jnp.zeros_like(dv)

            def q_block(qi, dg):
                rows = pl.ds(pl.multiple_of(qi * SBQ, SBQ), SBQ)
                qb = q_ref[rows, :]
                o = opre_ref[rows, :]
                d_o, dg_rows = _head_norm_bwd(o, g_ref[...], dy_ref[rows, :])
                dob = d_o.astype(BF16)
                lt = lt_ref[rows, :][:, :1]

                def tile(kj, st, shift):
                    run, grun, dq_acc = st
                    krows = pl.ds(pl.multiple_of(kj * SBK, SBK), SBK)
                    kb, vb = k_ref[krows, :], v_ref[krows, :]
                    log_not, log_beta, strict = _sb_tile(qb, kb, scale, shift)
                    excl = lt - (run + _scan_cols(log_not, upto, 2))
                    a = jnp.exp(log_beta + excl)
                    if shift is not None:
                        a = jnp.where(strict, a, 0.0)
                    g_a = a * _dot(dob, vb, NT)
                    g_before = grun + _scan_cols(g_a, before, 2)
                    dz = (g_a - (g_a + g_before) * jnp.exp(log_beta)) * scale
                    if shift is not None:
                        dz = jnp.where(strict, dz, 0.0)
                    dzb = dz.astype(BF16)
                    dk[krows, :] += _dot(dzb, qb, TN)
                    dv[krows, :] += _dot(a.astype(BF16), dob, TN)
                    return (run + jnp.sum(log_not, axis=-1, keepdims=True), grun + jnp.sum(g_a, axis=-1, keepdims=True),
                            dq_acc + _dot(dzb, kb, NN))

                zero_col = jnp.zeros((SBQ, 1), F32)
                st = lax.fori_loop(0, qi * per_q, lambda kj, st: tile(kj, st, None),
                                   (zero_col, zero_col, jnp.zeros((SBQ, HEAD_DIM), F32)))
                for shift in SB_EDGE:
                    st = tile(qi * per_q + shift // SBK, st, shift)
                dq_acc = st[2]
                dq[rows, :] = dq_acc.astype(BF16)
                return dg + jnp.sum(dg_rows, axis=0, keepdims=True)

            dg_ref[...] += lax.fori_loop(0, nb, q_block, jnp.zeros((1, HEAD_DIM), F32))
            out_ref[...] = dq[...]

        @pl.when(which == 1)
        def _():
            out_ref[...] = dk[...].astype(BF16)

        @pl.when(which == 2)
        def _():
            out_ref[...] = dv[...].astype(BF16)

    hs = n_heads
    col = lambda off: pl.BlockSpec((t, HEAD_DIM), lambda h, b, w: (b, off + h))
    per_head = pl.BlockSpec((t, HEAD_DIM), lambda h, b, w: (b * hs + h, 0))
    gvec = pl.BlockSpec((1, HEAD_DIM), lambda h, b, w: (0, h))
    tb = (t, HEAD_DIM)
    return pl.pallas_call(
        body, name="sb_bwd", grid=(hs, bl, 3),
        in_specs=[col(3 * hs), col(4 * hs), col(5 * hs), col(hs), per_head, col(hs), gvec, HBM_SPEC],
        out_specs=[pl.BlockSpec((t, HEAD_DIM), lambda h, b, w: (b, (3 + w) * hs + h)), gvec],
        out_shape=[S(dproj_in.shape, BF16), S((1, hs * HEAD_DIM), F32)],
        scratch_shapes=[pltpu.VMEM(tb, BF16), pltpu.VMEM(tb, F32), pltpu.VMEM(tb, F32)],
        input_output_aliases={7: 0}, compiler_params=_cp(3),
    )(proj, proj, proj, opre, ltot, d_omix, g_sb, dproj_in)


def _all_gather(shards):
    k_w = len(shards)

    def body(*refs):
        ins, outs = refs[:k_w], refs[k_w:2 * k_w]
        send_sems, recv_sems, local_sems = refs[2 * k_w:]
        x, y, c = lax.axis_index("x"), lax.axis_index("y"), lax.axis_index("c")
        me, sibling = (x, y, c), (x, y, 1 - c)
        chips = [(1 - x, y), (x, 1 - y), (1 - x, 1 - y)]

        def slot(dev):
            return 4 * dev[0] + 2 * dev[1] + dev[2]

        def copy(w, k, block, to, src=None):
            dst = outs[w].at[slot(block)]
            return pltpu.make_async_remote_copy(
                src_ref=dst if src is None else src, dst_ref=dst, send_sem=send_sems.at[w * 7 + k], recv_sem=recv_sems.at[w * 7 + k],
                device_id=to, device_id_type=MESH)

        mine = [pltpu.make_async_copy(ins[w], outs[w].at[slot(me)], local_sems.at[w]) for w in range(k_w)]
        first = []
        for w in range(k_w):
            mine[w].start()
            first.append(copy(w, 0, me, sibling, src=ins[w]))
            first += [copy(w, 1 + j, me, (*chip, c), src=ins[w]) for j, chip in enumerate(chips)]
        for cp in first:
            cp.start()
        passed = []
        for w in range(k_w):
            for j, chip in enumerate(chips):
                copy(w, 1 + j, (*chip, c), me).wait_recv()
                fwd = copy(w, 4 + j, (*chip, c), sibling)
                fwd.start()
                passed.append(fwd)
        for w in range(k_w):
            copy(w, 0, sibling, me).wait_recv()
            for j, chip in enumerate(chips):
                copy(w, 4 + j, (*chip, 1 - c), me).wait_recv()
        for cp in first + passed:
            cp.wait_send()
        for cp in mine:
            cp.wait()

    return pl.pallas_call(
        body, name="weights_all_gather", in_specs=[HBM_SPEC] * k_w, out_specs=[HBM_SPEC] * k_w,
        out_shape=[S((N_DEV,) + s.shape, s.dtype) for s in shards],
        scratch_shapes=[pltpu.SemaphoreType.DMA((7 * k_w,)), pltpu.SemaphoreType.DMA((7 * k_w,)), pltpu.SemaphoreType.DMA((k_w,))],
    )(*shards)


def _grad_exchange(grads, after):
    k_w = len(grads)

    def body(*refs):
        ins, outs = refs[:k_w], refs[k_w + 1:2 * k_w + 1]
        send_sems, recv_sems, local_sems = refs[2 * k_w + 1:]
        local, remote = _exchange_copies(ins, outs, send_sems, recv_sems, local_sems, False, ALL_PEERS)
        for cp in local + remote:
            cp.start()
        for cp in remote:
            cp.wait_send()
            cp.wait_recv()
        for cp in local:
            cp.wait()

    n_sem = (N_DEV - 1) * k_w
    return pl.pallas_call(
        body, name="grad_exchange", in_specs=[HBM_SPEC] * k_w + [pl.BlockSpec(memory_space=pl.ANY)], out_specs=[HBM_SPEC] * k_w,
        out_shape=[S(g.shape, g.dtype) for g in grads],
        scratch_shapes=[pltpu.SemaphoreType.DMA((n_sem,)), pltpu.SemaphoreType.DMA((n_sem,)), pltpu.SemaphoreType.DMA((k_w,))],
    )(*grads, after)


def _peer(x, y, c, k):
    px, py, pc = x ^ (k >> 2), y ^ ((k >> 1) & 1), c ^ (k & 1)
    return (px, py, pc), 4 * px + 2 * py + pc


ALL_PEERS = tuple(range(1, N_DEV))
SAME_CORE_PEERS = (2, 4, 6)


def _exchange_copies(srcs, lands, send_sems, recv_sems, local_sems, gather, peers):
    x, y, c = lax.axis_index("x"), lax.axis_index("y"), lax.axis_index("c")
    my_slot = 4 * x + 2 * y + c
    local, remote = [], []
    for w, (src, land) in enumerate(zip(srcs, lands)):
        local.append(pltpu.make_async_copy(src if gather else src.at[my_slot], land.at[my_slot], local_sems.at[w]))
        for i, k in enumerate(peers):
            peer, peer_slot = _peer(x, y, c, k)
            remote.append(pltpu.make_async_remote_copy(
                src_ref=src if gather else src.at[peer_slot], dst_ref=land.at[my_slot],
                send_sem=send_sems.at[w * len(peers) + i], recv_sem=recv_sems.at[w * len(peers) + i],
                device_id=peer, device_id_type=MESH))
    return local, remote


def _exchange_start(name, srcs, gather, after, peers=ALL_PEERS):
    k_w = len(srcs)
    land_shapes = [((N_DEV,) + s.shape) if gather else s.shape for s in srcs]

    def body(*refs):
        src_refs, land_refs = refs[:k_w], refs[k_w:2 * k_w]
        send_sems, recv_sems, local_sems = refs[2 * k_w + 1:2 * k_w + 4]
        token = refs[-1]
        local, remote = _exchange_copies(src_refs, land_refs, send_sems, recv_sems, local_sems, gather, peers)
        for cp in local + remote:
            cp.start()
        token[...] = jnp.zeros_like(token)

    n_sem = len(peers) * k_w
    hbm = lambda a: pltpu.with_memory_space_constraint(a, pltpu.HBM)
    outs = pl.pallas_call(
        body, name=name,
        in_specs=[HBM_SPEC] * (2 * k_w) + [pl.BlockSpec(memory_space=pl.ANY)],
        out_shape=(pltpu.SemaphoreType.DMA((n_sem,)), pltpu.SemaphoreType.DMA((n_sem,)), pltpu.SemaphoreType.DMA((k_w,)),
                   *[pltpu.HBM(s.shape, s.dtype) for s in srcs], *[pltpu.HBM(ls, s.dtype) for ls, s in zip(land_shapes, srcs)],
                   S((8, HEAD_DIM), F32)),
        out_specs=(SEM_SPEC, SEM_SPEC, SEM_SPEC, *[HBM_SPEC] * (2 * k_w), pl.BlockSpec(memory_space=pltpu.VMEM)),
        input_output_aliases={i: 3 + i for i in range(2 * k_w)},
        compiler_params=pltpu.CompilerParams(has_side_effects=SIDE_EFFECT),
    )(*[hbm(s) for s in srcs], *[hbm(lax.empty(ls, s.dtype)) for ls, s in zip(land_shapes, srcs)], after)
    return dict(sems=outs[:3], srcs=outs[3:3 + k_w], lands=outs[3 + k_w:3 + 2 * k_w], token_block=outs[-1], token=outs[-1][0, 0], gather=gather, peers=peers)


def _exchange_wait(name, handle, after):
    k_w = len(handle["srcs"])
    gather = handle["gather"]

    def body(*refs):
        src_refs, land_refs = refs[:k_w], refs[k_w:2 * k_w]
        send_sems, recv_sems, local_sems = refs[2 * k_w:2 * k_w + 3]
        local, remote = _exchange_copies(src_refs, land_refs, send_sems, recv_sems, local_sems, gather, handle["peers"])
        for cp in local:
            cp.wait()
        for cp in remote:
            cp.wait_send()
            cp.wait_recv()

    outs = pl.pallas_call(
        body, name=name,
        in_specs=[HBM_SPEC] * (2 * k_w) + [SEM_SPEC] * 3 + [pl.BlockSpec(memory_space=pl.ANY)],
        out_shape=tuple(pltpu.HBM(a.shape, a.dtype) for a in (*handle["srcs"], *handle["lands"])),
        out_specs=tuple([HBM_SPEC] * (2 * k_w)),
        input_output_aliases={i: i for i in range(2 * k_w)},
        compiler_params=pltpu.CompilerParams(has_side_effects=SIDE_EFFECT),
    )(*handle["srcs"], *handle["lands"], *handle["sems"], after)
    return outs[k_w:]


CHIPS = ((0, 0), (0, 1), (1, 0), (1, 1))


def _swap_copies(lands, send_sems, recv_sems):
    x, y, c = lax.axis_index("x"), lax.axis_index("y"), lax.axis_index("c")
    copies = []
    for w, land in enumerate(lands):
        for i, (px, py) in enumerate(CHIPS):
            mine = land.at[4 * px + 2 * py + c]
            copies.append(pltpu.make_async_remote_copy(
                src_ref=mine, dst_ref=mine, send_sem=send_sems.at[w * len(CHIPS) + i], recv_sem=recv_sems.at[w * len(CHIPS) + i],
                device_id=(x, y, 1 - c), device_id_type=MESH))
    return copies


def _swap_start(name, lands, after):
    k_w = len(lands)

    def body(*refs):
        land_refs = refs[:k_w]
        send_sems, recv_sems = refs[k_w + 1:k_w + 3]
        for cp in _swap_copies(land_refs, send_sems, recv_sems):
            cp.start()
        refs[-1][...] = jnp.zeros_like(refs[-1])

    n_sem = len(CHIPS) * k_w
    outs = pl.pallas_call(
        body, name=name, in_specs=[HBM_SPEC] * k_w + [pl.BlockSpec(memory_space=pl.ANY)],
        out_shape=(pltpu.SemaphoreType.DMA((n_sem,)), pltpu.SemaphoreType.DMA((n_sem,)),
                   *[pltpu.HBM(a.shape, a.dtype) for a in lands], S((8, HEAD_DIM), F32)),
        out_specs=(SEM_SPEC, SEM_SPEC, *[HBM_SPEC] * k_w, pl.BlockSpec(memory_space=pltpu.VMEM)),
        input_output_aliases={i: 2 + i for i in range(k_w)},
        compiler_params=pltpu.CompilerParams(has_side_effects=SIDE_EFFECT),
    )(*lands, after)
    return dict(sems=outs[:2], lands=outs[2:2 + k_w], token_block=outs[-1])


def _swap_wait(name, handle, after):
    k_w = len(handle["lands"])

    def body(*refs):
        for cp in _swap_copies(refs[:k_w], refs[k_w], refs[k_w + 1]):
            cp.wait_send()
            cp.wait_recv()

    return pl.pallas_call(
        body, name=name, in_specs=[HBM_SPEC] * k_w + [SEM_SPEC] * 2 + [pl.BlockSpec(memory_space=pl.ANY)],
        out_shape=tuple(pltpu.HBM(a.shape, a.dtype) for a in handle["lands"]), out_specs=tuple([HBM_SPEC] * k_w),
        input_output_aliases={i: i for i in range(k_w)},
        compiler_params=pltpu.CompilerParams(has_side_effects=SIDE_EFFECT),
    )(*handle["lands"], *handle["sems"], after)


def _adamw(name, parts, w, m, v):
    r, c = w.shape
    tr = _tile(r, max(16, (1 << 19) // c // 16 * 16))

    def body(p_ref, w_ref, m_ref, v_ref, g_ref, d_ref, nm_ref, nv_ref):
        g = p_ref[0].astype(F32)
        for s in range(1, N_DEV):
            g = g + p_ref[s].astype(F32)
        m_new = ADAM_B1 * m_ref[...] + (1.0 - ADAM_B1) * g
        v_new = ADAM_B2 * v_ref[...] + (1.0 - ADAM_B2) * jnp.square(g)
        m_hat = m_new / (1.0 - ADAM_B1 ** ADAM_STEP)
        v_hat = v_new / (1.0 - ADAM_B2 ** ADAM_STEP)
        g_ref[...] = g
        d_ref[...] = -ADAM_LR * (m_hat / (jnp.sqrt(v_hat) + ADAM_EPS) + ADAM_WD * w_ref[...])
        nm_ref[...] = m_new
        nv_ref[...] = v_new

    blk = pl.BlockSpec((tr, c), lambda i: (i, 0))
    return pl.pallas_call(body, name=name, grid=(r // tr,), in_specs=[pl.BlockSpec((N_DEV, tr, c), lambda i: (0, i, 0)), blk, blk, blk],
                          out_specs=[blk] * 4, out_shape=[S((r, c), F32)] * 4, compiler_params=_cp(1))(parts, w, m, v)


def _rope_tables(positions):
    inv_freq = jnp.power(jnp.float32(ROPE_THETA), -jnp.arange(ROPE_HALF, dtype=F32) / ROPE_HALF)
    ang = positions.astype(F32).reshape(-1, 1) * inv_freq
    cos, sin = jnp.cos(ang), jnp.sin(ang)
    n = ang.shape[0]
    rest = HEAD_DIM - 2 * ROPE_HALF
    zeros = jnp.zeros((n, ROPE_HALF), F32)
    c = jnp.concatenate([cos, cos, jnp.ones((n, rest), F32)], axis=1)
    s1 = jnp.concatenate([zeros, sin, jnp.zeros((n, rest), F32)], axis=1)
    s2 = jnp.concatenate([-sin, zeros, jnp.zeros((n, rest), F32)], axis=1)
    return c, s1, s2


def kernel(x, positions, norm_mix_g, w_in, norm_out_dil_g, norm_out_sb_g, w_out, norm_ffn_g, w_gate, w_up, w_down, norm_final_g, loss_target, m_norm_mix_g, m_w_in, m_norm_out_dil_g, m_norm_out_sb_g, m_w_out, m_norm_ffn_g, m_w_gate, m_w_up, m_w_down, m_norm_final_g, v_norm_mix_g, v_w_in, v_norm_out_dil_g, v_norm_out_sb_g, v_w_out, v_norm_ffn_g, v_w_gate, v_w_up, v_w_down, v_norm_final_g):
    bl, t, d = x.shape
    n = bl * t
    hs = d // (2 * HEAD_DIM)
    x2 = x.reshape(n, d)
    target = loss_target.reshape(n, d)
    g_final = norm_final_g.reshape(1, d)
    rc, rs1, rs2 = _rope_tables(positions)

    (win_all,) = _all_gather([w_in[0].astype(BF16)])
    ex_wout = _exchange_start("wout_gather_start", [w_out[0].astype(BF16)], True, win_all, SAME_CORE_PEERS)
    ex_wgu = _exchange_start("wgu_gather_start", [w_gate[0].astype(BF16), w_up[0].astype(BF16)], True, ex_wout["token_block"], SAME_CORE_PEERS)
    ex_wd = _exchange_start("wd_gather_start", [w_down[0].astype(BF16)], True, ex_wgu["token_block"], SAME_CORE_PEERS)
    rc = rc + ex_wd["token"]

    hn1 = _rms_fwd("rms_mix_fwd", x2, norm_mix_g)
    proj = _proj_fwd(hn1, win_all, rc, rs1, rs2, 2 * hs)
    omix, opre, lse = _dil_fwd(proj, norm_out_dil_g, bl, t, hs)
    omix, opre, ltot = _sb_fwd(proj, norm_out_sb_g, omix, opre, bl, t, hs)
    (wout_half,) = _exchange_wait("wout_gather_wait", ex_wout, ltot)
    wg_half, wu_half = _exchange_wait("wgu_gather_wait", ex_wgu, wout_half)
    sw_wout = _swap_start("wout_swap_start", [wout_half], wg_half)
    sw_wgu = _swap_start("wgu_swap_start", [wg_half, wu_half], sw_wout["token_block"])
    (wout_all,) = _swap_wait("wout_swap_wait", sw_wout, sw_wgu["token_block"])
    wout_full = wout_all.reshape(d, d)
    h1 = _dense_res("out_fwd", omix, wout_full, x2, NN)
    hn2 = _rms_fwd("rms_ffn_fwd", h1, norm_ffn_g)
    wg_all, wu_all = _swap_wait("wgu_swap_wait", sw_wgu, hn2)
    (wd_half,) = _exchange_wait("wd_gather_wait", ex_wd, wg_all)
    sw_wd = _swap_start("wd_swap_start", [wd_half], wg_all)
    gate, up, act = _gateup_fwd(hn2, wg_all, wu_all, sw_wd["token_block"])
    (wd_all,) = _swap_wait("wd_swap_wait", sw_wd, act)
    h2 = _down_fwd(act, wd_all, h1)
    dh2, dh2b, dg_final, sse = _final_loss(h2, g_final, target)
    loss = lax.psum(sse[0, 0], ("x", "y", "c")) * (0.5 / d)

    dgate, dup = _dact_bwd(dh2b, wd_all, gate, up)
    dwd = _dwd_bwd(act, dh2b)
    ex_dwd = _exchange_start("dwd_exchange_start", [dwd], False, dgate)
    dwg, dwu = _dw_cols_bwd("dwgu_bwd", hn2, (dgate, dup), ex_dwd["token_block"])
    ex_dwgu = _exchange_start("dwgu_exchange_start", [dwg, dwu], False, ex_dwd["token_block"])
    dhn2 = _dhn_from_shards("dhn2_bwd", (dgate, dup), (wg_all, wu_all), False, ex_dwgu["token_block"])
    dh1, dg_ffn = _rms_bwd("rms_ffn_bwd", h1, norm_ffn_g, dhn2, dh2)
    d_omix = _dense_res("domix_bwd", dh1, wout_full, None, NT)
    dwout = _tn_full("dwout_bwd", omix, dh1).reshape(N_DEV, d // N_DEV, d)
    ex_dwout = _exchange_start("dwout_exchange_start", [dwout], False, d_omix)
    dproj, dg_dil = _dil_bwd(proj, opre, lse, d_omix, norm_out_dil_g + ex_dwout["token"], rc, rs1, rs2, bl, t, hs)
    dproj, dg_sb = _sb_bwd(proj, opre, ltot, d_omix, norm_out_sb_g, dproj, bl, t, hs)
    dwin = _dwin_bwd(hn1, dproj)
    ex_dwin = _exchange_start("dwin_exchange_start", [dwin], False, dproj)
    dhn1 = _dhn_from_shards("dhn1_bwd", (dproj,), (win_all,), True, ex_dwin["token_block"])
    dx, dg_mix = _rms_bwd("rms_mix_bwd", x2, norm_mix_g, dhn1, dh1)

    gains = [norm_mix_g, norm_out_dil_g, norm_out_sb_g, norm_ffn_g, g_final]
    m_gains = [m_norm_mix_g, m_norm_out_dil_g, m_norm_out_sb_g, m_norm_ffn_g, m_norm_final_g.reshape(1, d)]
    v_gains = [v_norm_mix_g, v_norm_out_dil_g, v_norm_out_sb_g, v_norm_ffn_g, v_norm_final_g.reshape(1, d)]
    dg_vec = jnp.concatenate([dg_mix, dg_dil, dg_sb, dg_ffn, dg_final], axis=1)
    dg_all = jnp.broadcast_to(dg_vec[None], (N_DEV,) + dg_vec.shape)

    out_w = {}
    (rwd,) = _exchange_wait("dwd_exchange_wait", ex_dwd, dx)
    out_w["w_down"] = _adamw("adamw_w_down", rwd, w_down[0], m_w_down[0], v_w_down[0])
    rwg, rwu = _exchange_wait("dwgu_exchange_wait", ex_dwgu, out_w["w_down"][0])
    out_w["w_gate"] = _adamw("adamw_w_gate", rwg, w_gate[0], m_w_gate[0], v_w_gate[0])
    out_w["w_up"] = _adamw("adamw_w_up", rwu, w_up[0], m_w_up[0], v_w_up[0])
    (rwout,) = _exchange_wait("dwout_exchange_wait", ex_dwout, out_w["w_up"][0])
    out_w["w_out"] = _adamw("adamw_w_out", rwout, w_out[0], m_w_out[0], v_w_out[0])
    (rwin,) = _exchange_wait("dwin_exchange_wait", ex_dwin, out_w["w_out"][0])
    out_w["w_in"] = _adamw("adamw_w_in", rwin, w_in[0], m_w_in[0], v_w_in[0])
    (rg,) = _grad_exchange([dg_all], out_w["w_in"][0])
    out_w = {name: [o[None] for o in outs] for name, outs in out_w.items()}
    cat = lambda vs: jnp.concatenate(vs, axis=1)
    gain_out = _adamw("adamw_gains", rg, cat(gains), cat(m_gains), cat(v_gains))
    widths = [d, d // 2, d // 2, d]
    cuts = [sum(widths[:i + 1]) for i in range(4)]
    gain_split = [jnp.split(o, cuts, axis=1) for o in gain_out]

    def ordered(kind):
        gs = gain_split[kind]
        return (gs[0], out_w["w_in"][kind], gs[1], gs[2], out_w["w_out"][kind], gs[3], out_w["w_gate"][kind],
                out_w["w_up"][kind], out_w["w_down"][kind], gs[4].reshape(d))

    return (loss, dx.reshape(bl, t, d), *ordered(0), *ordered(1), *ordered(2), *ordered(3))
```

```python
import functools
import math

import jax
import jax.numpy as jnp
from jax import lax
from jax.experimental import pallas as pl
from jax.experimental.pallas import tpu as pltpu

F32 = jnp.float32
BF16 = jnp.bfloat16
S = jax.ShapeDtypeStruct

N_DEV = 8
HEAD_DIM = 128
BLK = 128
SBQ = 512
SBK = 256
DIL_UNROLL = 8
ROPE_HALF = 16
ROPE_THETA = 500000.0
RMS_EPS = 1e-5
DILATIONS = (1, 4, 16)
NEG = -1e30
VMEM_LIMIT = 56 * 1024 * 1024

ADAM_LR = 0.001
ADAM_B1 = 0.9
ADAM_B2 = 0.999
ADAM_EPS = 1e-08
ADAM_WD = 0.01
ADAM_STEP = 10

MESH = pl.DeviceIdType.MESH
HBM_SPEC = pl.BlockSpec(memory_space=pltpu.HBM)
SEM_SPEC = pl.BlockSpec(memory_space=pltpu.SEMAPHORE)
SIDE_EFFECT = pltpu.SideEffectType.DATAFLOW_SIDE_EFFECTING


def _cp(n_axes):
    return pltpu.CompilerParams(dimension_semantics=("arbitrary",) * n_axes, vmem_limit_bytes=VMEM_LIMIT)


def _tile(n, want):
    if n <= want:
        return n
    t = want
    while t >= 16:
        if n % t == 0 and t % 16 == 0:
            return t
        t -= 16
    return n


NN = (((1,), (0,)), ((), ()))
NT = (((1,), (1,)), ((), ()))
TN = (((0,), (0,)), ((), ()))
ROW_TILE = 512


def _matmul(name, grid, red_axis, ins, in_specs, terms, dims, acc_shapes, out_shapes, out_specs, epilogue, after=None):
    if after is not None:
        ins, in_specs = (*ins, after), [*in_specs, pl.BlockSpec(memory_space=pl.ANY)]
    n_in, n_out = len(ins), len(out_shapes)
    n_red = grid[red_axis]

    def body(*refs):
        in_refs, out_refs, acc_refs = refs[:n_in], refs[n_in:n_in + n_out], refs[n_in + n_out:]
        sums = {}
        for a_idx, a_sl, b_idx, b_sl, acc_idx in terms:
            a = (in_refs[a_idx][...] if a_sl is None else in_refs[a_idx][a_sl]).astype(BF16)
            b = (in_refs[b_idx][...] if b_sl is None else in_refs[b_idx][b_sl]).astype(BF16)
            prod = lax.dot_general(a, b, dims, preferred_element_type=F32)
            sums[acc_idx] = prod if acc_idx not in sums else sums[acc_idx] + prod
        if n_red == 1:
            for idx, v in sums.items():
                acc_refs[idx][...] = v
            epilogue(acc_refs, in_refs, out_refs)
            return
        k = pl.program_id(red_axis)

        @pl.when(k == 0)
        def _():
            for idx, v in sums.items():
                acc_refs[idx][...] = v

        @pl.when(k > 0)
        def _():
            for idx, v in sums.items():
                acc_refs[idx][...] += v

        @pl.when(k == n_red - 1)
        def _():
            epilogue(acc_refs, in_refs, out_refs)

    return pl.pallas_call(
        body, name=name, grid=grid, in_specs=in_specs, out_specs=out_specs, out_shape=out_shapes,
        scratch_shapes=[pltpu.VMEM(s, F32) for s in acc_shapes], compiler_params=_cp(len(grid)),
    )(*ins)


def _store_epilogue(acc_refs, in_refs, out_refs):
    for acc, out in zip(acc_refs, out_refs):
        out[...] = acc[...].astype(out.dtype)


def _rope_fwd(a, c, s1, s2):
    return a * c + pltpu.roll(a, ROPE_HALF, 1) * s1 + pltpu.roll(a, HEAD_DIM - ROPE_HALF, 1) * s2


def _rope_bwd(d, c, s1, s2):
    return d * c + pltpu.roll(d * s1, HEAD_DIM - ROPE_HALF, 1) + pltpu.roll(d * s2, ROPE_HALF, 1)


def _proj_fwd(hn, w_all, rc, rs1, rs2, n_rope_heads):
    n, d = hn.shape
    _, _, ws = w_all.shape
    tm = _tile(n, ROW_TILE)
    heads_per_shard = ws // HEAD_DIM
    rows = _tile(tm, 256)

    def epilogue(acc_refs, in_refs, out_refs):
        acc, out = acc_refs[0], out_refs[0]
        j = pl.program_id(0)
        for r0 in range(0, tm, rows):
            c, s1, s2 = (ref[pl.ds(r0, rows), :] for ref in in_refs[2:5])
            for hh in range(heads_per_shard):
                a = acc[pl.ds(r0, rows), pl.ds(hh * HEAD_DIM, HEAD_DIM)]
                roped = _rope_fwd(a, c, s1, s2)
                a = jnp.where(j * heads_per_shard + hh < n_rope_heads, roped, a)
                out[pl.ds(r0, rows), pl.ds(hh * HEAD_DIM, HEAD_DIM)] = a.astype(out.dtype)

    tab = pl.BlockSpec((tm, HEAD_DIM), lambda j, m, k: (m, 0))
    return _matmul(
        "proj_fwd", (N_DEV, n // tm, 1), 2, (hn, w_all, rc, rs1, rs2),
        [pl.BlockSpec((tm, d), lambda j, m, k: (m, 0)), pl.BlockSpec((None, d, ws), lambda j, m, k: (j, 0, 0)), tab, tab, tab],
        [(0, None, 1, None, 0)], NN, [(tm, ws)], [S((n, N_DEV * ws), BF16)], [pl.BlockSpec((tm, ws), lambda j, m, k: (m, j))], epilogue)[0]


def _dense_res(name, a, b, res, dims, out_dtype=F32):
    m, kdim = a.shape
    n = b.shape[1] if dims == NN else b.shape[0]
    tm, tn = _tile(m, ROW_TILE), _tile(n, 1024)
    ins = [a, b] + ([res] if res is not None else [])
    b_spec = pl.BlockSpec((kdim, tn), lambda j, i, k: (0, j)) if dims == NN else pl.BlockSpec((tn, kdim), lambda j, i, k: (j, 0))
    specs = [pl.BlockSpec((tm, kdim), lambda j, i, k: (i, 0)), b_spec]
    if res is not None:
        specs.append(pl.BlockSpec((tm, tn), lambda j, i, k: (i, j)))

    def epilogue(acc_refs, in_refs, out_refs):
        v = acc_refs[0][...]
        if res is not None:
            v = v + in_refs[2][...]
        out_refs[0][...] = v.astype(out_dtype)

    return _matmul(name, (n // tn, m // tm, 1), 2, ins, specs, [(0, None, 1, None, 0)], dims, [(tm, tn)],
                   [S((m, n), out_dtype)], [pl.BlockSpec((tm, tn), lambda j, i, k: (i, j))], epilogue)[0]


def _tn_full(name, a, b, out_dtype=BF16):
    m, kdim = a.shape
    n = b.shape[1]
    tk, tn = _tile(kdim, 512), _tile(n, 1024)
    return _matmul(name, (kdim // tk, n // tn, 1), 2, (a, b),
                   [pl.BlockSpec((m, tk), lambda i, j, t: (0, i)), pl.BlockSpec((m, tn), lambda i, j, t: (0, j))],
                   [(0, None, 1, None, 0)], TN, [(tk, tn)], [S((kdim, n), out_dtype)], [pl.BlockSpec((tk, tn), lambda i, j, t: (i, j))],
                   _store_epilogue)[0]


def _gate_fwd(hn, wg_all, after=None):
    n, d = hn.shape
    _, _, fs = wg_all.shape
    tm = _tile(n, ROW_TILE)
    o_spec = pl.BlockSpec((None, tm, fs), lambda j, m, k: (j, m, 0))
    return _matmul("gate_fwd", (N_DEV, n // tm, 1), 2, (hn, wg_all),
                   [pl.BlockSpec((tm, d), lambda j, m, k: (m, 0)), pl.BlockSpec((None, d, fs), lambda j, m, k: (j, 0, 0))],
                   [(0, None, 1, None, 0)], NN, [(tm, fs)], [S((N_DEV, n, fs), BF16)], [o_spec], _store_epilogue, after)[0]


def _up_fwd(hn, wu_all, gate):
    n, d = hn.shape
    _, _, fs = wu_all.shape
    tm = _tile(n, ROW_TILE)
    rows = _tile(tm, 256)

    def epilogue(acc_refs, in_refs, out_refs):
        for r0 in range(0, tm, rows):
            u = acc_refs[0][pl.ds(r0, rows), :]
            g = in_refs[2][pl.ds(r0, rows), :].astype(F32)
            out_refs[0][pl.ds(r0, rows), :] = u.astype(BF16)
            out_refs[1][pl.ds(r0, rows), :] = (g * jax.nn.sigmoid(g) * u).astype(BF16)

    t_spec = pl.BlockSpec((None, tm, fs), lambda j, m, k: (j, m, 0))
    o_shape = S((N_DEV, n, fs), BF16)
    return _matmul("up_fwd", (N_DEV, n // tm, 1), 2, (hn, wu_all, gate),
                   [pl.BlockSpec((tm, d), lambda j, m, k: (m, 0)), pl.BlockSpec((None, d, fs), lambda j, m, k: (j, 0, 0)), t_spec],
                   [(0, None, 1, None, 0)], NN, [(tm, fs)], [o_shape] * 2, [t_spec] * 2, epilogue)


def _down_fwd(act, wd_all, res):
    _, n, fs = act.shape
    d = wd_all.shape[2]
    tm, tn, group = _tile(n, ROW_TILE), _tile(d, 1024), 4

    def epilogue(acc_refs, in_refs, out_refs):
        out_refs[0][...] = acc_refs[0][...] + in_refs[2][...]

    return _matmul("down_fwd", (n // tm, d // tn, N_DEV // group), 2, (act, wd_all, res),
                   [pl.BlockSpec((group, tm, fs), lambda i, j, s: (s, i, 0)), pl.BlockSpec((group, fs, tn), lambda i, j, s: (s, 0, j)),
                    pl.BlockSpec((tm, tn), lambda i, j, s: (i, j))],
                   [(0, q, 1, q, 0) for q in range(group)], NN, [(tm, tn)], [S((n, d), F32)],
                   [pl.BlockSpec((tm, tn), lambda i, j, s: (i, j))], epilogue)[0]


def _dact_bwd(dh, wd_all, gate, up):
    n, d = dh.shape
    _, fs, _ = wd_all.shape
    tm = _tile(n, ROW_TILE)
    rows = _tile(tm, 256)

    def epilogue(acc_refs, in_refs, out_refs):
        for r0 in range(0, tm, rows):
            da = acc_refs[0][pl.ds(r0, rows), :]
            g = in_refs[2][pl.ds(r0, rows), :].astype(F32)
            u = in_refs[3][pl.ds(r0, rows), :].astype(F32)
            sg = jax.nn.sigmoid(g)
            out_refs[0][pl.ds(r0, rows), :] = (da * u * (sg * (1.0 + g * (1.0 - sg)))).astype(BF16)
            out_refs[1][pl.ds(r0, rows), :] = (da * (g * sg)).astype(BF16)

    t_spec = pl.BlockSpec((None, tm, fs), lambda j, m, k: (j, m, 0))
    o_shape = S((N_DEV, n, fs), BF16)
    return _matmul("dact_bwd", (N_DEV, n // tm, 1), 2, (dh, wd_all, gate, up),
                   [pl.BlockSpec((tm, d), lambda j, m, k: (m, 0)), pl.BlockSpec((None, fs, d), lambda j, m, k: (j, 0, 0)), t_spec, t_spec],
                   [(0, None, 1, None, 0)], NT, [(tm, fs)], [o_shape] * 2, [t_spec] * 2, epilogue)


def _dwd_bwd(act, dh):
    _, n, fs = act.shape
    d = dh.shape[1]
    tn = _tile(d, 512)
    return _matmul("dwd_bwd", (N_DEV, d // tn, 1), 2, (act, dh),
                   [pl.BlockSpec((None, n, fs), lambda j, c, t: (j, 0, 0)), pl.BlockSpec((n, tn), lambda j, c, t: (0, c))],
                   [(0, None, 1, None, 0)], TN, [(fs, tn)], [S((N_DEV, fs, d), BF16)], [pl.BlockSpec((None, fs, tn), lambda j, c, t: (j, 0, c))],
                   _store_epilogue)[0]


def _dw_cols_bwd(name, hn, dy, after=None):
    n, d = hn.shape
    ws = dy.shape[2]
    tk = _tile(d, 512)
    return _matmul(name, (N_DEV, d // tk, 1), 2, (hn, dy),
                   [pl.BlockSpec((n, tk), lambda j, c, t: (0, c)), pl.BlockSpec((None, n, ws), lambda j, c, t: (j, 0, 0))],
                   [(0, None, 1, None, 0)], TN, [(tk, ws)], [S((N_DEV, d, ws), BF16)], [pl.BlockSpec((None, tk, ws), lambda j, c, t: (j, c, 0))],
                   _store_epilogue, after)[0]


def _dwin_bwd(hn, dproj):
    n, d = hn.shape
    ws = dproj.shape[1] // N_DEV
    tk = _tile(d, 512)
    return _matmul("dwin_bwd", (N_DEV, d // tk, 1), 2, (hn, dproj),
                   [pl.BlockSpec((n, tk), lambda j, c, t: (0, c)), pl.BlockSpec((n, ws), lambda j, c, t: (0, j))],
                   [(0, None, 1, None, 0)], TN, [(tk, ws)], [S((N_DEV, d, ws), BF16)], [pl.BlockSpec((None, tk, ws), lambda j, c, t: (j, c, 0))],
                   _store_epilogue)[0]


def _dhn_from_shards(name, dys, ws_all, dy_is_flat, after=None):
    if dy_is_flat:
        n, ws = dys[0].shape[0], dys[0].shape[1] // N_DEV
    else:
        _, n, ws = dys[0].shape
    d = ws_all[0].shape[1]
    k_terms = len(dys)
    group = 4 // k_terms
    tm, tn = _tile(n, ROW_TILE), _tile(d, 1024)
    if dy_is_flat:
        y_spec = pl.BlockSpec((tm, group * ws), lambda i, j, s: (i, s))
        y_sl = lambda q: (slice(None), slice(q * ws, (q + 1) * ws))
    else:
        y_spec = pl.BlockSpec((group, tm, ws), lambda i, j, s: (s, i, 0))
        y_sl = lambda q: q
    w_spec = pl.BlockSpec((group, tn, ws), lambda i, j, s: (s, j, 0))
    return _matmul(name, (n // tm, d // tn, N_DEV // group), 2, (*dys, *ws_all), [y_spec] * k_terms + [w_spec] * k_terms,
                   [(i, y_sl(q), k_terms + i, q, 0) for i in range(k_terms) for q in range(group)], NT, [(tm, tn)], [S((n, d), F32)],
                   [pl.BlockSpec((tm, tn), lambda i, j, s: (i, j))], _store_epilogue, after)[0]


def _rms_fwd(name, x, g):
    n, d = x.shape
    tm = _tile(n, 256)

    def body(x_ref, g_ref, o_ref):
        xv = x_ref[...]
        r = lax.rsqrt(jnp.mean(xv * xv, axis=-1, keepdims=True) + RMS_EPS)
        o_ref[...] = (xv * r * g_ref[...]).astype(BF16)

    return pl.pallas_call(body, name=name, grid=(n // tm,),
                          in_specs=[pl.BlockSpec((tm, d), lambda i: (i, 0)), pl.BlockSpec((1, d), lambda i: (0, 0))],
                          out_specs=pl.BlockSpec((tm, d), lambda i: (i, 0)), out_shape=S((n, d), BF16), compiler_params=_cp(1))(x, g)


def _rms_bwd(name, x, g, dy, res):
    n, d = x.shape
    tm = _tile(n, 256)

    def body(x_ref, g_ref, dy_ref, res_ref, dx_ref, dxb_ref, dg_ref):
        xv, dyv = x_ref[...], dy_ref[...]
        r = lax.rsqrt(jnp.mean(xv * xv, axis=-1, keepdims=True) + RMS_EPS)
        xr = xv * r
        dgy = dyv * g_ref[...]
        dx = res_ref[...] + r * (dgy - xr * jnp.mean(dgy * xr, axis=-1, keepdims=True))
        dx_ref[...] = dx
        dxb_ref[...] = dx.astype(BF16)

        @pl.when(pl.program_id(0) == 0)
        def _():
            dg_ref[...] = jnp.zeros_like(dg_ref)

        dg_ref[...] += jnp.sum(dyv * xr, axis=0, keepdims=True)

    row = pl.BlockSpec((tm, d), lambda i: (i, 0))
    vec = pl.BlockSpec((1, d), lambda i: (0, 0))
    return pl.pallas_call(body, name=name, grid=(n // tm,), in_specs=[row, vec, row, row], out_specs=[row, row, vec],
                          out_shape=[S((n, d), F32), S((n, d), BF16), S((1, d), F32)], compiler_params=_cp(1))(x, g, dy, res)


def _final_loss(h, g, target):
    n, d = h.shape
    tm = _tile(n, 256)

    def body(h_ref, g_ref, t_ref, dh_ref, dhb_ref, dg_ref, sse_ref):
        hv, gv = h_ref[...], g_ref[...]
        r = lax.rsqrt(jnp.mean(hv * hv, axis=-1, keepdims=True) + RMS_EPS)
        hr = hv * r
        err = hr * gv - t_ref[...]
        dy = err * (1.0 / d)
        dgy = dy * gv
        dh = r * (dgy - hr * jnp.mean(dgy * hr, axis=-1, keepdims=True))
        dh_ref[...] = dh
        dhb_ref[...] = dh.astype(BF16)

        @pl.when(pl.program_id(0) == 0)
        def _():
            dg_ref[...] = jnp.zeros_like(dg_ref)
            sse_ref[...] = jnp.zeros_like(sse_ref)

        dg_ref[...] += jnp.sum(dy * hr, axis=0, keepdims=True)
        sse_ref[...] += jnp.sum(err * err)

    row = pl.BlockSpec((tm, d), lambda i: (i, 0))
    vec = pl.BlockSpec((1, d), lambda i: (0, 0))
    one = pl.BlockSpec((8, HEAD_DIM), lambda i: (0, 0))
    return pl.pallas_call(body, name="final_loss", grid=(n // tm,), in_specs=[row, vec, row], out_specs=[row, row, vec, one],
                          out_shape=[S((n, d), F32), S((n, d), BF16), S((1, d), F32), S((8, HEAD_DIM), F32)],
                          compiler_params=_cp(1))(h, g, target)


def _dot(a, b, dims):
    return lax.dot_general(a, b, dims, preferred_element_type=F32)


def _split3(x):
    hi = x.astype(BF16)
    r1 = x - hi.astype(F32)
    mid = r1.astype(BF16)
    lo = (r1 - mid.astype(F32)).astype(BF16)
    return hi, mid, lo


def _scan_cols(x, tri, terms):
    parts = _split3(x)[:terms]
    out = _dot(parts[0], tri, NN)
    for p in parts[1:]:
        out = out + _dot(p, tri, NN)
    return out


def _head_norm_fwd(o, g):
    r = lax.rsqrt(jnp.mean(o * o, axis=-1, keepdims=True) + RMS_EPS)
    return o * r * g


def _head_norm_bwd(o, g, dy):
    r = lax.rsqrt(jnp.mean(o * o, axis=-1, keepdims=True) + RMS_EPS)
    orr = o * r
    dgy = dy * g
    return r * (dgy - orr * jnp.mean(dgy * orr, axis=-1, keepdims=True)), dy * orr


def _interleave_plan(t):
    return [(i, dil, t // dil) for i, dil in enumerate(DILATIONS)]


def _band_mask(u, blocks_per_seq):
    row = lax.broadcasted_iota(jnp.int32, (BLK, 2 * BLK), 0)
    col = lax.broadcasted_iota(jnp.int32, (BLK, 2 * BLK), 1)
    dist = row + BLK - col
    has_prev = (u % blocks_per_seq) != 0
    return (dist >= 0) & (dist <= BLK) & ((col >= BLK) | has_prev)


def _dil_fwd(proj, g_dil, bl, t, n_heads):
    n = bl * t
    nb = t // BLK
    scale = HEAD_DIM ** -0.5
    plan = _interleave_plan(t)
    chunk = _tile(t, 256)

    def body(q_ref, k_ref, v_ref, g_ref, omix_ref, opre_ref, lse_ref, stg, qd, kd, vd, ob, lb, on, ln):
        for src, dst, pad in ((q_ref, qd, 0), (k_ref, kd, BLK), (v_ref, vd, BLK)):
            stg[...] = src[...].astype(F32)
            for bi, dil, sub in plan:
                if pad:
                    dst[bi, pl.ds(0, BLK), :] = jnp.zeros((BLK, HEAD_DIM), BF16)
                if dil == 1:
                    dst[bi, pl.ds(pad, t), :] = src[...]
                else:
                    for r in range(dil):
                        dst[bi, pl.ds(pad + r * sub, sub), :] = stg[pl.ds(r, sub, stride=dil), :].astype(BF16)

        for bi, dil, sub in plan:
            def blk(u, carry, bi=bi, sub=sub):
                rows = pl.ds(pl.multiple_of(u * BLK, BLK), BLK)
                win = pl.ds(pl.multiple_of(u * BLK, BLK), 2 * BLK)
                sc = _dot(qd[bi, rows, :], kd[bi, win, :], NT) * scale
                sc = jnp.where(_band_mask(u, sub // BLK), sc, NEG)
                m = jnp.max(sc, axis=-1, keepdims=True)
                p = jnp.exp(sc - m)
                den = jnp.sum(p, axis=-1, keepdims=True)
                ob[bi, rows, :] = _dot((p / den).astype(BF16), vd[bi, win, :], NN)
                lb[bi, rows, :] = jnp.broadcast_to(m + jnp.log(den), (BLK, HEAD_DIM))
                return carry
            lax.fori_loop(0, nb // DIL_UNROLL, lambda i, c, blk=blk: [blk(i * DIL_UNROLL + s, c) for s in range(DIL_UNROLL)][-1], 0)

        for bi, dil, sub in plan[1:]:
            for r in range(dil):
                on[bi - 1, pl.ds(r, sub, stride=dil), :] = ob[bi, pl.ds(r * sub, sub), :]
                ln[bi - 1, pl.ds(r, sub, stride=dil), :] = lb[bi, pl.ds(r * sub, sub), :]

        def merge(i, carry):
            rows = pl.ds(pl.multiple_of(i * chunk, chunk), chunk)
            l0, l1, l2 = lb[0, rows, :], ln[0, rows, :], ln[1, rows, :]
            mx = jnp.maximum(jnp.maximum(l0, l1), l2)
            w0, w1, w2 = jnp.exp(l0 - mx), jnp.exp(l1 - mx), jnp.exp(l2 - mx)
            tot = w0 + w1 + w2
            o = (w0 / tot) * ob[0, rows, :] + (w1 / tot) * on[0, rows, :] + (w2 / tot) * on[1, rows, :]
            lse_ref[rows, :] = mx + jnp.log(tot)
            opre_ref[rows, :] = o
            omix_ref[rows, :] = _head_norm_fwd(o, g_ref[...]).astype(BF16)
            return carry
        lax.fori_loop(0, t // chunk, merge, 0)

    hs = n_heads
    col = lambda off: pl.BlockSpec((t, HEAD_DIM), lambda b, h: (b, off + h))
    return pl.pallas_call(
        body, name="dil_fwd", grid=(bl, hs),
        in_specs=[col(0), col(hs), col(2 * hs), pl.BlockSpec((1, HEAD_DIM), lambda b, h: (0, h))],
        out_specs=[col(0), col(0), pl.BlockSpec((t, HEAD_DIM), lambda b, h: (b * hs + h, 0))],
        out_shape=[S((n, 2 * hs * HEAD_DIM), BF16), S((n, 2 * hs * HEAD_DIM), F32), S((bl * hs * t, HEAD_DIM), F32)],
        scratch_shapes=[pltpu.VMEM((t, HEAD_DIM), F32), pltpu.VMEM((3, t, HEAD_DIM), BF16),
                        pltpu.VMEM((3, t + BLK, HEAD_DIM), BF16), pltpu.VMEM((3, t + BLK, HEAD_DIM), BF16),
                        pltpu.VMEM((3, t, HEAD_DIM), F32), pltpu.VMEM((3, t, HEAD_DIM), F32),
                        pltpu.VMEM((2, t, HEAD_DIM), F32), pltpu.VMEM((2, t, HEAD_DIM), F32)],
        compiler_params=_cp(2),
    )(proj, proj, proj, g_dil)


def _dil_bwd(proj, opre, lse, d_omix, g_dil, rc, rs1, rs2, bl, t, n_heads):
    n = bl * t
    nb = t // BLK
    scale = HEAD_DIM ** -0.5
    plan = _interleave_plan(t)
    chunk = _tile(t, 256)

    def body(q_ref, k_ref, v_ref, opre_ref, lse_ref, dy_ref, g_ref, c_ref, s1_ref, s2_ref, out_ref, dg_ref,
             stg, qd, kd, vd, dod, ldd, dqd, dkc, dkp, dvc, dvp, sk, sv):
        which = pl.program_id(2)

        @pl.when(jnp.logical_and(which == 0, pl.program_id(1) == 0))
        def _():
            dg_ref[...] = jnp.zeros_like(dg_ref)

        @pl.when(which == 0)
        def _():
            def prep(i, dg):
                rows = pl.ds(pl.multiple_of(i * chunk, chunk), chunk)
                o = opre_ref[rows, :]
                d_o, dg_rows = _head_norm_bwd(o, g_ref[...], dy_ref[rows, :])
                stg[rows, :] = d_o
                lane = lax.broadcasted_iota(jnp.int32, (chunk, HEAD_DIM), 1)
                ldd[0, rows, :] = jnp.where(lane < HEAD_DIM // 2, lse_ref[rows, :], jnp.sum(d_o * o, axis=-1, keepdims=True))
                return dg + jnp.sum(dg_rows, axis=0, keepdims=True)
            dg_ref[...] += lax.fori_loop(0, t // chunk, prep, jnp.zeros((1, HEAD_DIM), F32))

            dod[0] = stg[...].astype(BF16)
            for bi, dil, sub in plan[1:]:
                for r in range(dil):
                    dst = pl.ds(r * sub, sub)
                    dod[bi, dst, :] = stg[pl.ds(r, sub, stride=dil), :].astype(BF16)
                    ldd[bi, dst, :] = ldd[0, pl.ds(r, sub, stride=dil), :]
            for src, dst, pad in ((q_ref, qd, 0), (k_ref, kd, BLK), (v_ref, vd, BLK)):
                stg[...] = src[...].astype(F32)
                for bi, dil, sub in plan:
                    if pad:
                        dst[bi, pl.ds(0, BLK), :] = jnp.zeros((BLK, HEAD_DIM), BF16)
                    if dil == 1:
                        dst[bi, pl.ds(pad, t), :] = src[...]
                    else:
                        for r in range(dil):
                            dst[bi, pl.ds(pad + r * sub, sub), :] = stg[pl.ds(r, sub, stride=dil), :].astype(BF16)

            for bi, dil, sub in plan:
                def blk(u, carry, bi=bi, sub=sub):
                    rows = pl.ds(pl.multiple_of(u * BLK, BLK), BLK)
                    win = pl.ds(pl.multiple_of(u * BLK, BLK), 2 * BLK)
                    qb, kw, vw, dob = qd[bi, rows, :], kd[bi, win, :], vd[bi, win, :], dod[bi, rows, :]
                    sc = _dot(qb, kw, NT) * scale
                    stats = ldd[bi, rows, :]
                    p = jnp.where(_band_mask(u, sub // BLK), jnp.exp(sc - stats[:, :1]), 0.0)
                    dp = _dot(dob, vw, NT)
                    ds = (p * (dp - stats[:, HEAD_DIM // 2:HEAD_DIM // 2 + 1]) * scale).astype(BF16)
                    dqd[bi, rows, :] = _dot(ds, kw, NN)
                    dk_win = _dot(ds, qb, TN)
                    dv_win = _dot(p.astype(BF16), dob, TN)
                    dkp[bi, rows, :] = dk_win[:BLK]
                    dkc[bi, rows, :] = dk_win[BLK:]
                    dvp[bi, rows, :] = dv_win[:BLK]
                    dvc[bi, rows, :] = dv_win[BLK:]
                    return carry
                lax.fori_loop(0, nb // DIL_UNROLL, lambda i, c, blk=blk: [blk(i * DIL_UNROLL + s, c) for s in range(DIL_UNROLL)][-1], 0)

            for cur, prev, undo_rope, dst in ((dqd, None, True, out_ref), (dkc, dkp, True, sk), (dvc, dvp, False, sv)):
                def summed(bi, start, size, cur=cur, prev=prev):
                    v = cur[bi, pl.ds(start, size), :]
                    if prev is None:
                        return v
                    if start + size < t:
                        return v + prev[bi, pl.ds(start + BLK, size), :]
                    if size == BLK:
                        return v
                    return v + jnp.concatenate([prev[bi, pl.ds(start + BLK, size - BLK), :], jnp.zeros((BLK, HEAD_DIM), F32)], axis=0)
                stg[...] = summed(0, 0, t)
                for bi, dil, sub in plan[1:]:
                    for r in range(dil):
                        stg[pl.ds(r, sub, stride=dil), :] += summed(bi, r * sub, sub)
                if undo_rope:
                    dst[...] = _rope_bwd(stg[...], c_ref[...], s1_ref[...], s2_ref[...]).astype(BF16)
                else:
                    dst[...] = stg[...].astype(BF16)

        @pl.when(which == 1)
        def _():
            out_ref[...] = sk[...]

        @pl.when(which == 2)
        def _():
            out_ref[...] = sv[...]

    hs = n_heads
    col = lambda off: pl.BlockSpec((t, HEAD_DIM), lambda h, b, w: (b, off + h))
    per_head = pl.BlockSpec((t, HEAD_DIM), lambda h, b, w: (b * hs + h, 0))
    tab = pl.BlockSpec((t, HEAD_DIM), lambda h, b, w: (b, 0))
    gvec = pl.BlockSpec((1, HEAD_DIM), lambda h, b, w: (0, h))
    tb = (t, HEAD_DIM)
    tp = (t + BLK, HEAD_DIM)
    return pl.pallas_call(
        body, name="dil_bwd", grid=(hs, bl, 3),
        in_specs=[col(0), col(hs), col(2 * hs), col(0), per_head, col(0), gvec, tab, tab, tab],
        out_specs=[pl.BlockSpec((t, HEAD_DIM), lambda h, b, w: (b, w * hs + h)), gvec],
        out_shape=[S((n, 6 * hs * HEAD_DIM), BF16), S((1, hs * HEAD_DIM), F32)],
        scratch_shapes=[pltpu.VMEM(tb, F32), pltpu.VMEM((3,) + tb, BF16), pltpu.VMEM((3,) + tp, BF16), pltpu.VMEM((3,) + tp, BF16),
                        pltpu.VMEM((3,) + tb, BF16), pltpu.VMEM((3,) + tb, F32), pltpu.VMEM((3,) + tb, F32),
                        pltpu.VMEM((3,) + tb, F32), pltpu.VMEM((3,) + tb, F32), pltpu.VMEM((3,) + tb, F32), pltpu.VMEM((3,) + tb, F32),
                        pltpu.VMEM(tb, BF16), pltpu.VMEM(tb, BF16)],
        compiler_params=_cp(3),
    )(proj, proj, proj, opre, lse, d_omix, g_dil, rc, rs1, rs2)


def _sb_tile(qb, kb, scale, shift):
    z = _dot(qb, kb, NT) * scale
    tl = jnp.log(1.0 + jnp.exp(-jnp.abs(z)))
    log_not = -(jnp.maximum(z, 0.0) + tl)
    strict = None
    if shift is not None:
        row = lax.broadcasted_iota(jnp.int32, (SBQ, SBK), 0)
        col = lax.broadcasted_iota(jnp.int32, (SBQ, SBK), 1)
        strict = col + shift < row
        log_not = jnp.where(strict, log_not, 0.0)
    return log_not, jnp.minimum(z, 0.0) - tl, strict


def _tri(cmp):
    row = lax.broadcasted_iota(jnp.int32, (SBK, SBK), 0)
    col = lax.broadcasted_iota(jnp.int32, (SBK, SBK), 1)
    return jnp.where(cmp(row, col), 1.0, 0.0).astype(BF16)


SB_EDGE = tuple(range(0, SBQ, SBK))


def _sb_fwd(proj, g_sb, omix_in, opre_in, bl, t, n_heads):
    nb = t // SBQ
    per_q = SBQ // SBK
    scale = HEAD_DIM ** -0.5

    def body(q_ref, k_ref, v_ref, g_ref, _omix_in, _opre_in, omix_ref, opre_ref, lt_ref):
        later = _tri(lambda r, c: r > c)

        def q_block(qi, carry):
            rows = pl.ds(pl.multiple_of(qi * SBQ, SBQ), SBQ)
            qb = q_ref[rows, :]

            def tile(kj, st, shift):
                run, acc = st
                krows = pl.ds(pl.multiple_of(kj * SBK, SBK), SBK)
                log_not, log_beta, strict = _sb_tile(qb, k_ref[krows, :], scale, shift)
                a = jnp.exp(log_beta + _scan_cols(log_not, later, 2) + run)
                if shift is not None:
                    a = jnp.where(strict, a, 0.0)
                return run + jnp.sum(log_not, axis=-1, keepdims=True), acc + _dot(a.astype(BF16), v_ref[krows, :], NN)

            st = (jnp.zeros((SBQ, 1), F32), jnp.zeros((SBQ, HEAD_DIM), F32))
            for shift in reversed(SB_EDGE):
                st = tile(qi * per_q + shift // SBK, st, shift)
            run, acc = lax.fori_loop(0, qi * per_q, lambda it, st: tile(qi * per_q - 1 - it, st, None), st)
            lt_ref[rows, :] = jnp.broadcast_to(run, (SBQ, HEAD_DIM))
            opre_ref[rows, :] = acc
            omix_ref[rows, :] = _head_norm_fwd(acc, g_ref[...]).astype(BF16)
            return carry

        lax.fori_loop(0, nb, q_block, 0)

    hs = n_heads
    col = lambda off: pl.BlockSpec((t, HEAD_DIM), lambda b, h: (b, off + h))
    return pl.pallas_call(
        body, name="sb_fwd", grid=(bl, hs),
        in_specs=[col(3 * hs), col(4 * hs), col(5 * hs), pl.BlockSpec((1, HEAD_DIM), lambda b, h: (0, h)), HBM_SPEC, HBM_SPEC],
        out_specs=[col(hs), col(hs), pl.BlockSpec((t, HEAD_DIM), lambda b, h: (b * hs + h, 0))],
        out_shape=[S(omix_in.shape, BF16), S(opre_in.shape, F32), S((bl * hs * t, HEAD_DIM), F32)],
        input_output_aliases={4: 0, 5: 1}, compiler_params=_cp(2),
    )(proj, proj, proj, g_sb, omix_in, opre_in)


def _sb_bwd(proj, opre, ltot, d_omix, g_sb, dproj_in, bl, t, n_heads):
    nb = t // SBQ
    per_q = SBQ // SBK
    scale = HEAD_DIM ** -0.5

    def body(q_ref, k_ref, v_ref, opre_ref, lt_ref, dy_ref, g_ref, _dproj_in, out_ref, dg_ref, dq, dk, dv):
        which = pl.program_id(2)

        @pl.when(jnp.logical_and(which == 0, pl.program_id(1) == 0))
        def _():
            dg_ref[...] = jnp.zeros_like(dg_ref)

        @pl.when(which == 0)
        def _():
            upto = _tri(lambda r, c: r <= c)
            before = _tri(lambda r, c: r < c)
            dk[...] = jnp.zeros_like(dk)
            dv[...] = jnp.zeros_like(dv)

            def q_block(qi, dg):
                rows = pl.ds(pl.multiple_of(qi * SBQ, SBQ), SBQ)
                qb = q_ref[rows, :]
                o = opre_ref[rows, :]
                d_o, dg_rows = _head_norm_bwd(o, g_ref[...], dy_ref[rows, :])
                dob = d_o.astype(BF16)
                lt = lt_ref[rows, :][:, :1]

                def tile(kj, st, shift):
                    run, grun, dq_acc = st
                    krows = pl.ds(pl.multiple_of(kj * SBK, SBK), SBK)
                    kb, vb = k_ref[krows, :], v_ref[krows, :]
                    log_not, log_beta, strict = _sb_tile(qb, kb, scale, shift)
                    excl = lt - (run + _scan_cols(log_not, upto, 2))
                    a = jnp.exp(log_beta + excl)
                    if shift is not None:
                        a = jnp.where(strict, a, 0.0)
                    g_a = a * _dot(dob, vb, NT)
                    g_before = grun + _scan_cols(g_a, before, 2)
                    dz = (g_a - (g_a + g_before) * jnp.exp(log_beta)) * scale
                    if shift is not None:
                        dz = jnp.where(strict, dz, 0.0)
                    dzb = dz.astype(BF16)
                    dk[krows, :] += _dot(dzb, qb, TN)
                    dv[krows, :] += _dot(a.astype(BF16), dob, TN)
                    return (run + jnp.sum(log_not, axis=-1, keepdims=True), grun + jnp.sum(g_a, axis=-1, keepdims=True),
                            dq_acc + _dot(dzb, kb, NN))

                zero_col = jnp.zeros((SBQ, 1), F32)
                st = lax.fori_loop(0, qi * per_q, lambda kj, st: tile(kj, st, None),
                                   (zero_col, zero_col, jnp.zeros((SBQ, HEAD_DIM), F32)))
                for shift in SB_EDGE:
                    st = tile(qi * per_q + shift // SBK, st, shift)
                dq_acc = st[2]
                dq[rows, :] = dq_acc.astype(BF16)
                return dg + jnp.sum(dg_rows, axis=0, keepdims=True)

            dg_ref[...] += lax.fori_loop(0, nb, q_block, jnp.zeros((1, HEAD_DIM), F32))
            out_ref[...] = dq[...]

        @pl.when(which == 1)
        def _():
            out_ref[...] = dk[...].astype(BF16)

        @pl.when(which == 2)
        def _():
            out_ref[...] = dv[...].astype(BF16)

    hs = n_heads
    col = lambda off: pl.BlockSpec((t, HEAD_DIM), lambda h, b, w: (b, off + h))
    per_head = pl.BlockSpec((t, HEAD_DIM), lambda h, b, w: (b * hs + h, 0))
    gvec = pl.BlockSpec((1, HEAD_DIM), lambda h, b, w: (0, h))
    tb = (t, HEAD_DIM)
    return pl.pallas_call(
        body, name="sb_bwd", grid=(hs, bl, 3),
        in_specs=[col(3 * hs), col(4 * hs), col(5 * hs), col(hs), per_head, col(hs), gvec, HBM_SPEC],
        out_specs=[pl.BlockSpec((t, HEAD_DIM), lambda h, b, w: (b, (3 + w) * hs + h)), gvec],
        out_shape=[S(dproj_in.shape, BF16), S((1, hs * HEAD_DIM), F32)],
        scratch_shapes=[pltpu.VMEM(tb, BF16), pltpu.VMEM(tb, F32), pltpu.VMEM(tb, F32)],
        input_output_aliases={7: 0}, compiler_params=_cp(3),
    )(proj, proj, proj, opre, ltot, d_omix, g_sb, dproj_in)


def _all_gather(shards):
    k_w = len(shards)

    def body(*refs):
        ins, outs = refs[:k_w], refs[k_w:2 * k_w]
        send_sems, recv_sems, local_sems = refs[2 * k_w:]
        x, y, c = lax.axis_index("x"), lax.axis_index("y"), lax.axis_index("c")
        me, sibling = (x, y, c), (x, y, 1 - c)
        chips = [(1 - x, y), (x, 1 - y), (1 - x, 1 - y)]

        def slot(dev):
            return 4 * dev[0] + 2 * dev[1] + dev[2]

        def copy(w, k, block, to, src=None):
            dst = outs[w].at[slot(block)]
            return pltpu.make_async_remote_copy(
                src_ref=dst if src is None else src, dst_ref=dst, send_sem=send_sems.at[w * 7 + k], recv_sem=recv_sems.at[w * 7 + k],
                device_id=to, device_id_type=MESH)

        mine = [pltpu.make_async_copy(ins[w], outs[w].at[slot(me)], local_sems.at[w]) for w in range(k_w)]
        first = []
        for w in range(k_w):
            mine[w].start()
            first.append(copy(w, 0, me, sibling, src=ins[w]))
            first += [copy(w, 1 + j, me, (*chip, c), src=ins[w]) for j, chip in enumerate(chips)]
        for cp in first:
            cp.start()
        passed = []
        for w in range(k_w):
            for j, chip in enumerate(chips):
                copy(w, 1 + j, (*chip, c), me).wait_recv()
                fwd = copy(w, 4 + j, (*chip, c), sibling)
                fwd.start()
                passed.append(fwd)
        for w in range(k_w):
            copy(w, 0, sibling, me).wait_recv()
            for j, chip in enumerate(chips):
                copy(w, 4 + j, (*chip, 1 - c), me).wait_recv()
        for cp in first + passed:
            cp.wait_send()
        for cp in mine:
            cp.wait()

    return pl.pallas_call(
        body, name="weights_all_gather", in_specs=[HBM_SPEC] * k_w, out_specs=[HBM_SPEC] * k_w,
        out_shape=[S((N_DEV,) + s.shape, s.dtype) for s in shards],
        scratch_shapes=[pltpu.SemaphoreType.DMA((7 * k_w,)), pltpu.SemaphoreType.DMA((7 * k_w,)), pltpu.SemaphoreType.DMA((k_w,))],
    )(*shards)


def _grad_exchange(grads, after):
    k_w = len(grads)

    def body(*refs):
        ins, outs = refs[:k_w], refs[k_w + 1:2 * k_w + 1]
        send_sems, recv_sems, local_sems = refs[2 * k_w + 1:]
        local, remote = _exchange_copies(ins, outs, send_sems, recv_sems, local_sems, False, ALL_PEERS)
        for cp in local + remote:
            cp.start()
        for cp in remote:
            cp.wait_send()
            cp.wait_recv()
        for cp in local:
            cp.wait()

    n_sem = (N_DEV - 1) * k_w
    return pl.pallas_call(
        body, name="grad_exchange", in_specs=[HBM_SPEC] * k_w + [pl.BlockSpec(memory_space=pl.ANY)], out_specs=[HBM_SPEC] * k_w,
        out_shape=[S(g.shape, g.dtype) for g in grads],
        scratch_shapes=[pltpu.SemaphoreType.DMA((n_sem,)), pltpu.SemaphoreType.DMA((n_sem,)), pltpu.SemaphoreType.DMA((k_w,))],
    )(*grads, after)


def _peer(x, y, c, k):
    px, py, pc = x ^ (k >> 2), y ^ ((k >> 1) & 1), c ^ (k & 1)
    return (px, py, pc), 4 * px + 2 * py + pc


ALL_PEERS = tuple(range(1, N_DEV))
SAME_CORE_PEERS = (2, 4, 6)


def _exchange_copies(srcs, lands, send_sems, recv_sems, local_sems, gather, peers):
    x, y, c = lax.axis_index("x"), lax.axis_index("y"), lax.axis_index("c")
    my_slot = 4 * x + 2 * y + c
    local, remote = [], []
    for w, (src, land) in enumerate(zip(srcs, lands)):
        local.append(pltpu.make_async_copy(src if gather else src.at[my_slot], land.at[my_slot], local_sems.at[w]))
        for i, k in enumerate(peers):
            peer, peer_slot = _peer(x, y, c, k)
            remote.append(pltpu.make_async_remote_copy(
                src_ref=src if gather else src.at[peer_slot], dst_ref=land.at[my_slot],
                send_sem=send_sems.at[w * len(peers) + i], recv_sem=recv_sems.at[w * len(peers) + i],
                device_id=peer, device_id_type=MESH))
    return local, remote


def _exchange_start(name, srcs, gather, after, peers=ALL_PEERS):
    k_w = len(srcs)
    land_shapes = [((N_DEV,) + s.shape) if gather else s.shape for s in srcs]

    def body(*refs):
        src_refs, land_refs = refs[:k_w], refs[k_w:2 * k_w]
        send_sems, recv_sems, local_sems = refs[2 * k_w + 1:2 * k_w + 4]
        token = refs[-1]
        local, remote = _exchange_copies(src_refs, land_refs, send_sems, recv_sems, local_sems, gather, peers)
        for cp in local + remote:
            cp.start()
        token[...] = jnp.zeros_like(token)

    n_sem = len(peers) * k_w
    hbm = lambda a: pltpu.with_memory_space_constraint(a, pltpu.HBM)
    outs = pl.pallas_call(
        body, name=name,
        in_specs=[HBM_SPEC] * (2 * k_w) + [pl.BlockSpec(memory_space=pl.ANY)],
        out_shape=(pltpu.SemaphoreType.DMA((n_sem,)), pltpu.SemaphoreType.DMA((n_sem,)), pltpu.SemaphoreType.DMA((k_w,)),
                   *[pltpu.HBM(s.shape, s.dtype) for s in srcs], *[pltpu.HBM(ls, s.dtype) for ls, s in zip(land_shapes, srcs)],
                   S((8, HEAD_DIM), F32)),
        out_specs=(SEM_SPEC, SEM_SPEC, SEM_SPEC, *[HBM_SPEC] * (2 * k_w), pl.BlockSpec(memory_space=pltpu.VMEM)),
        input_output_aliases={i: 3 + i for i in range(2 * k_w)},
        compiler_params=pltpu.CompilerParams(has_side_effects=SIDE_EFFECT),
    )(*[hbm(s) for s in srcs], *[hbm(lax.empty(ls, s.dtype)) for ls, s in zip(land_shapes, srcs)], after)
    return dict(sems=outs[:3], srcs=outs[3:3 + k_w], lands=outs[3 + k_w:3 + 2 * k_w], token_block=outs[-1], token=outs[-1][0, 0], gather=gather, peers=peers)


def _exchange_wait(name, handle, after):
    k_w = len(handle["srcs"])
    gather = handle["gather"]

    def body(*refs):
        src_refs, land_refs = refs[:k_w], refs[k_w:2 * k_w]
        send_sems, recv_sems, local_sems = refs[2 * k_w:2 * k_w + 3]
        local, remote = _exchange_copies(src_refs, land_refs, send_sems, recv_sems, local_sems, gather, handle["peers"])
        for cp in local:
            cp.wait()
        for cp in remote:
            cp.wait_send()
            cp.wait_recv()

    outs = pl.pallas_call(
        body, name=name,
        in_specs=[HBM_SPEC] * (2 * k_w) + [SEM_SPEC] * 3 + [pl.BlockSpec(memory_space=pl.ANY)],
        out_shape=tuple(pltpu.HBM(a.shape, a.dtype) for a in (*handle["srcs"], *handle["lands"])),
        out_specs=tuple([HBM_SPEC] * (2 * k_w)),
        input_output_aliases={i: i for i in range(2 * k_w)},
        compiler_params=pltpu.CompilerParams(has_side_effects=SIDE_EFFECT),
    )(*handle["srcs"], *handle["lands"], *handle["sems"], after)
    return outs[k_w:]


CHIPS = ((0, 0), (0, 1), (1, 0), (1, 1))


def _swap_copies(lands, send_sems, recv_sems):
    x, y, c = lax.axis_index("x"), lax.axis_index("y"), lax.axis_index("c")
    copies = []
    for w, land in enumerate(lands):
        for i, (px, py) in enumerate(CHIPS):
            mine = land.at[4 * px + 2 * py + c]
            copies.append(pltpu.make_async_remote_copy(
                src_ref=mine, dst_ref=mine, send_sem=send_sems.at[w * len(CHIPS) + i], recv_sem=recv_sems.at[w * len(CHIPS) + i],
                device_id=(x, y, 1 - c), device_id_type=MESH))
    return copies


def _swap_start(name, lands, after):
    k_w = len(lands)

    def body(*refs):
        land_refs = refs[:k_w]
        send_sems, recv_sems = refs[k_w + 1:k_w + 3]
        for cp in _swap_copies(land_refs, send_sems, recv_sems):
            cp.start()
        refs[-1][...] = jnp.zeros_like(refs[-1])

    n_sem = len(CHIPS) * k_w
    outs = pl.pallas_call(
        body, name=name, in_specs=[HBM_SPEC] * k_w + [pl.BlockSpec(memory_space=pl.ANY)],
        out_shape=(pltpu.SemaphoreType.DMA((n_sem,)), pltpu.SemaphoreType.DMA((n_sem,)),
                   *[pltpu.HBM(a.shape, a.dtype) for a in lands], S((8, HEAD_DIM), F32)),
        out_specs=(SEM_SPEC, SEM_SPEC, *[HBM_SPEC] * k_w, pl.BlockSpec(memory_space=pltpu.VMEM)),
        input_output_aliases={i: 2 + i for i in range(k_w)},
        compiler_params=pltpu.CompilerParams(has_side_effects=SIDE_EFFECT),
    )(*lands, after)
    return dict(sems=outs[:2], lands=outs[2:2 + k_w], token_block=outs[-1])


def _swap_wait(name, handle, after):
    k_w = len(handle["lands"])

    def body(*refs):
        for cp in _swap_copies(refs[:k_w], refs[k_w], refs[k_w + 1]):
            cp.wait_send()
            cp.wait_recv()

    return pl.pallas_call(
        body, name=name, in_specs=[HBM_SPEC] * k_w + [SEM_SPEC] * 2 + [pl.BlockSpec(memory_space=pl.ANY)],
        out_shape=tuple(pltpu.HBM(a.shape, a.dtype) for a in handle["lands"]), out_specs=tuple([HBM_SPEC] * k_w),
        input_output_aliases={i: i for i in range(k_w)},
        compiler_params=pltpu.CompilerParams(has_side_effects=SIDE_EFFECT),
    )(*handle["lands"], *handle["sems"], after)


def _adamw(name, parts, w, m, v):
    r, c = w.shape
    tr = _tile(r, max(16, (1 << 19) // c // 16 * 16))

    def body(p_ref, w_ref, m_ref, v_ref, g_ref, d_ref, nm_ref, nv_ref):
        g = p_ref[0].astype(F32)
        for s in range(1, N_DEV):
            g = g + p_ref[s].astype(F32)
        m_new = ADAM_B1 * m_ref[...] + (1.0 - ADAM_B1) * g
        v_new = ADAM_B2 * v_ref[...] + (1.0 - ADAM_B2) * jnp.square(g)
        m_hat = m_new / (1.0 - ADAM_B1 ** ADAM_STEP)
        v_hat = v_new / (1.0 - ADAM_B2 ** ADAM_STEP)
        g_ref[...] = g
        d_ref[...] = -ADAM_LR * (m_hat / (jnp.sqrt(v_hat) + ADAM_EPS) + ADAM_WD * w_ref[...])
        nm_ref[...] = m_new
        nv_ref[...] = v_new

    blk = pl.BlockSpec((tr, c), lambda i: (i, 0))
    return pl.pallas_call(body, name=name, grid=(r // tr,), in_specs=[pl.BlockSpec((N_DEV, tr, c), lambda i: (0, i, 0)), blk, blk, blk],
                          out_specs=[blk] * 4, out_shape=[S((r, c), F32)] * 4, compiler_params=_cp(1))(parts, w, m, v)


def _rope_tables(positions):
    inv_freq = jnp.power(jnp.float32(ROPE_THETA), -jnp.arange(ROPE_HALF, dtype=F32) / ROPE_HALF)
    ang = positions.astype(F32).reshape(-1, 1) * inv_freq
    cos, sin = jnp.cos(ang), jnp.sin(ang)
    n = ang.shape[0]
    rest = HEAD_DIM - 2 * ROPE_HALF
    zeros = jnp.zeros((n, ROPE_HALF), F32)
    c = jnp.concatenate([cos, cos, jnp.ones((n, rest), F32)], axis=1)
    s1 = jnp.concatenate([zeros, sin, jnp.zeros((n, rest), F32)], axis=1)
    s2 = jnp.concatenate([-sin, zeros, jnp.zeros((n, rest), F32)], axis=1)
    return c, s1, s2


def kernel(x, positions, norm_mix_g, w_in, norm_out_dil_g, norm_out_sb_g, w_out, norm_ffn_g, w_gate, w_up, w_down, norm_final_g, loss_target, m_norm_mix_g, m_w_in, m_norm_out_dil_g, m_norm_out_sb_g, m_w_out, m_norm_ffn_g, m_w_gate, m_w_up, m_w_down, m_norm_final_g, v_norm_mix_g, v_w_in, v_norm_out_dil_g, v_norm_out_sb_g, v_w_out, v_norm_ffn_g, v_w_gate, v_w_up, v_w_down, v_norm_final_g):
    bl, t, d = x.shape
    n = bl * t
    hs = d // (2 * HEAD_DIM)
    x2 = x.reshape(n, d)
    target = loss_target.reshape(n, d)
    g_final = norm_final_g.reshape(1, d)
    rc, rs1, rs2 = _rope_tables(positions)

    (win_all,) = _all_gather([w_in[0].astype(BF16)])
    ex_wout = _exchange_start("wout_gather_start", [w_out[0].astype(BF16)], True, win_all, SAME_CORE_PEERS)
    ex_wgu = _exchange_start("wgu_gather_start", [w_gate[0].astype(BF16), w_up[0].astype(BF16)], True, ex_wout["token_block"], SAME_CORE_PEERS)
    ex_wd = _exchange_start("wd_gather_start", [w_down[0].astype(BF16)], True, ex_wgu["token_block"], SAME_CORE_PEERS)
    rc = rc + ex_wd["token"]

    hn1 = _rms_fwd("rms_mix_fwd", x2, norm_mix_g)
    proj = _proj_fwd(hn1, win_all, rc, rs1, rs2, 2 * hs)
    omix, opre, lse = _dil_fwd(proj, norm_out_dil_g, bl, t, hs)
    omix, opre, ltot = _sb_fwd(proj, norm_out_sb_g, omix, opre, bl, t, hs)
    (wout_half,) = _exchange_wait("wout_gather_wait", ex_wout, ltot)
    wg_half, wu_half = _exchange_wait("wgu_gather_wait", ex_wgu, wout_half)
    sw_wout = _swap_start("wout_swap_start", [wout_half], wg_half)
    sw_wgu = _swap_start("wgu_swap_start", [wg_half, wu_half], sw_wout["token_block"])
    (wout_all,) = _swap_wait("wout_swap_wait", sw_wout, sw_wgu["token_block"])
    wout_full = wout_all.reshape(d, d)
    h1 = _dense_res("out_fwd", omix, wout_full, x2, NN)
    hn2 = _rms_fwd("rms_ffn_fwd", h1, norm_ffn_g)
    wg_all, wu_all = _swap_wait("wgu_swap_wait", sw_wgu, hn2)
    (wd_half,) = _exchange_wait("wd_gather_wait", ex_wd, wg_all)
    sw_wd = _swap_start("wd_swap_start", [wd_half], wg_all)
    gate = _gate_fwd(hn2, wg_all, sw_wd["token_block"])
    up, act = _up_fwd(hn2, wu_all, gate)
    (wd_all,) = _swap_wait("wd_swap_wait", sw_wd, act)
    h2 = _down_fwd(act, wd_all, h1)
    dh2, dh2b, dg_final, sse = _final_loss(h2, g_final, target)
    loss = lax.psum(sse[0, 0], ("x", "y", "c")) * (0.5 / d)

    dgate, dup = _dact_bwd(dh2b, wd_all, gate, up)
    dwd = _dwd_bwd(act, dh2b)
    ex_dwd = _exchange_start("dwd_exchange_start", [dwd], False, dgate)
    dwg = _dw_cols_bwd("dwg_bwd", hn2, dgate, ex_dwd["token_block"])
    dwu = _dw_cols_bwd("dwu_bwd", hn2, dup)
    ex_dwgu = _exchange_start("dwgu_exchange_start", [dwg, dwu], False, ex_dwd["token_block"])
    dhn2 = _dhn_from_shards("dhn2_bwd", (dgate, dup), (wg_all, wu_all), False, ex_dwgu["token_block"])
    dh1, dh1b, dg_ffn = _rms_bwd("rms_ffn_bwd", h1, norm_ffn_g, dhn2, dh2)
    d_omix = _dense_res("domix_bwd", dh1b, wout_full, None, NT)
    dwout = _tn_full("dwout_bwd", omix, dh1b).reshape(N_DEV, d // N_DEV, d)
    ex_dwout = _exchange_start("dwout_exchange_start", [dwout], False, d_omix)
    dproj, dg_dil = _dil_bwd(proj, opre, lse, d_omix, norm_out_dil_g + ex_dwout["token"], rc, rs1, rs2, bl, t, hs)
    dproj, dg_sb = _sb_bwd(proj, opre, ltot, d_omix, norm_out_sb_g, dproj, bl, t, hs)
    dwin = _dwin_bwd(hn1, dproj)
    ex_dwin = _exchange_start("dwin_exchange_start", [dwin], False, dproj)
    dhn1 = _dhn_from_shards("dhn1_bwd", (dproj,), (win_all,), True, ex_dwin["token_block"])
    dx, _, dg_mix = _rms_bwd("rms_mix_bwd", x2, norm_mix_g, dhn1, dh1)

    gains = [norm_mix_g, norm_out_dil_g, norm_out_sb_g, norm_ffn_g, g_final]
    m_gains = [m_norm_mix_g, m_norm_out_dil_g, m_norm_out_sb_g, m_norm_ffn_g, m_norm_final_g.reshape(1, d)]
    v_gains = [v_norm_mix_g, v_norm_out_dil_g, v_norm_out_sb_g, v_norm_ffn_g, v_norm_final_g.reshape(1, d)]
    dg_vec = jnp.concatenate([dg_mix, dg_dil, dg_sb, dg_ffn, dg_final], axis=1)
    dg_all = jnp.broadcast_to(dg_vec[None], (N_DEV,) + dg_vec.shape)

    out_w = {}
    (rwd,) = _exchange_wait("dwd_exchange_wait", ex_dwd, dx)
    out_w["w_down"] = _adamw("adamw_w_down", rwd, w_down[0], m_w_down[0], v_w_down[0])
    rwg, rwu = _exchange_wait("dwgu_exchange_wait", ex_dwgu, out_w["w_down"][0])
    out_w["w_gate"] = _adamw("adamw_w_gate", rwg, w_gate[0], m_w_gate[0], v_w_gate[0])
    out_w["w_up"] = _adamw("adamw_w_up", rwu, w_up[0], m_w_up[0], v_w_up[0])
    (rwout,) = _exchange_wait("dwout_exchange_wait", ex_dwout, out_w["w_up"][0])
    out_w["w_out"] = _adamw("adamw_w_out", rwout, w_out[0], m_w_out[0], v_w_out[0])
    (rwin,) = _exchange_wait("dwin_exchange_wait", ex_dwin, out_w["w_out"][0])
    out_w["w_in"] = _adamw("adamw_w_in", rwin, w_in[0], m_w_in[0], v_w_in[0])
    (rg,) = _grad_exchange([dg_all], out_w["w_in"][0])
    out_w = {name: [o[None] for o in outs] for name, outs in out_w.items()}
    cat = lambda vs: jnp.concatenate(vs, axis=1)
    gain_out = _adamw("adamw_gains", rg, cat(gains), cat(m_gains), cat(v_gains))
    widths = [d, d // 2, d // 2, d]
    cuts = [sum(widths[:i + 1]) for i in range(4)]
    gain_split = [jnp.split(o, cuts, axis=1) for o in gain_out]

    def ordered(kind):
        gs = gain_split[kind]
        return (gs[0], out_w["w_in"][kind], gs[1], gs[2], out_w["w_out"][kind], gs[3], out_w["w_gate"][kind],
                out_w["w_up"][kind], out_w["w_down"][kind], gs[4].reshape(d))

    return (loss, dx.reshape(bl, t, d), *ordered(0), *ordered(1), *ordered(2), *ordered(3))
```

```python
import functools
import math

import jax
import jax.numpy as jnp
from jax import lax
from jax.experimental import pallas as pl
from jax.experimental.pallas import tpu as pltpu

F32 = jnp.float32
BF16 = jnp.bfloat16
S = jax.ShapeDtypeStruct

N_DEV = 8
HEAD_DIM = 128
BLK = 128
SBQ = 512
SBK = 256
DIL_UNROLL = 8
ROPE_HALF = 16
ROPE_THETA = 500000.0
RMS_EPS = 1e-5
DILATIONS = (1, 4, 16)
NEG = -1e30
VMEM_LIMIT = 56 * 1024 * 1024

ADAM_LR = 0.001
ADAM_B1 = 0.9
ADAM_B2 = 0.999
ADAM_EPS = 1e-08
ADAM_WD = 0.01
ADAM_STEP = 10

MESH = pl.DeviceIdType.MESH
HBM_SPEC = pl.BlockSpec(memory_space=pltpu.HBM)
SEM_SPEC = pl.BlockSpec(memory_space=pltpu.SEMAPHORE)
SIDE_EFFECT = pltpu.SideEffectType.DATAFLOW_SIDE_EFFECTING


def _cp(n_axes):
    return pltpu.CompilerParams(dimension_semantics=("arbitrary",) * n_axes, vmem_limit_bytes=VMEM_LIMIT)


def _tile(n, want):
    if n <= want:
        return n
    t = want
    while t >= 16:
        if n % t == 0 and t % 16 == 0:
            return t
        t -= 16
    return n


NN = (((1,), (0,)), ((), ()))
NT = (((1,), (1,)), ((), ()))
TN = (((0,), (0,)), ((), ()))
ROW_TILE = 512


def _matmul(name, grid, red_axis, ins, in_specs, terms, dims, acc_shapes, out_shapes, out_specs, epilogue, after=None):
    if after is not None:
        ins, in_specs = (*ins, after), [*in_specs, pl.BlockSpec(memory_space=pl.ANY)]
    n_in, n_out = len(ins), len(out_shapes)
    n_red = grid[red_axis]

    def body(*refs):
        in_refs, out_refs, acc_refs = refs[:n_in], refs[n_in:n_in + n_out], refs[n_in + n_out:]
        sums = {}
        for a_idx, a_sl, b_idx, b_sl, acc_idx in terms:
            a = (in_refs[a_idx][...] if a_sl is None else in_refs[a_idx][a_sl]).astype(BF16)
            b = (in_refs[b_idx][...] if b_sl is None else in_refs[b_idx][b_sl]).astype(BF16)
            prod = lax.dot_general(a, b, dims, preferred_element_type=F32)
            sums[acc_idx] = prod if acc_idx not in sums else sums[acc_idx] + prod
        if n_red == 1:
            for idx, v in sums.items():
                acc_refs[idx][...] = v
            epilogue(acc_refs, in_refs, out_refs)
            return
        k = pl.program_id(red_axis)

        @pl.when(k == 0)
        def _():
            for idx, v in sums.items():
                acc_refs[idx][...] = v

        @pl.when(k > 0)
        def _():
            for idx, v in sums.items():
                acc_refs[idx][...] += v

        @pl.when(k == n_red - 1)
        def _():
            epilogue(acc_refs, in_refs, out_refs)

    return pl.pallas_call(
        body, name=name, grid=grid, in_specs=in_specs, out_specs=out_specs, out_shape=out_shapes,
        scratch_shapes=[pltpu.VMEM(s, F32) for s in acc_shapes], compiler_params=_cp(len(grid)),
    )(*ins)


def _store_epilogue(acc_refs, in_refs, out_refs):
    for acc, out in zip(acc_refs, out_refs):
        out[...] = acc[...].astype(out.dtype)


def _rope_fwd(a, c, s1, s2):
    return a * c + pltpu.roll(a, ROPE_HALF, 1) * s1 + pltpu.roll(a, HEAD_DIM - ROPE_HALF, 1) * s2


def _rope_bwd(d, c, s1, s2):
    return d * c + pltpu.roll(d * s1, HEAD_DIM - ROPE_HALF, 1) + pltpu.roll(d * s2, ROPE_HALF, 1)


def _proj_fwd(hn, w_all, rc, rs1, rs2, n_rope_heads):
    n, d = hn.shape
    _, _, ws = w_all.shape
    tm = _tile(n, ROW_TILE)
    heads_per_shard = ws // HEAD_DIM
    rows = _tile(tm, 256)

    def epilogue(acc_refs, in_refs, out_refs):
        acc, out = acc_refs[0], out_refs[0]
        j = pl.program_id(0)
        for r0 in range(0, tm, rows):
            c, s1, s2 = (ref[pl.ds(r0, rows), :] for ref in in_refs[2:5])
            for hh in range(heads_per_shard):
                a = acc[pl.ds(r0, rows), pl.ds(hh * HEAD_DIM, HEAD_DIM)]
                roped = _rope_fwd(a, c, s1, s2)
                a = jnp.where(j * heads_per_shard + hh < n_rope_heads, roped, a)
                out[pl.ds(r0, rows), pl.ds(hh * HEAD_DIM, HEAD_DIM)] = a.astype(out.dtype)

    tab = pl.BlockSpec((tm, HEAD_DIM), lambda j, m, k: (m, 0))
    return _matmul(
        "proj_fwd", (N_DEV, n // tm, 1), 2, (hn, w_all, rc, rs1, rs2),
        [pl.BlockSpec((tm, d), lambda j, m, k: (m, 0)), pl.BlockSpec((None, d, ws), lambda j, m, k: (j, 0, 0)), tab, tab, tab],
        [(0, None, 1, None, 0)], NN, [(tm, ws)], [S((n, N_DEV * ws), BF16)], [pl.BlockSpec((tm, ws), lambda j, m, k: (m, j))], epilogue)[0]


def _dense_res(name, a, b, res, dims, out_dtype=F32):
    m, kdim = a.shape
    n = b.shape[1] if dims == NN else b.shape[0]
    tm, tn = _tile(m, ROW_TILE), _tile(n, 1024)
    ins = [a, b] + ([res] if res is not None else [])
    b_spec = pl.BlockSpec((kdim, tn), lambda j, i, k: (0, j)) if dims == NN else pl.BlockSpec((tn, kdim), lambda j, i, k: (j, 0))
    specs = [pl.BlockSpec((tm, kdim), lambda j, i, k: (i, 0)), b_spec]
    if res is not None:
        specs.append(pl.BlockSpec((tm, tn), lambda j, i, k: (i, j)))

    def epilogue(acc_refs, in_refs, out_refs):
        v = acc_refs[0][...]
        if res is not None:
            v = v + in_refs[2][...]
        out_refs[0][...] = v.astype(out_dtype)

    return _matmul(name, (n // tn, m // tm, 1), 2, ins, specs, [(0, None, 1, None, 0)], dims, [(tm, tn)],
                   [S((m, n), out_dtype)], [pl.BlockSpec((tm, tn), lambda j, i, k: (i, j))], epilogue)[0]


def _tn_full(name, a, b, out_dtype=BF16):
    m, kdim = a.shape
    n = b.shape[1]
    tk, tn = _tile(kdim, 512), _tile(n, 1024)
    return _matmul(name, (kdim // tk, n // tn, 1), 2, (a, b),
                   [pl.BlockSpec((m, tk), lambda i, j, t: (0, i)), pl.BlockSpec((m, tn), lambda i, j, t: (0, j))],
                   [(0, None, 1, None, 0)], TN, [(tk, tn)], [S((kdim, n), out_dtype)], [pl.BlockSpec((tk, tn), lambda i, j, t: (i, j))],
                   _store_epilogue)[0]


def _gate_fwd(hn, wgt_all, after=None):
    n, d = hn.shape
    _, fs, _ = wgt_all.shape
    tm = _tile(n, ROW_TILE)
    o_spec = pl.BlockSpec((None, tm, fs), lambda j, m, k: (j, m, 0))
    return _matmul("gate_fwd", (N_DEV, n // tm, 1), 2, (hn, wgt_all),
                   [pl.BlockSpec((tm, d), lambda j, m, k: (m, 0)), pl.BlockSpec((None, fs, d), lambda j, m, k: (j, 0, 0))],
                   [(0, None, 1, None, 0)], NT, [(tm, fs)], [S((N_DEV, n, fs), BF16)], [o_spec], _store_epilogue, after)[0]


def _up_fwd(hn, wut_all, gate):
    n, d = hn.shape
    _, fs, _ = wut_all.shape
    tm = _tile(n, ROW_TILE)
    rows = _tile(tm, 256)

    def epilogue(acc_refs, in_refs, out_refs):
        for r0 in range(0, tm, rows):
            u = acc_refs[0][pl.ds(r0, rows), :]
            g = in_refs[2][pl.ds(r0, rows), :].astype(F32)
            out_refs[0][pl.ds(r0, rows), :] = u.astype(BF16)
            out_refs[1][pl.ds(r0, rows), :] = (g * jax.nn.sigmoid(g) * u).astype(BF16)

    t_spec = pl.BlockSpec((None, tm, fs), lambda j, m, k: (j, m, 0))
    o_shape = S((N_DEV, n, fs), BF16)
    return _matmul("up_fwd", (N_DEV, n // tm, 1), 2, (hn, wut_all, gate),
                   [pl.BlockSpec((tm, d), lambda j, m, k: (m, 0)), pl.BlockSpec((None, fs, d), lambda j, m, k: (j, 0, 0)), t_spec],
                   [(0, None, 1, None, 0)], NT, [(tm, fs)], [o_shape] * 2, [t_spec] * 2, epilogue)


def _rows_fwd(name, ys, ws_all, res, after=None):
    _, n, fs = ys[0].shape
    d = ws_all[0].shape[2]
    k_terms = len(ys)
    group = 4 // k_terms
    tm, tn = _tile(n, ROW_TILE), _tile(d, 1024)
    ins = [*ys, *ws_all] + ([res] if res is not None else [])
    specs = ([pl.BlockSpec((group, tm, fs), lambda i, j, s: (s, i, 0))] * k_terms
             + [pl.BlockSpec((group, fs, tn), lambda i, j, s: (s, 0, j))] * k_terms)
    if res is not None:
        specs.append(pl.BlockSpec((tm, tn), lambda i, j, s: (i, j)))

    def epilogue(acc_refs, in_refs, out_refs):
        v = acc_refs[0][...]
        out_refs[0][...] = v if res is None else v + in_refs[2 * k_terms][...]

    return _matmul(name, (n // tm, d // tn, N_DEV // group), 2, ins, specs,
                   [(i, q, k_terms + i, q, 0) for i in range(k_terms) for q in range(group)], NN, [(tm, tn)], [S((n, d), F32)],
                   [pl.BlockSpec((tm, tn), lambda i, j, s: (i, j))], epilogue, after)[0]


def _dact_bwd(dh, wd_all, gate, up):
    n, d = dh.shape
    _, fs, _ = wd_all.shape
    tm = _tile(n, ROW_TILE)
    rows = _tile(tm, 256)

    def epilogue(acc_refs, in_refs, out_refs):
        for r0 in range(0, tm, rows):
            da = acc_refs[0][pl.ds(r0, rows), :]
            g = in_refs[2][pl.ds(r0, rows), :].astype(F32)
            u = in_refs[3][pl.ds(r0, rows), :].astype(F32)
            sg = jax.nn.sigmoid(g)
            out_refs[0][pl.ds(r0, rows), :] = (da * u * (sg * (1.0 + g * (1.0 - sg)))).astype(BF16)
            out_refs[1][pl.ds(r0, rows), :] = (da * (g * sg)).astype(BF16)

    t_spec = pl.BlockSpec((None, tm, fs), lambda j, m, k: (j, m, 0))
    o_shape = S((N_DEV, n, fs), BF16)
    return _matmul("dact_bwd", (N_DEV, n // tm, 1), 2, (dh, wd_all, gate, up),
                   [pl.BlockSpec((tm, d), lambda j, m, k: (m, 0)), pl.BlockSpec((None, fs, d), lambda j, m, k: (j, 0, 0)), t_spec, t_spec],
                   [(0, None, 1, None, 0)], NT, [(tm, fs)], [o_shape] * 2, [t_spec] * 2, epilogue)


def _dw_rows_bwd(name, act, dh, after=None):
    _, n, fs = act.shape
    d = dh.shape[1]
    tn = _tile(d, 512)
    return _matmul(name, (N_DEV, d // tn, 1), 2, (act, dh),
                   [pl.BlockSpec((None, n, fs), lambda j, c, t: (j, 0, 0)), pl.BlockSpec((n, tn), lambda j, c, t: (0, c))],
                   [(0, None, 1, None, 0)], TN, [(fs, tn)], [S((N_DEV, fs, d), BF16)], [pl.BlockSpec((None, fs, tn), lambda j, c, t: (j, 0, c))],
                   _store_epilogue, after)[0]


def _dwin_bwd(hn, dproj):
    n, d = hn.shape
    ws = dproj.shape[1] // N_DEV
    tk = _tile(d, 512)
    return _matmul("dwin_bwd", (N_DEV, d // tk, 1), 2, (hn, dproj),
                   [pl.BlockSpec((n, tk), lambda j, c, t: (0, c)), pl.BlockSpec((n, ws), lambda j, c, t: (0, j))],
                   [(0, None, 1, None, 0)], TN, [(tk, ws)], [S((N_DEV, d, ws), BF16)], [pl.BlockSpec((None, tk, ws), lambda j, c, t: (j, c, 0))],
                   _store_epilogue)[0]


def _dhn_from_shards(name, dys, ws_all, dy_is_flat, after=None):
    if dy_is_flat:
        n, ws = dys[0].shape[0], dys[0].shape[1] // N_DEV
    else:
        _, n, ws = dys[0].shape
    d = ws_all[0].shape[1]
    k_terms = len(dys)
    group = 4 // k_terms
    tm, tn = _tile(n, ROW_TILE), _tile(d, 1024)
    if dy_is_flat:
        y_spec = pl.BlockSpec((tm, group * ws), lambda i, j, s: (i, s))
        y_sl = lambda q: (slice(None), slice(q * ws, (q + 1) * ws))
    else:
        y_spec = pl.BlockSpec((group, tm, ws), lambda i, j, s: (s, i, 0))
        y_sl = lambda q: q
    w_spec = pl.BlockSpec((group, tn, ws), lambda i, j, s: (s, j, 0))
    return _matmul(name, (n // tm, d // tn, N_DEV // group), 2, (*dys, *ws_all), [y_spec] * k_terms + [w_spec] * k_terms,
                   [(i, y_sl(q), k_terms + i, q, 0) for i in range(k_terms) for q in range(group)], NT, [(tm, tn)], [S((n, d), F32)],
                   [pl.BlockSpec((tm, tn), lambda i, j, s: (i, j))], _store_epilogue, after)[0]


def _rms_fwd(name, x, g):
    n, d = x.shape
    tm = _tile(n, 256)

    def body(x_ref, g_ref, o_ref):
        xv = x_ref[...]
        r = lax.rsqrt(jnp.mean(xv * xv, axis=-1, keepdims=True) + RMS_EPS)
        o_ref[...] = (xv * r * g_ref[...]).astype(BF16)

    return pl.pallas_call(body, name=name, grid=(n // tm,),
                          in_specs=[pl.BlockSpec((tm, d), lambda i: (i, 0)), pl.BlockSpec((1, d), lambda i: (0, 0))],
                          out_specs=pl.BlockSpec((tm, d), lambda i: (i, 0)), out_shape=S((n, d), BF16), compiler_params=_cp(1))(x, g)


def _rms_bwd(name, x, g, dy, res):
    n, d = x.shape
    tm = _tile(n, 256)

    def body(x_ref, g_ref, dy_ref, res_ref, dx_ref, dxb_ref, dg_ref):
        xv, dyv = x_ref[...], dy_ref[...]
        r = lax.rsqrt(jnp.mean(xv * xv, axis=-1, keepdims=True) + RMS_EPS)
        xr = xv * r
        dgy = dyv * g_ref[...]
        dx = res_ref[...] + r * (dgy - xr * jnp.mean(dgy * xr, axis=-1, keepdims=True))
        dx_ref[...] = dx
        dxb_ref[...] = dx.astype(BF16)

        @pl.when(pl.program_id(0) == 0)
        def _():
            dg_ref[...] = jnp.zeros_like(dg_ref)

        dg_ref[...] += jnp.sum(dyv * xr, axis=0, keepdims=True)

    row = pl.BlockSpec((tm, d), lambda i: (i, 0))
    vec = pl.BlockSpec((1, d), lambda i: (0, 0))
    return pl.pallas_call(body, name=name, grid=(n // tm,), in_specs=[row, vec, row, row], out_specs=[row, row, vec],
                          out_shape=[S((n, d), F32), S((n, d), BF16), S((1, d), F32)], compiler_params=_cp(1))(x, g, dy, res)


def _final_loss(h, g, target):
    n, d = h.shape
    tm = _tile(n, 256)

    def body(h_ref, g_ref, t_ref, dh_ref, dhb_ref, dg_ref, sse_ref):
        hv, gv = h_ref[...], g_ref[...]
        r = lax.rsqrt(jnp.mean(hv * hv, axis=-1, keepdims=True) + RMS_EPS)
        hr = hv * r
        err = hr * gv - t_ref[...]
        dy = err * (1.0 / d)
        dgy = dy * gv
        dh = r * (dgy - hr * jnp.mean(dgy * hr, axis=-1, keepdims=True))
        dh_ref[...] = dh
        dhb_ref[...] = dh.astype(BF16)

        @pl.when(pl.program_id(0) == 0)
        def _():
            dg_ref[...] = jnp.zeros_like(dg_ref)
            sse_ref[...] = jnp.zeros_like(sse_ref)

        dg_ref[...] += jnp.sum(dy * hr, axis=0, keepdims=True)
        sse_ref[...] += jnp.sum(err * err)

    row = pl.BlockSpec((tm, d), lambda i: (i, 0))
    vec = pl.BlockSpec((1, d), lambda i: (0, 0))
    one = pl.BlockSpec((8, HEAD_DIM), lambda i: (0, 0))
    return pl.pallas_call(body, name="final_loss", grid=(n // tm,), in_specs=[row, vec, row], out_specs=[row, row, vec, one],
                          out_shape=[S((n, d), F32), S((n, d), BF16), S((1, d), F32), S((8, HEAD_DIM), F32)],
                          compiler_params=_cp(1))(h, g, target)


def _dot(a, b, dims):
    return lax.dot_general(a, b, dims, preferred_element_type=F32)


def _split3(x):
    hi = x.astype(BF16)
    r1 = x - hi.astype(F32)
    mid = r1.astype(BF16)
    lo = (r1 - mid.astype(F32)).astype(BF16)
    return hi, mid, lo


def _scan_cols(x, tri, terms):
    parts = _split3(x)[:terms]
    out = _dot(parts[0], tri, NN)
    for p in parts[1:]:
        out = out + _dot(p, tri, NN)
    return out


def _head_norm_fwd(o, g):
    r = lax.rsqrt(jnp.mean(o * o, axis=-1, keepdims=True) + RMS_EPS)
    return o * r * g


def _head_norm_bwd(o, g, dy):
    r = lax.rsqrt(jnp.mean(o * o, axis=-1, keepdims=True) + RMS_EPS)
    orr = o * r
    dgy = dy * g
    return r * (dgy - orr * jnp.mean(dgy * orr, axis=-1, keepdims=True)), dy * orr


def _interleave_plan(t):
    return [(i, dil, t // dil) for i, dil in enumerate(DILATIONS)]


def _band_mask(u, blocks_per_seq):
    row = lax.broadcasted_iota(jnp.int32, (BLK, 2 * BLK), 0)
    col = lax.broadcasted_iota(jnp.int32, (BLK, 2 * BLK), 1)
    dist = row + BLK - col
    has_prev = (u % blocks_per_seq) != 0
    return (dist >= 0) & (dist <= BLK) & ((col >= BLK) | has_prev)


def _dil_fwd(proj, g_dil, bl, t, n_heads):
    n = bl * t
    nb = t // BLK
    scale = HEAD_DIM ** -0.5
    plan = _interleave_plan(t)
    chunk = _tile(t, 256)

    def body(q_ref, k_ref, v_ref, g_ref, omix_ref, opre_ref, lse_ref, stg, qd, kd, vd, ob, lb, on, ln):
        for src, dst, pad in ((q_ref, qd, 0), (k_ref, kd, BLK), (v_ref, vd, BLK)):
            stg[...] = src[...].astype(F32)
            for bi, dil, sub in plan:
                if pad:
                    dst[bi, pl.ds(0, BLK), :] = jnp.zeros((BLK, HEAD_DIM), BF16)
                if dil == 1:
                    dst[bi, pl.ds(pad, t), :] = src[...]
                else:
                    for r in range(dil):
                        dst[bi, pl.ds(pad + r * sub, sub), :] = stg[pl.ds(r, sub, stride=dil), :].astype(BF16)

        for bi, dil, sub in plan:
            def blk(u, carry, bi=bi, sub=sub):
                rows = pl.ds(pl.multiple_of(u * BLK, BLK), BLK)
                win = pl.ds(pl.multiple_of(u * BLK, BLK), 2 * BLK)
                sc = _dot(qd[bi, rows, :], kd[bi, win, :], NT) * scale
                sc = jnp.where(_band_mask(u, sub // BLK), sc, NEG)
                m = jnp.max(sc, axis=-1, keepdims=True)
                p = jnp.exp(sc - m)
                den = jnp.sum(p, axis=-1, keepdims=True)
                ob[bi, rows, :] = _dot((p / den).astype(BF16), vd[bi, win, :], NN)
                lb[bi, rows, :] = jnp.broadcast_to(m + jnp.log(den), (BLK, HEAD_DIM))
                return carry
            lax.fori_loop(0, nb // DIL_UNROLL, lambda i, c, blk=blk: [blk(i * DIL_UNROLL + s, c) for s in range(DIL_UNROLL)][-1], 0)

        for bi, dil, sub in plan[1:]:
            for r in range(dil):
                on[bi - 1, pl.ds(r, sub, stride=dil), :] = ob[bi, pl.ds(r * sub, sub), :]
                ln[bi - 1, pl.ds(r, sub, stride=dil), :] = lb[bi, pl.ds(r * sub, sub), :]

        def merge(i, carry):
            rows = pl.ds(pl.multiple_of(i * chunk, chunk), chunk)
            l0, l1, l2 = lb[0, rows, :], ln[0, rows, :], ln[1, rows, :]
            mx = jnp.maximum(jnp.maximum(l0, l1), l2)
            w0, w1, w2 = jnp.exp(l0 - mx), jnp.exp(l1 - mx), jnp.exp(l2 - mx)
            tot = w0 + w1 + w2
            o = (w0 / tot) * ob[0, rows, :] + (w1 / tot) * on[0, rows, :] + (w2 / tot) * on[1, rows, :]
            lse_ref[rows, :] = mx + jnp.log(tot)
            opre_ref[rows, :] = o
            omix_ref[rows, :] = _head_norm_fwd(o, g_ref[...]).astype(BF16)
            return carry
        lax.fori_loop(0, t // chunk, merge, 0)

    hs = n_heads
    col = lambda off: pl.BlockSpec((t, HEAD_DIM), lambda b, h: (b, off + h))
    return pl.pallas_call(
        body, name="dil_fwd", grid=(bl, hs),
        in_specs=[col(0), col(hs), col(2 * hs), pl.BlockSpec((1, HEAD_DIM), lambda b, h: (0, h))],
        out_specs=[col(0), col(0), pl.BlockSpec((t, HEAD_DIM), lambda b, h: (b * hs + h, 0))],
        out_shape=[S((n, 2 * hs * HEAD_DIM), BF16), S((n, 2 * hs * HEAD_DIM), F32), S((bl * hs * t, HEAD_DIM), F32)],
        scratch_shapes=[pltpu.VMEM((t, HEAD_DIM), F32), pltpu.VMEM((3, t, HEAD_DIM), BF16),
                        pltpu.VMEM((3, t + BLK, HEAD_DIM), BF16), pltpu.VMEM((3, t + BLK, HEAD_DIM), BF16),
                        pltpu.VMEM((3, t, HEAD_DIM), F32), pltpu.VMEM((3, t, HEAD_DIM), F32),
                        pltpu.VMEM((2, t, HEAD_DIM), F32), pltpu.VMEM((2, t, HEAD_DIM), F32)],
        compiler_params=_cp(2),
    )(proj, proj, proj, g_dil)


def _dil_bwd(proj, opre, lse, d_omix, g_dil, rc, rs1, rs2, bl, t, n_heads):
    n = bl * t
    nb = t // BLK
    scale = HEAD_DIM ** -0.5
    plan = _interleave_plan(t)
    chunk = _tile(t, 256)

    def body(q_ref, k_ref, v_ref, opre_ref, lse_ref, dy_ref, g_ref, c_ref, s1_ref, s2_ref, out_ref, dg_ref,
             stg, qd, kd, vd, dod, ldd, dqd, dkc, dkp, dvc, dvp, sk, sv):
        which = pl.program_id(2)

        @pl.when(jnp.logical_and(which == 0, pl.program_id(1) == 0))
        def _():
            dg_ref[...] = jnp.zeros_like(dg_ref)

        @pl.when(which == 0)
        def _():
            def prep(i, dg):
                rows = pl.ds(pl.multiple_of(i * chunk, chunk), chunk)
                o = opre_ref[rows, :]
                d_o, dg_rows = _head_norm_bwd(o, g_ref[...], dy_ref[rows, :])
                stg[rows, :] = d_o
                lane = lax.broadcasted_iota(jnp.int32, (chunk, HEAD_DIM), 1)
                ldd[0, rows, :] = jnp.where(lane < HEAD_DIM // 2, lse_ref[rows, :], jnp.sum(d_o * o, axis=-1, keepdims=True))
                return dg + jnp.sum(dg_rows, axis=0, keepdims=True)
            dg_ref[...] += lax.fori_loop(0, t // chunk, prep, jnp.zeros((1, HEAD_DIM), F32))

            dod[0] = stg[...].astype(BF16)
            for bi, dil, sub in plan[1:]:
                for r in range(dil):
                    dst = pl.ds(r * sub, sub)
                    dod[bi, dst, :] = stg[pl.ds(r, sub, stride=dil), :].astype(BF16)
                    ldd[bi, dst, :] = ldd[0, pl.ds(r, sub, stride=dil), :]
            for src, dst, pad in ((q_ref, qd, 0), (k_ref, kd, BLK), (v_ref, vd, BLK)):
                stg[...] = src[...].astype(F32)
                for bi, dil, sub in plan:
                    if pad:
                        dst[bi, pl.ds(0, BLK), :] = jnp.zeros((BLK, HEAD_DIM), BF16)
                    if dil == 1:
                        dst[bi, pl.ds(pad, t), :] = src[...]
                    else:
                        for r in range(dil):
                            dst[bi, pl.ds(pad + r * sub, sub), :] = stg[pl.ds(r, sub, stride=dil), :].astype(BF16)

            for bi, dil, sub in plan:
                def blk(u, carry, bi=bi, sub=sub):
                    rows = pl.ds(pl.multiple_of(u * BLK, BLK), BLK)
                    win = pl.ds(pl.multiple_of(u * BLK, BLK), 2 * BLK)
                    qb, kw, vw, dob = qd[bi, rows, :], kd[bi, win, :], vd[bi, win, :], dod[bi, rows, :]
                    sc = _dot(qb, kw, NT) * scale
                    stats = ldd[bi, rows, :]
                    p = jnp.where(_band_mask(u, sub // BLK), jnp.exp(sc - stats[:, :1]), 0.0)
                    dp = _dot(dob, vw, NT)
                    ds = (p * (dp - stats[:, HEAD_DIM // 2:HEAD_DIM // 2 + 1]) * scale).astype(BF16)
                    dqd[bi, rows, :] = _dot(ds, kw, NN)
                    dk_win = _dot(ds, qb, TN)
                    dv_win = _dot(p.astype(BF16), dob, TN)
                    dkp[bi, rows, :] = dk_win[:BLK]
                    dkc[bi, rows, :] = dk_win[BLK:]
                    dvp[bi, rows, :] = dv_win[:BLK]
                    dvc[bi, rows, :] = dv_win[BLK:]
                    return carry
                lax.fori_loop(0, nb // DIL_UNROLL, lambda i, c, blk=blk: [blk(i * DIL_UNROLL + s, c) for s in range(DIL_UNROLL)][-1], 0)

            for cur, prev, undo_rope, dst in ((dqd, None, True, out_ref), (dkc, dkp, True, sk), (dvc, dvp, False, sv)):
                def summed(bi, start, size, cur=cur, prev=prev):
                    v = cur[bi, pl.ds(start, size), :]
                    if prev is None:
                        return v
                    if start + size < t:
                        return v + prev[bi, pl.ds(start + BLK, size), :]
                    if size == BLK:
                        return v
                    return v + jnp.concatenate([prev[bi, pl.ds(start + BLK, size - BLK), :], jnp.zeros((BLK, HEAD_DIM), F32)], axis=0)
                stg[...] = summed(0, 0, t)
                for bi, dil, sub in plan[1:]:
                    for r in range(dil):
                        stg[pl.ds(r, sub, stride=dil), :] += summed(bi, r * sub, sub)
                if undo_rope:
                    dst[...] = _rope_bwd(stg[...], c_ref[...], s1_ref[...], s2_ref[...]).astype(BF16)
                else:
                    dst[...] = stg[...].astype(BF16)

        @pl.when(which == 1)
        def _():
            out_ref[...] = sk[...]

        @pl.when(which == 2)
        def _():
            out_ref[...] = sv[...]

    hs = n_heads
    col = lambda off: pl.BlockSpec((t, HEAD_DIM), lambda h, b, w: (b, off + h))
    per_head = pl.BlockSpec((t, HEAD_DIM), lambda h, b, w: (b * hs + h, 0))
    tab = pl.BlockSpec((t, HEAD_DIM), lambda h, b, w: (b, 0))
    gvec = pl.BlockSpec((1, HEAD_DIM), lambda h, b, w: (0, h))
    tb = (t, HEAD_DIM)
    tp = (t + BLK, HEAD_DIM)
    return pl.pallas_call(
        body, name="dil_bwd", grid=(hs, bl, 3),
        in_specs=[col(0), col(hs), col(2 * hs), col(0), per_head, col(0), gvec, tab, tab, tab],
        out_specs=[pl.BlockSpec((t, HEAD_DIM), lambda h, b, w: (b, w * hs + h)), gvec],
        out_shape=[S((n, 6 * hs * HEAD_DIM), BF16), S((1, hs * HEAD_DIM), F32)],
        scratch_shapes=[pltpu.VMEM(tb, F32), pltpu.VMEM((3,) + tb, BF16), pltpu.VMEM((3,) + tp, BF16), pltpu.VMEM((3,) + tp, BF16),
                        pltpu.VMEM((3,) + tb, BF16), pltpu.VMEM((3,) + tb, F32), pltpu.VMEM((3,) + tb, F32),
                        pltpu.VMEM((3,) + tb, F32), pltpu.VMEM((3,) + tb, F32), pltpu.VMEM((3,) + tb, F32), pltpu.VMEM((3,) + tb, F32),
                        pltpu.VMEM(tb, BF16), pltpu.VMEM(tb, BF16)],
        compiler_params=_cp(3),
    )(proj, proj, proj, opre, lse, d_omix, g_dil, rc, rs1, rs2)


def _sb_tile(qb, kb, scale, shift):
    z = _dot(qb, kb, NT) * scale
    tl = jnp.log(1.0 + jnp.exp(-jnp.abs(z)))
    log_not = -(jnp.maximum(z, 0.0) + tl)
    log_beta = z + log_not
    strict = None
    if shift is not None:
        row = lax.broadcasted_iota(jnp.int32, (SBQ, SBK), 0)
        col = lax.broadcasted_iota(jnp.int32, (SBQ, SBK), 1)
        strict = col + shift < row
        log_not = jnp.where(strict, log_not, 0.0)
    return log_not, log_beta, strict


def _tri(cmp):
    row = lax.broadcasted_iota(jnp.int32, (SBK, SBK), 0)
    col = lax.broadcasted_iota(jnp.int32, (SBK, SBK), 1)
    return jnp.where(cmp(row, col), 1.0, 0.0).astype(BF16)


SB_EDGE = tuple(range(0, SBQ, SBK))


def _sb_fwd(proj, g_sb, omix_in, opre_in, bl, t, n_heads):
    nb = t // SBQ
    per_q = SBQ // SBK
    scale = HEAD_DIM ** -0.5

    def body(q_ref, k_ref, v_ref, g_ref, _omix_in, _opre_in, omix_ref, opre_ref, lt_ref):
        later = _tri(lambda r, c: r > c)

        def q_block(qi, carry):
            rows = pl.ds(pl.multiple_of(qi * SBQ, SBQ), SBQ)
            qb = q_ref[rows, :]

            def tile(kj, st, shift):
                run, acc = st
                krows = pl.ds(pl.multiple_of(kj * SBK, SBK), SBK)
                log_not, log_beta, strict = _sb_tile(qb, k_ref[krows, :], scale, shift)
                a = jnp.exp(log_beta + _scan_cols(log_not, later, 2) + run)
                if shift is not None:
                    a = jnp.where(strict, a, 0.0)
                return run + jnp.sum(log_not, axis=-1, keepdims=True), acc + _dot(a.astype(BF16), v_ref[krows, :], NN)

            st = (jnp.zeros((SBQ, 1), F32), jnp.zeros((SBQ, HEAD_DIM), F32))
            for shift in reversed(SB_EDGE):
                st = tile(qi * per_q + shift // SBK, st, shift)
            run, acc = lax.fori_loop(0, qi * per_q, lambda it, st: tile(qi * per_q - 1 - it, st, None), st)
            lt_ref[rows, :] = jnp.broadcast_to(run, (SBQ, HEAD_DIM))
            opre_ref[rows, :] = acc
            omix_ref[rows, :] = _head_norm_fwd(acc, g_ref[...]).astype(BF16)
            return carry

        lax.fori_loop(0, nb, q_block, 0)

    hs = n_heads
    col = lambda off: pl.BlockSpec((t, HEAD_DIM), lambda b, h: (b, off + h))
    return pl.pallas_call(
        body, name="sb_fwd", grid=(bl, hs),
        in_specs=[col(3 * hs), col(4 * hs), col(5 * hs), pl.BlockSpec((1, HEAD_DIM), lambda b, h: (0, h)), HBM_SPEC, HBM_SPEC],
        out_specs=[col(hs), col(hs), pl.BlockSpec((t, HEAD_DIM), lambda b, h: (b * hs + h, 0))],
        out_shape=[S(omix_in.shape, BF16), S(opre_in.shape, F32), S((bl * hs * t, HEAD_DIM), F32)],
        input_output_aliases={4: 0, 5: 1}, compiler_params=_cp(2),
    )(proj, proj, proj, g_sb, omix_in, opre_in)


def _sb_bwd(proj, opre, ltot, d_omix, g_sb, dproj_in, bl, t, n_heads):
    nb = t // SBQ
    per_q = SBQ // SBK
    scale = HEAD_DIM ** -0.5

    def body(q_ref, k_ref, v_ref, opre_ref, lt_ref, dy_ref, g_ref, _dproj_in, out_ref, dg_ref, dq, dk, dv):
        which = pl.program_id(2)

        @pl.when(jnp.logical_and(which == 0, pl.program_id(1) == 0))
        def _():
            dg_ref[...] = jnp.zeros_like(dg_ref)

        @pl.when(which == 0)
        def _():
            upto = _tri(lambda r, c: r <= c)
            before = _tri(lambda r, c: r < c)
            dk[...] = jnp.zeros_like(dk)
            dv[...] = jnp.zeros_like(dv)

            def q_block(qi, dg):
                rows = pl.ds(pl.multiple_of(qi * SBQ, SBQ), SBQ)
                qb = q_ref[rows, :]
                o = opre_ref[rows, :]
                d_o, dg_rows = _head_norm_bwd(o, g_ref[...], dy_ref[rows, :])
                dob = d_o.astype(BF16)
                lt = lt_ref[rows, :][:, :1]

                def tile(kj, st, shift):
                    run, grun, dq_acc = st
                    krows = pl.ds(pl.multiple_of(kj * SBK, SBK), SBK)
                    kb, vb = k_ref[krows, :], v_ref[krows, :]
                    log_not, log_beta, strict = _sb_tile(qb, kb, scale, shift)
                    excl = lt - (run + _scan_cols(log_not, upto, 2))
                    a = jnp.exp(log_beta + excl)
                    if shift is not None:
                        a = jnp.where(strict, a, 0.0)
                    g_a = a * _dot(dob, vb, NT)
                    g_before = grun + _scan_cols(g_a, before, 1)
                    dz = g_a - (g_a + g_before) * jnp.exp(log_beta)
                    if shift is not None:
                        dz = jnp.where(strict, dz, 0.0)
                    dzb = dz.astype(BF16)
                    dk[krows, :] += _dot(dzb, qb, TN)
                    dv[krows, :] += _dot(a.astype(BF16), dob, TN)
                    return (run + jnp.sum(log_not, axis=-1, keepdims=True), grun + jnp.sum(g_a, axis=-1, keepdims=True),
                            dq_acc + _dot(dzb, kb, NN))

                zero_col = jnp.zeros((SBQ, 1), F32)
                st = lax.fori_loop(0, qi * per_q, lambda kj, st: tile(kj, st, None),
                                   (zero_col, zero_col, jnp.zeros((SBQ, HEAD_DIM), F32)))
                for shift in SB_EDGE:
                    st = tile(qi * per_q + shift // SBK, st, shift)
                dq_acc = st[2]
                dq[rows, :] = (dq_acc * scale).astype(BF16)
                return dg + jnp.sum(dg_rows, axis=0, keepdims=True)

            dg_ref[...] += lax.fori_loop(0, nb, q_block, jnp.zeros((1, HEAD_DIM), F32))
            out_ref[...] = dq[...]

        @pl.when(which == 1)
        def _():
            out_ref[...] = (dk[...] * scale).astype(BF16)

        @pl.when(which == 2)
        def _():
            out_ref[...] = dv[...].astype(BF16)

    hs = n_heads
    col = lambda off: pl.BlockSpec((t, HEAD_DIM), lambda h, b, w: (b, off + h))
    per_head = pl.BlockSpec((t, HEAD_DIM), lambda h, b, w: (b * hs + h, 0))
    gvec = pl.BlockSpec((1, HEAD_DIM), lambda h, b, w: (0, h))
    tb = (t, HEAD_DIM)
    return pl.pallas_call(
        body, name="sb_bwd", grid=(hs, bl, 3),
        in_specs=[col(3 * hs), col(4 * hs), col(5 * hs), col(hs), per_head, col(hs), gvec, HBM_SPEC],
        out_specs=[pl.BlockSpec((t, HEAD_DIM), lambda h, b, w: (b, (3 + w) * hs + h)), gvec],
        out_shape=[S(dproj_in.shape, BF16), S((1, hs * HEAD_DIM), F32)],
        scratch_shapes=[pltpu.VMEM(tb, BF16), pltpu.VMEM(tb, F32), pltpu.VMEM(tb, F32)],
        input_output_aliases={7: 0}, compiler_params=_cp(3),
    )(proj, proj, proj, opre, ltot, d_omix, g_sb, dproj_in)


def _all_gather(shards):
    k_w = len(shards)

    def body(*refs):
        ins, outs = refs[:k_w], refs[k_w:2 * k_w]
        send_sems, recv_sems, local_sems = refs[2 * k_w:]
        x, y, c = lax.axis_index("x"), lax.axis_index("y"), lax.axis_index("c")
        me, sibling = (x, y, c), (x, y, 1 - c)
        chips = [(1 - x, y), (x, 1 - y), (1 - x, 1 - y)]

        def slot(dev):
            return 4 * dev[0] + 2 * dev[1] + dev[2]

        def copy(w, k, block, to, src=None):
            dst = outs[w].at[slot(block)]
            return pltpu.make_async_remote_copy(
                src_ref=dst if src is None else src, dst_ref=dst, send_sem=send_sems.at[w * 7 + k], recv_sem=recv_sems.at[w * 7 + k],
                device_id=to, device_id_type=MESH)

        mine = [pltpu.make_async_copy(ins[w], outs[w].at[slot(me)], local_sems.at[w]) for w in range(k_w)]
        first = []
        for w in range(k_w):
            mine[w].start()
            first.append(copy(w, 0, me, sibling, src=ins[w]))
            first += [copy(w, 1 + j, me, (*chip, c), src=ins[w]) for j, chip in enumerate(chips)]
        for cp in first:
            cp.start()
        passed = []
        for w in range(k_w):
            for j, chip in enumerate(chips):
                copy(w, 1 + j, (*chip, c), me).wait_recv()
                fwd = copy(w, 4 + j, (*chip, c), sibling)
                fwd.start()
                passed.append(fwd)
        for w in range(k_w):
            copy(w, 0, sibling, me).wait_recv()
            for j, chip in enumerate(chips):
                copy(w, 4 + j, (*chip, 1 - c), me).wait_recv()
        for cp in first + passed:
            cp.wait_send()
        for cp in mine:
            cp.wait()

    return pl.pallas_call(
        body, name="weights_all_gather", in_specs=[HBM_SPEC] * k_w, out_specs=[HBM_SPEC] * k_w,
        out_shape=[S((N_DEV,) + s.shape, s.dtype) for s in shards],
        scratch_shapes=[pltpu.SemaphoreType.DMA((7 * k_w,)), pltpu.SemaphoreType.DMA((7 * k_w,)), pltpu.SemaphoreType.DMA((k_w,))],
    )(*shards)


def _grad_exchange(grads, after):
    k_w = len(grads)

    def body(*refs):
        ins, outs = refs[:k_w], refs[k_w + 1:2 * k_w + 1]
        send_sems, recv_sems, local_sems = refs[2 * k_w + 1:]
        local, remote = _exchange_copies(ins, outs, send_sems, recv_sems, local_sems, False, ALL_PEERS)
        for cp in local + remote:
            cp.start()
        for cp in remote:
            cp.wait_send()
            cp.wait_recv()
        for cp in local:
            cp.wait()

    n_sem = (N_DEV - 1) * k_w
    return pl.pallas_call(
        body, name="grad_exchange", in_specs=[HBM_SPEC] * k_w + [pl.BlockSpec(memory_space=pl.ANY)], out_specs=[HBM_SPEC] * k_w,
        out_shape=[S(g.shape, g.dtype) for g in grads],
        scratch_shapes=[pltpu.SemaphoreType.DMA((n_sem,)), pltpu.SemaphoreType.DMA((n_sem,)), pltpu.SemaphoreType.DMA((k_w,))],
    )(*grads, after)


def _peer(x, y, c, k):
    px, py, pc = x ^ (k >> 2), y ^ ((k >> 1) & 1), c ^ (k & 1)
    return (px, py, pc), 4 * px + 2 * py + pc


ALL_PEERS = tuple(range(1, N_DEV))
SAME_CORE_PEERS = (2, 4, 6)


def _exchange_copies(srcs, lands, send_sems, recv_sems, local_sems, gather, peers):
    x, y, c = lax.axis_index("x"), lax.axis_index("y"), lax.axis_index("c")
    my_slot = 4 * x + 2 * y + c
    local, remote = [], []
    for w, (src, land) in enumerate(zip(srcs, lands)):
        local.append(pltpu.make_async_copy(src if gather else src.at[my_slot], land.at[my_slot], local_sems.at[w]))
        for i, k in enumerate(peers):
            peer, peer_slot = _peer(x, y, c, k)
            remote.append(pltpu.make_async_remote_copy(
                src_ref=src if gather else src.at[peer_slot], dst_ref=land.at[my_slot],
                send_sem=send_sems.at[w * len(peers) + i], recv_sem=recv_sems.at[w * len(peers) + i],
                device_id=peer, device_id_type=MESH))
    return local, remote


def _exchange_start(name, srcs, gather, after, peers=ALL_PEERS):
    k_w = len(srcs)
    land_shapes = [((N_DEV,) + s.shape) if gather else s.shape for s in srcs]

    def body(*refs):
        src_refs, land_refs = refs[:k_w], refs[k_w:2 * k_w]
        send_sems, recv_sems, local_sems = refs[2 * k_w + 1:2 * k_w + 4]
        token = refs[-1]
        local, remote = _exchange_copies(src_refs, land_refs, send_sems, recv_sems, local_sems, gather, peers)
        for cp in local + remote:
            cp.start()
        token[...] = jnp.zeros_like(token)

    n_sem = len(peers) * k_w
    hbm = lambda a: pltpu.with_memory_space_constraint(a, pltpu.HBM)
    outs = pl.pallas_call(
        body, name=name,
        in_specs=[HBM_SPEC] * (2 * k_w) + [pl.BlockSpec(memory_space=pl.ANY)],
        out_shape=(pltpu.SemaphoreType.DMA((n_sem,)), pltpu.SemaphoreType.DMA((n_sem,)), pltpu.SemaphoreType.DMA((k_w,)),
                   *[pltpu.HBM(s.shape, s.dtype) for s in srcs], *[pltpu.HBM(ls, s.dtype) for ls, s in zip(land_shapes, srcs)],
                   S((8, HEAD_DIM), F32)),
        out_specs=(SEM_SPEC, SEM_SPEC, SEM_SPEC, *[HBM_SPEC] * (2 * k_w), pl.BlockSpec(memory_space=pltpu.VMEM)),
        input_output_aliases={i: 3 + i for i in range(2 * k_w)},
        compiler_params=pltpu.CompilerParams(has_side_effects=SIDE_EFFECT),
    )(*[hbm(s) for s in srcs], *[hbm(lax.empty(ls, s.dtype)) for ls, s in zip(land_shapes, srcs)], after)
    return dict(sems=outs[:3], srcs=outs[3:3 + k_w], lands=outs[3 + k_w:3 + 2 * k_w], token_block=outs[-1], token=outs[-1][0, 0], gather=gather, peers=peers)


def _exchange_wait(name, handle, after):
    k_w = len(handle["srcs"])
    gather = handle["gather"]

    def body(*refs):
        src_refs, land_refs = refs[:k_w], refs[k_w:2 * k_w]
        send_sems, recv_sems, local_sems = refs[2 * k_w:2 * k_w + 3]
        local, remote = _exchange_copies(src_refs, land_refs, send_sems, recv_sems, local_sems, gather, handle["peers"])
        for cp in local:
            cp.wait()
        for cp in remote:
            cp.wait_send()
            cp.wait_recv()

    outs = pl.pallas_call(
        body, name=name,
        in_specs=[HBM_SPEC] * (2 * k_w) + [SEM_SPEC] * 3 + [pl.BlockSpec(memory_space=pl.ANY)],
        out_shape=tuple(pltpu.HBM(a.shape, a.dtype) for a in (*handle["srcs"], *handle["lands"])),
        out_specs=tuple([HBM_SPEC] * (2 * k_w)),
        input_output_aliases={i: i for i in range(2 * k_w)},
        compiler_params=pltpu.CompilerParams(has_side_effects=SIDE_EFFECT),
    )(*handle["srcs"], *handle["lands"], *handle["sems"], after)
    return outs[k_w:]


CHIPS = ((0, 0), (0, 1), (1, 0), (1, 1))


def _swap_copies(lands, send_sems, recv_sems):
    x, y, c = lax.axis_index("x"), lax.axis_index("y"), lax.axis_index("c")
    copies = []
    for w, land in enumerate(lands):
        for i, (px, py) in enumerate(CHIPS):
            mine = land.at[4 * px + 2 * py + c]
            copies.append(pltpu.make_async_remote_copy(
                src_ref=mine, dst_ref=mine, send_sem=send_sems.at[w * len(CHIPS) + i], recv_sem=recv_sems.at[w * len(CHIPS) + i],
                device_id=(x, y, 1 - c), device_id_type=MESH))
    return copies


def _swap_start(name, lands, after):
    k_w = len(lands)

    def body(*refs):
        land_refs = refs[:k_w]
        send_sems, recv_sems = refs[k_w + 1:k_w + 3]
        for cp in _swap_copies(land_refs, send_sems, recv_sems):
            cp.start()
        refs[-1][...] = jnp.zeros_like(refs[-1])

    n_sem = len(CHIPS) * k_w
    outs = pl.pallas_call(
        body, name=name, in_specs=[HBM_SPEC] * k_w + [pl.BlockSpec(memory_space=pl.ANY)],
        out_shape=(pltpu.SemaphoreType.DMA((n_sem,)), pltpu.SemaphoreType.DMA((n_sem,)),
                   *[pltpu.HBM(a.shape, a.dtype) for a in lands], S((8, HEAD_DIM), F32)),
        out_specs=(SEM_SPEC, SEM_SPEC, *[HBM_SPEC] * k_w, pl.BlockSpec(memory_space=pltpu.VMEM)),
        input_output_aliases={i: 2 + i for i in range(k_w)},
        compiler_params=pltpu.CompilerParams(has_side_effects=SIDE_EFFECT),
    )(*lands, after)
    return dict(sems=outs[:2], lands=outs[2:2 + k_w], token_block=outs[-1])


def _swap_wait(name, handle, after):
    k_w = len(handle["lands"])

    def body(*refs):
        for cp in _swap_copies(refs[:k_w], refs[k_w], refs[k_w + 1]):
            cp.wait_send()
            cp.wait_recv()

    return pl.pallas_call(
        body, name=name, in_specs=[HBM_SPEC] * k_w + [SEM_SPEC] * 2 + [pl.BlockSpec(memory_space=pl.ANY)],
        out_shape=tuple(pltpu.HBM(a.shape, a.dtype) for a in handle["lands"]), out_specs=tuple([HBM_SPEC] * k_w),
        input_output_aliases={i: i for i in range(k_w)},
        compiler_params=pltpu.CompilerParams(has_side_effects=SIDE_EFFECT),
    )(*handle["lands"], *handle["sems"], after)


def _adamw(name, parts, w, m, v):
    r, c = w.shape
    tr = _tile(r, max(16, (1 << 19) // c // 16 * 16))

    def body(p_ref, w_ref, m_ref, v_ref, g_ref, d_ref, nm_ref, nv_ref):
        g = p_ref[0].astype(F32)
        for s in range(1, N_DEV):
            g = g + p_ref[s].astype(F32)
        m_new = ADAM_B1 * m_ref[...] + (1.0 - ADAM_B1) * g
        v_new = ADAM_B2 * v_ref[...] + (1.0 - ADAM_B2) * jnp.square(g)
        m_hat = m_new / (1.0 - ADAM_B1 ** ADAM_STEP)
        v_hat = v_new / (1.0 - ADAM_B2 ** ADAM_STEP)
        g_ref[...] = g
        d_ref[...] = -ADAM_LR * (m_hat / (jnp.sqrt(v_hat) + ADAM_EPS) + ADAM_WD * w_ref[...])
        nm_ref[...] = m_new
        nv_ref[...] = v_new

    blk = pl.BlockSpec((tr, c), lambda i: (i, 0))
    return pl.pallas_call(body, name=name, grid=(r // tr,), in_specs=[pl.BlockSpec((N_DEV, tr, c), lambda i: (0, i, 0)), blk, blk, blk],
                          out_specs=[blk] * 4, out_shape=[S((r, c), F32)] * 4, compiler_params=_cp(1))(parts, w, m, v)


def _rope_tables(positions):
    inv_freq = jnp.power(jnp.float32(ROPE_THETA), -jnp.arange(ROPE_HALF, dtype=F32) / ROPE_HALF)
    ang = positions.astype(F32).reshape(-1, 1) * inv_freq
    cos, sin = jnp.cos(ang), jnp.sin(ang)
    n = ang.shape[0]
    rest = HEAD_DIM - 2 * ROPE_HALF
    zeros = jnp.zeros((n, ROPE_HALF), F32)
    c = jnp.concatenate([cos, cos, jnp.ones((n, rest), F32)], axis=1)
    s1 = jnp.concatenate([zeros, sin, jnp.zeros((n, rest), F32)], axis=1)
    s2 = jnp.concatenate([-sin, zeros, jnp.zeros((n, rest), F32)], axis=1)
    return c, s1, s2


def kernel(x, positions, norm_mix_g, w_in, norm_out_dil_g, norm_out_sb_g, w_out, norm_ffn_g, w_gate, w_up, w_down, norm_final_g, loss_target, m_norm_mix_g, m_w_in, m_norm_out_dil_g, m_norm_out_sb_g, m_w_out, m_norm_ffn_g, m_w_gate, m_w_up, m_w_down, m_norm_final_g, v_norm_mix_g, v_w_in, v_norm_out_dil_g, v_norm_out_sb_g, v_w_out, v_norm_ffn_g, v_w_gate, v_w_up, v_w_down, v_norm_final_g):
    bl, t, d = x.shape
    n = bl * t
    hs = d // (2 * HEAD_DIM)
    x2 = x.reshape(n, d)
    target = loss_target.reshape(n, d)
    g_final = norm_final_g.reshape(1, d)
    rc, rs1, rs2 = _rope_tables(positions)

    (win_all,) = _all_gather([w_in[0].astype(BF16)])
    ex_wout = _exchange_start("wout_gather_start", [w_out[0].astype(BF16)], True, win_all, SAME_CORE_PEERS)
    tr = lambda a: jnp.swapaxes(a[0], 0, 1)
    ex_wgu = _exchange_start("wgu_gather_start", [tr(w_gate).astype(BF16), tr(w_up).astype(BF16)], True, ex_wout["token_block"], SAME_CORE_PEERS)
    ex_wd = _exchange_start("wd_gather_start", [w_down[0].astype(BF16)], True, ex_wgu["token_block"], SAME_CORE_PEERS)
    rc = rc + ex_wd["token"]

    hn1 = _rms_fwd("rms_mix_fwd", x2, norm_mix_g)
    proj = _proj_fwd(hn1, win_all, rc, rs1, rs2, 2 * hs)
    omix, opre, lse = _dil_fwd(proj, norm_out_dil_g, bl, t, hs)
    omix, opre, ltot = _sb_fwd(proj, norm_out_sb_g, omix, opre, bl, t, hs)
    (wout_half,) = _exchange_wait("wout_gather_wait", ex_wout, ltot)
    wg_half, wu_half = _exchange_wait("wgu_gather_wait", ex_wgu, wout_half)
    sw_wout = _swap_start("wout_swap_start", [wout_half], wg_half)
    sw_wgu = _swap_start("wgu_swap_start", [wg_half, wu_half], sw_wout["token_block"])
    (wout_all,) = _swap_wait("wout_swap_wait", sw_wout, sw_wgu["token_block"])
    wout_full = wout_all.reshape(d, d)
    h1 = _dense_res("out_fwd", omix, wout_full, x2, NN)
    hn2 = _rms_fwd("rms_ffn_fwd", h1, norm_ffn_g)
    wg_all, wu_all = _swap_wait("wgu_swap_wait", sw_wgu, hn2)
    (wd_half,) = _exchange_wait("wd_gather_wait", ex_wd, wg_all)
    sw_wd = _swap_start("wd_swap_start", [wd_half], wg_all)
    gate = _gate_fwd(hn2, wg_all, sw_wd["token_block"])
    up, act = _up_fwd(hn2, wu_all, gate)
    (wd_all,) = _swap_wait("wd_swap_wait", sw_wd, act)
    h2 = _rows_fwd("down_fwd", (act,), (wd_all,), h1)
    dh2, dh2b, dg_final, sse = _final_loss(h2, g_final, target)
    loss = lax.psum(sse[0, 0], ("x", "y", "c")) * (0.5 / d)

    dgate, dup = _dact_bwd(dh2b, wd_all, gate, up)
    dwd = _dw_rows_bwd("dwd_bwd", act, dh2b)
    ex_dwd = _exchange_start("dwd_exchange_start", [dwd], False, dgate)
    dwg = _dw_rows_bwd("dwg_bwd", dgate, hn2, ex_dwd["token_block"])
    dwu = _dw_rows_bwd("dwu_bwd", dup, hn2)
    ex_dwgu = _exchange_start("dwgu_exchange_start", [dwg, dwu], False, ex_dwd["token_block"])
    dhn2 = _rows_fwd("dhn2_bwd", (dgate, dup), (wg_all, wu_all), None, ex_dwgu["token_block"])
    dh1, dh1b, dg_ffn = _rms_bwd("rms_ffn_bwd", h1, norm_ffn_g, dhn2, dh2)
    d_omix = _dense_res("domix_bwd", dh1b, wout_full, None, NT)
    dwout = _tn_full("dwout_bwd", omix, dh1b).reshape(N_DEV, d // N_DEV, d)
    ex_dwout = _exchange_start("dwout_exchange_start", [dwout], False, d_omix)
    dproj, dg_dil = _dil_bwd(proj, opre, lse, d_omix, norm_out_dil_g + ex_dwout["token"], rc, rs1, rs2, bl, t, hs)
    dproj, dg_sb = _sb_bwd(proj, opre, ltot, d_omix, norm_out_sb_g, dproj, bl, t, hs)
    dwin = _dwin_bwd(hn1, dproj)
    ex_dwin = _exchange_start("dwin_exchange_start", [dwin], False, dproj)
    dhn1 = _dhn_from_shards("dhn1_bwd", (dproj,), (win_all,), True, ex_dwin["token_block"])
    dx, _, dg_mix = _rms_bwd("rms_mix_bwd", x2, norm_mix_g, dhn1, dh1)

    gains = [norm_mix_g, norm_out_dil_g, norm_out_sb_g, norm_ffn_g, g_final]
    m_gains = [m_norm_mix_g, m_norm_out_dil_g, m_norm_out_sb_g, m_norm_ffn_g, m_norm_final_g.reshape(1, d)]
    v_gains = [v_norm_mix_g, v_norm_out_dil_g, v_norm_out_sb_g, v_norm_ffn_g, v_norm_final_g.reshape(1, d)]
    dg_vec = jnp.concatenate([dg_mix, dg_dil, dg_sb, dg_ffn, dg_final], axis=1)
    dg_all = jnp.broadcast_to(dg_vec[None], (N_DEV,) + dg_vec.shape)

    out_w = {}
    (rwd,) = _exchange_wait("dwd_exchange_wait", ex_dwd, dx)
    out_w["w_down"] = _adamw("adamw_w_down", rwd, w_down[0], m_w_down[0], v_w_down[0])
    rwg, rwu = _exchange_wait("dwgu_exchange_wait", ex_dwgu, out_w["w_down"][0])
    gate_t = _adamw("adamw_w_gate", rwg, tr(w_gate), tr(m_w_gate), tr(v_w_gate))
    up_t = _adamw("adamw_w_up", rwu, tr(w_up), tr(m_w_up), tr(v_w_up))
    out_w["w_gate"] = [jnp.swapaxes(o, 0, 1) for o in gate_t]
    out_w["w_up"] = [jnp.swapaxes(o, 0, 1) for o in up_t]
    (rwout,) = _exchange_wait("dwout_exchange_wait", ex_dwout, up_t[0])
    out_w["w_out"] = _adamw("adamw_w_out", rwout, w_out[0], m_w_out[0], v_w_out[0])
    (rwin,) = _exchange_wait("dwin_exchange_wait", ex_dwin, out_w["w_out"][0])
    out_w["w_in"] = _adamw("adamw_w_in", rwin, w_in[0], m_w_in[0], v_w_in[0])
    (rg,) = _grad_exchange([dg_all], out_w["w_in"][0])
    out_w = {name: [o[None] for o in outs] for name, outs in out_w.items()}
    cat = lambda vs: jnp.concatenate(vs, axis=1)
    gain_out = _adamw("adamw_gains", rg, cat(gains), cat(m_gains), cat(v_gains))
    widths = [d, d // 2, d // 2, d]
    cuts = [sum(widths[:i + 1]) for i in range(4)]
    gain_split = [jnp.split(o, cuts, axis=1) for o in gain_out]

    def ordered(kind):
        gs = gain_split[kind]
        return (gs[0], out_w["w_in"][kind], gs[1], gs[2], out_w["w_out"][kind], gs[3], out_w["w_gate"][kind],
                out_w["w_up"][kind], out_w["w_down"][kind], gs[4].reshape(d))

    return (loss, dx.reshape(bl, t, d), *ordered(0), *ordered(1), *ordered(2), *ordered(3))
```

```python
import functools
import math

import jax
import jax.numpy as jnp
from jax import lax
from jax.experimental import pallas as pl
from jax.experimental.pallas import tpu as pltpu

F32 = jnp.float32
BF16 = jnp.bfloat16
S = jax.ShapeDtypeStruct

N_DEV = 8
HEAD_DIM = 128
BLK = 128
SBQ = 1024
SBK = 256
DIL_UNROLL = 16
ROPE_HALF = 16
ROPE_THETA = 500000.0
RMS_EPS = 1e-5
DILATIONS = (1, 4, 16)
NEG = -1e30
VMEM_LIMIT = 56 * 1024 * 1024

ADAM_LR = 0.001
ADAM_B1 = 0.9
ADAM_B2 = 0.999
ADAM_EPS = 1e-08
ADAM_WD = 0.01
ADAM_STEP = 10

MESH = pl.DeviceIdType.MESH
HBM_SPEC = pl.BlockSpec(memory_space=pltpu.HBM)
SEM_SPEC = pl.BlockSpec(memory_space=pltpu.SEMAPHORE)
SIDE_EFFECT = pltpu.SideEffectType.DATAFLOW_SIDE_EFFECTING


def _cp(n_axes):
    return pltpu.CompilerParams(dimension_semantics=("arbitrary",) * n_axes, vmem_limit_bytes=VMEM_LIMIT)


def _tile(n, want):
    if n <= want:
        return n
    t = want
    while t >= 16:
        if n % t == 0 and t % 16 == 0:
            return t
        t -= 16
    return n


NN = (((1,), (0,)), ((), ()))
NT = (((1,), (1,)), ((), ()))
TN = (((0,), (0,)), ((), ()))
ROW_TILE = 512


def _matmul(name, grid, red_axis, ins, in_specs, terms, dims, acc_shapes, out_shapes, out_specs, epilogue, after=None):
    if after is not None:
        ins, in_specs = (*ins, after), [*in_specs, pl.BlockSpec(memory_space=pl.ANY)]
    n_in, n_out = len(ins), len(out_shapes)
    n_red = grid[red_axis]

    def body(*refs):
        in_refs, out_refs, acc_refs = refs[:n_in], refs[n_in:n_in + n_out], refs[n_in + n_out:]
        sums = {}
        for a_idx, a_sl, b_idx, b_sl, acc_idx in terms:
            a = (in_refs[a_idx][...] if a_sl is None else in_refs[a_idx][a_sl]).astype(BF16)
            b = (in_refs[b_idx][...] if b_sl is None else in_refs[b_idx][b_sl]).astype(BF16)
            prod = lax.dot_general(a, b, dims, preferred_element_type=F32)
            sums[acc_idx] = prod if acc_idx not in sums else sums[acc_idx] + prod
        if n_red == 1:
            for idx, v in sums.items():
                acc_refs[idx][...] = v
            epilogue(acc_refs, in_refs, out_refs)
            return
        k = pl.program_id(red_axis)

        @pl.when(k == 0)
        def _():
            for idx, v in sums.items():
                acc_refs[idx][...] = v

        @pl.when(k > 0)
        def _():
            for idx, v in sums.items():
                acc_refs[idx][...] += v

        @pl.when(k == n_red - 1)
        def _():
            epilogue(acc_refs, in_refs, out_refs)

    return pl.pallas_call(
        body, name=name, grid=grid, in_specs=in_specs, out_specs=out_specs, out_shape=out_shapes,
        scratch_shapes=[pltpu.VMEM(s, F32) for s in acc_shapes], compiler_params=_cp(len(grid)),
    )(*ins)


def _store_epilogue(acc_refs, in_refs, out_refs):
    for acc, out in zip(acc_refs, out_refs):
        out[...] = acc[...].astype(out.dtype)


def _rope_fwd(a, c, s1, s2):
    return a * c + pltpu.roll(a, ROPE_HALF, 1) * s1 + pltpu.roll(a, HEAD_DIM - ROPE_HALF, 1) * s2


def _rope_bwd(d, c, s1, s2):
    return d * c + pltpu.roll(d * s1, HEAD_DIM - ROPE_HALF, 1) + pltpu.roll(d * s2, ROPE_HALF, 1)


def _proj_fwd(hn, w_all, rc, rs1, rs2, n_rope_heads):
    n, d = hn.shape
    _, _, ws = w_all.shape
    tm = _tile(n, ROW_TILE)
    heads_per_shard = ws // HEAD_DIM
    rows = _tile(tm, 256)

    def epilogue(acc_refs, in_refs, out_refs):
        acc, out = acc_refs[0], out_refs[0]
        j = pl.program_id(0)
        for r0 in range(0, tm, rows):
            c, s1, s2 = (ref[pl.ds(r0, rows), :] for ref in in_refs[2:5])
            for hh in range(heads_per_shard):
                a = acc[pl.ds(r0, rows), pl.ds(hh * HEAD_DIM, HEAD_DIM)]
                roped = _rope_fwd(a, c, s1, s2)
                a = jnp.where(j * heads_per_shard + hh < n_rope_heads, roped, a)
                out[pl.ds(r0, rows), pl.ds(hh * HEAD_DIM, HEAD_DIM)] = a.astype(out.dtype)

    tab = pl.BlockSpec((tm, HEAD_DIM), lambda j, m, k: (m, 0))
    return _matmul(
        "proj_fwd", (N_DEV, n // tm, 1), 2, (hn, w_all, rc, rs1, rs2),
        [pl.BlockSpec((tm, d), lambda j, m, k: (m, 0)), pl.BlockSpec((None, d, ws), lambda j, m, k: (j, 0, 0)), tab, tab, tab],
        [(0, None, 1, None, 0)], NN, [(tm, ws)], [S((n, N_DEV * ws), BF16)], [pl.BlockSpec((tm, ws), lambda j, m, k: (m, j))], epilogue)[0]


def _dense_res(name, a, b, res, dims, out_dtype=F32):
    m, kdim = a.shape
    n = b.shape[1] if dims == NN else b.shape[0]
    tm, tn = _tile(m, ROW_TILE), _tile(n, 1024)
    ins = [a, b] + ([res] if res is not None else [])
    b_spec = pl.BlockSpec((kdim, tn), lambda j, i, k: (0, j)) if dims == NN else pl.BlockSpec((tn, kdim), lambda j, i, k: (j, 0))
    specs = [pl.BlockSpec((tm, kdim), lambda j, i, k: (i, 0)), b_spec]
    if res is not None:
        specs.append(pl.BlockSpec((tm, tn), lambda j, i, k: (i, j)))

    def epilogue(acc_refs, in_refs, out_refs):
        v = acc_refs[0][...]
        if res is not None:
            v = v + in_refs[2][...]
        out_refs[0][...] = v.astype(out_dtype)

    return _matmul(name, (n // tn, m // tm, 1), 2, ins, specs, [(0, None, 1, None, 0)], dims, [(tm, tn)],
                   [S((m, n), out_dtype)], [pl.BlockSpec((tm, tn), lambda j, i, k: (i, j))], epilogue)[0]


def _tn_full(name, a, b, out_dtype=BF16):
    m, kdim = a.shape
    n = b.shape[1]
    tk, tn = _tile(kdim, 512), _tile(n, 1024)
    return _matmul(name, (kdim // tk, n // tn, 1), 2, (a, b),
                   [pl.BlockSpec((m, tk), lambda i, j, t: (0, i)), pl.BlockSpec((m, tn), lambda i, j, t: (0, j))],
                   [(0, None, 1, None, 0)], TN, [(tk, tn)], [S((kdim, n), out_dtype)], [pl.BlockSpec((tk, tn), lambda i, j, t: (i, j))],
                   _store_epilogue)[0]


def _gate_fwd(hn, wgt_all, after=None):
    n, d = hn.shape
    _, fs, _ = wgt_all.shape
    tm = _tile(n, ROW_TILE)
    o_spec = pl.BlockSpec((None, tm, fs), lambda j, m, k: (j, m, 0))
    return _matmul("gate_fwd", (N_DEV, n // tm, 1), 2, (hn, wgt_all),
                   [pl.BlockSpec((tm, d), lambda j, m, k: (m, 0)), pl.BlockSpec((None, fs, d), lambda j, m, k: (j, 0, 0))],
                   [(0, None, 1, None, 0)], NT, [(tm, fs)], [S((N_DEV, n, fs), BF16)], [o_spec], _store_epilogue, after)[0]


def _up_fwd(hn, wut_all, gate):
    n, d = hn.shape
    _, fs, _ = wut_all.shape
    tm = _tile(n, ROW_TILE)
    rows = _tile(tm, 256)

    def epilogue(acc_refs, in_refs, out_refs):
        for r0 in range(0, tm, rows):
            u = acc_refs[0][pl.ds(r0, rows), :]
            g = in_refs[2][pl.ds(r0, rows), :].astype(F32)
            out_refs[0][pl.ds(r0, rows), :] = u.astype(BF16)
            out_refs[1][pl.ds(r0, rows), :] = (g * jax.nn.sigmoid(g) * u).astype(BF16)

    t_spec = pl.BlockSpec((None, tm, fs), lambda j, m, k: (j, m, 0))
    o_shape = S((N_DEV, n, fs), BF16)
    return _matmul("up_fwd", (N_DEV, n // tm, 1), 2, (hn, wut_all, gate),
                   [pl.BlockSpec((tm, d), lambda j, m, k: (m, 0)), pl.BlockSpec((None, fs, d), lambda j, m, k: (j, 0, 0)), t_spec],
                   [(0, None, 1, None, 0)], NT, [(tm, fs)], [o_shape] * 2, [t_spec] * 2, epilogue)


def _rows_fwd(name, ys, ws_all, res, after=None):
    _, n, fs = ys[0].shape
    d = ws_all[0].shape[2]
    k_terms = len(ys)
    group = 4 // k_terms
    tm, tn = _tile(n, ROW_TILE), _tile(d, 1024)
    ins = [*ys, *ws_all] + ([res] if res is not None else [])
    specs = ([pl.BlockSpec((group, tm, fs), lambda i, j, s: (s, i, 0))] * k_terms
             + [pl.BlockSpec((group, fs, tn), lambda i, j, s: (s, 0, j))] * k_terms)
    if res is not None:
        specs.append(pl.BlockSpec((tm, tn), lambda i, j, s: (i, j)))

    def epilogue(acc_refs, in_refs, out_refs):
        v = acc_refs[0][...]
        out_refs[0][...] = v if res is None else v + in_refs[2 * k_terms][...]

    return _matmul(name, (n // tm, d // tn, N_DEV // group), 2, ins, specs,
                   [(i, q, k_terms + i, q, 0) for i in range(k_terms) for q in range(group)], NN, [(tm, tn)], [S((n, d), F32)],
                   [pl.BlockSpec((tm, tn), lambda i, j, s: (i, j))], epilogue, after)[0]


def _dact_bwd(dh, wd_all, gate, up):
    n, d = dh.shape
    _, fs, _ = wd_all.shape
    tm = _tile(n, ROW_TILE)
    rows = _tile(tm, 256)

    def epilogue(acc_refs, in_refs, out_refs):
        for r0 in range(0, tm, rows):
            da = acc_refs[0][pl.ds(r0, rows), :]
            g = in_refs[2][pl.ds(r0, rows), :].astype(F32)
            u = in_refs[3][pl.ds(r0, rows), :].astype(F32)
            sg = jax.nn.sigmoid(g)
            out_refs[0][pl.ds(r0, rows), :] = (da * u * (sg * (1.0 + g * (1.0 - sg)))).astype(BF16)
            out_refs[1][pl.ds(r0, rows), :] = (da * (g * sg)).astype(BF16)

    t_spec = pl.BlockSpec((None, tm, fs), lambda j, m, k: (j, m, 0))
    o_shape = S((N_DEV, n, fs), BF16)
    return _matmul("dact_bwd", (N_DEV, n // tm, 1), 2, (dh, wd_all, gate, up),
                   [pl.BlockSpec((tm, d), lambda j, m, k: (m, 0)), pl.BlockSpec((None, fs, d), lambda j, m, k: (j, 0, 0)), t_spec, t_spec],
                   [(0, None, 1, None, 0)], NT, [(tm, fs)], [o_shape] * 2, [t_spec] * 2, epilogue)


def _dw_rows_bwd(name, act, dh, after=None):
    _, n, fs = act.shape
    d = dh.shape[1]
    tn = _tile(d, 512)
    return _matmul(name, (N_DEV, d // tn, 1), 2, (act, dh),
                   [pl.BlockSpec((None, n, fs), lambda j, c, t: (j, 0, 0)), pl.BlockSpec((n, tn), lambda j, c, t: (0, c))],
                   [(0, None, 1, None, 0)], TN, [(fs, tn)], [S((N_DEV, fs, d), BF16)], [pl.BlockSpec((None, fs, tn), lambda j, c, t: (j, 0, c))],
                   _store_epilogue, after)[0]


def _dwin_bwd(hn, dproj):
    n, d = hn.shape
    ws = dproj.shape[1] // N_DEV
    tk = _tile(d, 512)
    return _matmul("dwin_bwd", (N_DEV, d // tk, 1), 2, (hn, dproj),
                   [pl.BlockSpec((n, tk), lambda j, c, t: (0, c)), pl.BlockSpec((n, ws), lambda j, c, t: (0, j))],
                   [(0, None, 1, None, 0)], TN, [(tk, ws)], [S((N_DEV, d, ws), BF16)], [pl.BlockSpec((None, tk, ws), lambda j, c, t: (j, c, 0))],
                   _store_epilogue)[0]


def _dhn_from_shards(name, dys, ws_all, dy_is_flat, after=None):
    if dy_is_flat:
        n, ws = dys[0].shape[0], dys[0].shape[1] // N_DEV
    else:
        _, n, ws = dys[0].shape
    d = ws_all[0].shape[1]
    k_terms = len(dys)
    group = 4 // k_terms
    tm, tn = _tile(n, ROW_TILE), _tile(d, 1024)
    if dy_is_flat:
        y_spec = pl.BlockSpec((tm, group * ws), lambda i, j, s: (i, s))
        y_sl = lambda q: (slice(None), slice(q * ws, (q + 1) * ws))
    else:
        y_spec = pl.BlockSpec((group, tm, ws), lambda i, j, s: (s, i, 0))
        y_sl = lambda q: q
    w_spec = pl.BlockSpec((group, tn, ws), lambda i, j, s: (s, j, 0))
    return _matmul(name, (n // tm, d // tn, N_DEV // group), 2, (*dys, *ws_all), [y_spec] * k_terms + [w_spec] * k_terms,
                   [(i, y_sl(q), k_terms + i, q, 0) for i in range(k_terms) for q in range(group)], NT, [(tm, tn)], [S((n, d), F32)],
                   [pl.BlockSpec((tm, tn), lambda i, j, s: (i, j))], _store_epilogue, after)[0]


def _rms_fwd(name, x, g):
    n, d = x.shape
    tm = _tile(n, 256)

    def body(x_ref, g_ref, o_ref):
        xv = x_ref[...]
        r = lax.rsqrt(jnp.mean(xv * xv, axis=-1, keepdims=True) + RMS_EPS)
        o_ref[...] = (xv * r * g_ref[...]).astype(BF16)

    return pl.pallas_call(body, name=name, grid=(n // tm,),
                          in_specs=[pl.BlockSpec((tm, d), lambda i: (i, 0)), pl.BlockSpec((1, d), lambda i: (0, 0))],
                          out_specs=pl.BlockSpec((tm, d), lambda i: (i, 0)), out_shape=S((n, d), BF16), compiler_params=_cp(1))(x, g)


def _rms_bwd(name, x, g, dy, res):
    n, d = x.shape
    tm = _tile(n, 256)

    def body(x_ref, g_ref, dy_ref, res_ref, dx_ref, dxb_ref, dg_ref):
        xv, dyv = x_ref[...], dy_ref[...]
        r = lax.rsqrt(jnp.mean(xv * xv, axis=-1, keepdims=True) + RMS_EPS)
        xr = xv * r
        dgy = dyv * g_ref[...]
        dx = res_ref[...] + r * (dgy - xr * jnp.mean(dgy * xr, axis=-1, keepdims=True))
        dx_ref[...] = dx
        dxb_ref[...] = dx.astype(BF16)

        @pl.when(pl.program_id(0) == 0)
        def _():
            dg_ref[...] = jnp.zeros_like(dg_ref)

        dg_ref[...] += jnp.sum(dyv * xr, axis=0, keepdims=True)

    row = pl.BlockSpec((tm, d), lambda i: (i, 0))
    vec = pl.BlockSpec((1, d), lambda i: (0, 0))
    return pl.pallas_call(body, name=name, grid=(n // tm,), in_specs=[row, vec, row, row], out_specs=[row, row, vec],
                          out_shape=[S((n, d), F32), S((n, d), BF16), S((1, d), F32)], compiler_params=_cp(1))(x, g, dy, res)


def _final_loss(h, g, target):
    n, d = h.shape
    tm = _tile(n, 256)

    def body(h_ref, g_ref, t_ref, dh_ref, dhb_ref, dg_ref, sse_ref):
        hv, gv = h_ref[...], g_ref[...]
        r = lax.rsqrt(jnp.mean(hv * hv, axis=-1, keepdims=True) + RMS_EPS)
        hr = hv * r
        err = hr * gv - t_ref[...]
        dy = err * (1.0 / d)
        dgy = dy * gv
        dh = r * (dgy - hr * jnp.mean(dgy * hr, axis=-1, keepdims=True))
        dh_ref[...] = dh
        dhb_ref[...] = dh.astype(BF16)

        @pl.when(pl.program_id(0) == 0)
        def _():
            dg_ref[...] = jnp.zeros_like(dg_ref)
            sse_ref[...] = jnp.zeros_like(sse_ref)

        dg_ref[...] += jnp.sum(dy * hr, axis=0, keepdims=True)
        sse_ref[...] += jnp.sum(err * err)

    row = pl.BlockSpec((tm, d), lambda i: (i, 0))
    vec = pl.BlockSpec((1, d), lambda i: (0, 0))
    one = pl.BlockSpec((8, HEAD_DIM), lambda i: (0, 0))
    return pl.pallas_call(body, name="final_loss", grid=(n // tm,), in_specs=[row, vec, row], out_specs=[row, row, vec, one],
                          out_shape=[S((n, d), F32), S((n, d), BF16), S((1, d), F32), S((8, HEAD_DIM), F32)],
                          compiler_params=_cp(1))(h, g, target)


def _dot(a, b, dims):
    return lax.dot_general(a, b, dims, preferred_element_type=F32)


def _split3(x):
    hi = x.astype(BF16)
    r1 = x - hi.astype(F32)
    mid = r1.astype(BF16)
    lo = (r1 - mid.astype(F32)).astype(BF16)
    return hi, mid, lo


def _scan_cols(x, tri, terms):
    parts = _split3(x)[:terms]
    out = _dot(parts[0], tri, NN)
    for p in parts[1:]:
        out = out + _dot(p, tri, NN)
    return out


def _head_norm_fwd(o, g):
    r = lax.rsqrt(jnp.mean(o * o, axis=-1, keepdims=True) + RMS_EPS)
    return o * r * g


def _head_norm_bwd(o, g, dy):
    r = lax.rsqrt(jnp.mean(o * o, axis=-1, keepdims=True) + RMS_EPS)
    orr = o * r
    dgy = dy * g
    return r * (dgy - orr * jnp.mean(dgy * orr, axis=-1, keepdims=True)), dy * orr


def _interleave_plan(t):
    return [(i, dil, t // dil) for i, dil in enumerate(DILATIONS)]


def _band_mask(u, blocks_per_seq):
    row = lax.broadcasted_iota(jnp.int32, (BLK, 2 * BLK), 0)
    col = lax.broadcasted_iota(jnp.int32, (BLK, 2 * BLK), 1)
    dist = row + BLK - col
    has_prev = (u % blocks_per_seq) != 0
    return (dist >= 0) & (dist <= BLK) & ((col >= BLK) | has_prev)


def _dil_fwd(proj, g_dil, bl, t, n_heads):
    n = bl * t
    nb = t // BLK
    scale = HEAD_DIM ** -0.5
    plan = _interleave_plan(t)
    chunk = _tile(t, 256)

    def body(q_ref, k_ref, v_ref, g_ref, omix_ref, opre_ref, lse_ref, stg, qd, kd, vd, ob, lb, on, ln):
        for src, dst, pad in ((q_ref, qd, 0), (k_ref, kd, BLK), (v_ref, vd, BLK)):
            stg[...] = src[...].astype(F32)
            for bi, dil, sub in plan:
                if pad:
                    dst[bi, pl.ds(0, BLK), :] = jnp.zeros((BLK, HEAD_DIM), BF16)
                if dil == 1:
                    dst[bi, pl.ds(pad, t), :] = src[...]
                else:
                    for r in range(dil):
                        dst[bi, pl.ds(pad + r * sub, sub), :] = stg[pl.ds(r, sub, stride=dil), :].astype(BF16)

        for bi, dil, sub in plan:
            def blk(u, carry, bi=bi, sub=sub):
                rows = pl.ds(pl.multiple_of(u * BLK, BLK), BLK)
                win = pl.ds(pl.multiple_of(u * BLK, BLK), 2 * BLK)
                sc = _dot(qd[bi, rows, :], kd[bi, win, :], NT) * scale
                sc = jnp.where(_band_mask(u, sub // BLK), sc, NEG)
                m = jnp.max(sc, axis=-1, keepdims=True)
                p = jnp.exp(sc - m)
                den = jnp.sum(p, axis=-1, keepdims=True)
                ob[bi, rows, :] = _dot((p / den).astype(BF16), vd[bi, win, :], NN)
                lb[bi, rows, :] = jnp.broadcast_to(m + jnp.log(den), (BLK, HEAD_DIM))
                return carry
            lax.fori_loop(0, nb // DIL_UNROLL, lambda i, c, blk=blk: [blk(i * DIL_UNROLL + s, c) for s in range(DIL_UNROLL)][-1], 0)

        for bi, dil, sub in plan[1:]:
            for r in range(dil):
                on[bi - 1, pl.ds(r, sub, stride=dil), :] = ob[bi, pl.ds(r * sub, sub), :]
                ln[bi - 1, pl.ds(r, sub, stride=dil), :] = lb[bi, pl.ds(r * sub, sub), :]

        def merge(i, carry):
            rows = pl.ds(pl.multiple_of(i * chunk, chunk), chunk)
            l0, l1, l2 = lb[0, rows, :], ln[0, rows, :], ln[1, rows, :]
            mx = jnp.maximum(jnp.maximum(l0, l1), l2)
            w0, w1, w2 = jnp.exp(l0 - mx), jnp.exp(l1 - mx), jnp.exp(l2 - mx)
            tot = w0 + w1 + w2
            o = (w0 / tot) * ob[0, rows, :] + (w1 / tot) * on[0, rows, :] + (w2 / tot) * on[1, rows, :]
            lse_ref[rows, :] = mx + jnp.log(tot)
            opre_ref[rows, :] = o
            omix_ref[rows, :] = _head_norm_fwd(o, g_ref[...]).astype(BF16)
            return carry
        lax.fori_loop(0, t // chunk, merge, 0)

    hs = n_heads
    col = lambda off: pl.BlockSpec((t, HEAD_DIM), lambda b, h: (b, off + h))
    return pl.pallas_call(
        body, name="dil_fwd", grid=(bl, hs),
        in_specs=[col(0), col(hs), col(2 * hs), pl.BlockSpec((1, HEAD_DIM), lambda b, h: (0, h))],
        out_specs=[col(0), col(0), pl.BlockSpec((t, HEAD_DIM), lambda b, h: (b * hs + h, 0))],
        out_shape=[S((n, 2 * hs * HEAD_DIM), BF16), S((n, 2 * hs * HEAD_DIM), F32), S((bl * hs * t, HEAD_DIM), F32)],
        scratch_shapes=[pltpu.VMEM((t, HEAD_DIM), F32), pltpu.VMEM((3, t, HEAD_DIM), BF16),
                        pltpu.VMEM((3, t + BLK, HEAD_DIM), BF16), pltpu.VMEM((3, t + BLK, HEAD_DIM), BF16),
                        pltpu.VMEM((3, t, HEAD_DIM), F32), pltpu.VMEM((3, t, HEAD_DIM), F32),
                        pltpu.VMEM((2, t, HEAD_DIM), F32), pltpu.VMEM((2, t, HEAD_DIM), F32)],
        compiler_params=_cp(2),
    )(proj, proj, proj, g_dil)


def _dil_bwd(proj, opre, lse, d_omix, g_dil, rc, rs1, rs2, bl, t, n_heads):
    n = bl * t
    nb = t // BLK
    scale = HEAD_DIM ** -0.5
    plan = _interleave_plan(t)
    chunk = _tile(t, 256)

    def body(q_ref, k_ref, v_ref, opre_ref, lse_ref, dy_ref, g_ref, c_ref, s1_ref, s2_ref, out_ref, dg_ref,
             stg, qd, kd, vd, dod, ldd, dqd, dkc, dkp, dvc, dvp, sk, sv):
        which = pl.program_id(2)

        @pl.when(jnp.logical_and(which == 0, pl.program_id(1) == 0))
        def _():
            dg_ref[...] = jnp.zeros_like(dg_ref)

        @pl.when(which == 0)
        def _():
            def prep(i, dg):
                rows = pl.ds(pl.multiple_of(i * chunk, chunk), chunk)
                o = opre_ref[rows, :]
                d_o, dg_rows = _head_norm_bwd(o, g_ref[...], dy_ref[rows, :])
                stg[rows, :] = d_o
                lane = lax.broadcasted_iota(jnp.int32, (chunk, HEAD_DIM), 1)
                ldd[0, rows, :] = jnp.where(lane < HEAD_DIM // 2, lse_ref[rows, :], jnp.sum(d_o * o, axis=-1, keepdims=True))
                return dg + jnp.sum(dg_rows, axis=0, keepdims=True)
            dg_ref[...] += lax.fori_loop(0, t // chunk, prep, jnp.zeros((1, HEAD_DIM), F32))

            dod[0] = stg[...].astype(BF16)
            for bi, dil, sub in plan[1:]:
                for r in range(dil):
                    dst = pl.ds(r * sub, sub)
                    dod[bi, dst, :] = stg[pl.ds(r, sub, stride=dil), :].astype(BF16)
                    ldd[bi, dst, :] = ldd[0, pl.ds(r, sub, stride=dil), :]
            for src, dst, pad in ((q_ref, qd, 0), (k_ref, kd, BLK), (v_ref, vd, BLK)):
                stg[...] = src[...].astype(F32)
                for bi, dil, sub in plan:
                    if pad:
                        dst[bi, pl.ds(0, BLK), :] = jnp.zeros((BLK, HEAD_DIM), BF16)
                    if dil == 1:
                        dst[bi, pl.ds(pad, t), :] = src[...]
                    else:
                        for r in range(dil):
                            dst[bi, pl.ds(pad + r * sub, sub), :] = stg[pl.ds(r, sub, stride=dil), :].astype(BF16)

            for bi, dil, sub in plan:
                def blk(u, carry, bi=bi, sub=sub):
                    rows = pl.ds(pl.multiple_of(u * BLK, BLK), BLK)
                    win = pl.ds(pl.multiple_of(u * BLK, BLK), 2 * BLK)
                    qb, kw, vw, dob = qd[bi, rows, :], kd[bi, win, :], vd[bi, win, :], dod[bi, rows, :]
                    sc = _dot(qb, kw, NT) * scale
                    stats = ldd[bi, rows, :]
                    p = jnp.where(_band_mask(u, sub // BLK), jnp.exp(sc - stats[:, :1]), 0.0)
                    dp = _dot(dob, vw, NT)
                    ds = (p * (dp - stats[:, HEAD_DIM // 2:HEAD_DIM // 2 + 1]) * scale).astype(BF16)
                    dqd[bi, rows, :] = _dot(ds, kw, NN)
                    dk_win = _dot(ds, qb, TN)
                    dv_win = _dot(p.astype(BF16), dob, TN)
                    dkp[bi, rows, :] = dk_win[:BLK]
                    dkc[bi, rows, :] = dk_win[BLK:]
                    dvp[bi, rows, :] = dv_win[:BLK]
                    dvc[bi, rows, :] = dv_win[BLK:]
                    return carry
                lax.fori_loop(0, nb // DIL_UNROLL, lambda i, c, blk=blk: [blk(i * DIL_UNROLL + s, c) for s in range(DIL_UNROLL)][-1], 0)

            for cur, prev, undo_rope, dst in ((dqd, None, True, out_ref), (dkc, dkp, True, sk), (dvc, dvp, False, sv)):
                def summed(bi, start, size, cur=cur, prev=prev):
                    v = cur[bi, pl.ds(start, size), :]
                    if prev is None:
                        return v
                    if start + size < t:
                        return v + prev[bi, pl.ds(start + BLK, size), :]
                    if size == BLK:
                        return v
                    return v + jnp.concatenate([prev[bi, pl.ds(start + BLK, size - BLK), :], jnp.zeros((BLK, HEAD_DIM), F32)], axis=0)
                stg[...] = summed(0, 0, t)
                for bi, dil, sub in plan[1:]:
                    for r in range(dil):
                        stg[pl.ds(r, sub, stride=dil), :] += summed(bi, r * sub, sub)
                if undo_rope:
                    dst[...] = _rope_bwd(stg[...], c_ref[...], s1_ref[...], s2_ref[...]).astype(BF16)
                else:
                    dst[...] = stg[...].astype(BF16)

        @pl.when(which == 1)
        def _():
            out_ref[...] = sk[...]

        @pl.when(which == 2)
        def _():
            out_ref[...] = sv[...]

    hs = n_heads
    col = lambda off: pl.BlockSpec((t, HEAD_DIM), lambda h, b, w: (b, off + h))
    per_head = pl.BlockSpec((t, HEAD_DIM), lambda h, b, w: (b * hs + h, 0))
    tab = pl.BlockSpec((t, HEAD_DIM), lambda h, b, w: (b, 0))
    gvec = pl.BlockSpec((1, HEAD_DIM), lambda h, b, w: (0, h))
    tb = (t, HEAD_DIM)
    tp = (t + BLK, HEAD_DIM)
    return pl.pallas_call(
        body, name="dil_bwd", grid=(hs, bl, 3),
        in_specs=[col(0), col(hs), col(2 * hs), col(0), per_head, col(0), gvec, tab, tab, tab],
        out_specs=[pl.BlockSpec((t, HEAD_DIM), lambda h, b, w: (b, w * hs + h)), gvec],
        out_shape=[S((n, 6 * hs * HEAD_DIM), BF16), S((1, hs * HEAD_DIM), F32)],
        scratch_shapes=[pltpu.VMEM(tb, F32), pltpu.VMEM((3,) + tb, BF16), pltpu.VMEM((3,) + tp, BF16), pltpu.VMEM((3,) + tp, BF16),
                        pltpu.VMEM((3,) + tb, BF16), pltpu.VMEM((3,) + tb, F32), pltpu.VMEM((3,) + tb, F32),
                        pltpu.VMEM((3,) + tb, F32), pltpu.VMEM((3,) + tb, F32), pltpu.VMEM((3,) + tb, F32), pltpu.VMEM((3,) + tb, F32),
                        pltpu.VMEM(tb, BF16), pltpu.VMEM(tb, BF16)],
        compiler_params=_cp(3),
    )(proj, proj, proj, opre, lse, d_omix, g_dil, rc, rs1, rs2)


def _sb_tile(qb, kb, scale, shift):
    z = _dot(qb, kb, NT) * scale
    tl = jnp.log(1.0 + jnp.exp(-jnp.abs(z)))
    log_not = -(jnp.maximum(z, 0.0) + tl)
    log_beta = z + log_not
    strict = None
    if shift is not None:
        row = lax.broadcasted_iota(jnp.int32, (SBQ, SBK), 0)
        col = lax.broadcasted_iota(jnp.int32, (SBQ, SBK), 1)
        strict = col + shift < row
        log_not = jnp.where(strict, log_not, 0.0)
    return log_not, log_beta, strict


def _tri(cmp):
    row = lax.broadcasted_iota(jnp.int32, (SBK, SBK), 0)
    col = lax.broadcasted_iota(jnp.int32, (SBK, SBK), 1)
    return jnp.where(cmp(row, col), 1.0, 0.0).astype(BF16)


SB_EDGE = tuple(range(0, SBQ, SBK))


def _sb_fwd(proj, g_sb, omix_in, opre_in, bl, t, n_heads):
    nb = t // SBQ
    per_q = SBQ // SBK
    scale = HEAD_DIM ** -0.5

    def body(q_ref, k_ref, v_ref, g_ref, _omix_in, _opre_in, omix_ref, opre_ref, lt_ref):
        later = _tri(lambda r, c: r > c)

        def q_block(qi, carry):
            rows = pl.ds(pl.multiple_of(qi * SBQ, SBQ), SBQ)
            qb = q_ref[rows, :]

            def tile(kj, st, shift):
                run, acc = st
                krows = pl.ds(pl.multiple_of(kj * SBK, SBK), SBK)
                log_not, log_beta, strict = _sb_tile(qb, k_ref[krows, :], scale, shift)
                a = jnp.exp(log_beta + _scan_cols(log_not, later, 2) + run)
                if shift is not None:
                    a = jnp.where(strict, a, 0.0)
                return run + jnp.sum(log_not, axis=-1, keepdims=True), acc + _dot(a.astype(BF16), v_ref[krows, :], NN)

            st = (jnp.zeros((SBQ, 1), F32), jnp.zeros((SBQ, HEAD_DIM), F32))
            for shift in reversed(SB_EDGE):
                st = tile(qi * per_q + shift // SBK, st, shift)
            run, acc = lax.fori_loop(0, qi * per_q, lambda it, st: tile(qi * per_q - 1 - it, st, None), st)
            lt_ref[rows, :] = jnp.broadcast_to(run, (SBQ, HEAD_DIM))
            opre_ref[rows, :] = acc
            omix_ref[rows, :] = _head_norm_fwd(acc, g_ref[...]).astype(BF16)
            return carry

        lax.fori_loop(0, nb, q_block, 0)

    hs = n_heads
    col = lambda off: pl.BlockSpec((t, HEAD_DIM), lambda b, h: (b, off + h))
    return pl.pallas_call(
        body, name="sb_fwd", grid=(bl, hs),
        in_specs=[col(3 * hs), col(4 * hs), col(5 * hs), pl.BlockSpec((1, HEAD_DIM), lambda b, h: (0, h)), HBM_SPEC, HBM_SPEC],
        out_specs=[col(hs), col(hs), pl.BlockSpec((t, HEAD_DIM), lambda b, h: (b * hs + h, 0))],
        out_shape=[S(omix_in.shape, BF16), S(opre_in.shape, F32), S((bl * hs * t, HEAD_DIM), F32)],
        input_output_aliases={4: 0, 5: 1}, compiler_params=_cp(2),
    )(proj, proj, proj, g_sb, omix_in, opre_in)


def _sb_bwd(proj, opre, ltot, d_omix, g_sb, dproj_in, bl, t, n_heads):
    nb = t // SBQ
    per_q = SBQ // SBK
    scale = HEAD_DIM ** -0.5

    def body(q_ref, k_ref, v_ref, opre_ref, lt_ref, dy_ref, g_ref, _dproj_in, out_ref, dg_ref, dq, dk, dv):
        which = pl.program_id(2)

        @pl.when(jnp.logical_and(which == 0, pl.program_id(1) == 0))
        def _():
            dg_ref[...] = jnp.zeros_like(dg_ref)

        @pl.when(which == 0)
        def _():
            upto = _tri(lambda r, c: r <= c)
            before = _tri(lambda r, c: r < c)
            dk[...] = jnp.zeros_like(dk)
            dv[...] = jnp.zeros_like(dv)

            def q_block(qi, dg):
                rows = pl.ds(pl.multiple_of(qi * SBQ, SBQ), SBQ)
                qb = q_ref[rows, :]
                o = opre_ref[rows, :]
                d_o, dg_rows = _head_norm_bwd(o, g_ref[...], dy_ref[rows, :])
                dob = d_o.astype(BF16)
                lt = lt_ref[rows, :][:, :1]

                def tile(kj, st, shift):
                    run, grun, dq_acc = st
                    krows = pl.ds(pl.multiple_of(kj * SBK, SBK), SBK)
                    kb, vb = k_ref[krows, :], v_ref[krows, :]
                    log_not, log_beta, strict = _sb_tile(qb, kb, scale, shift)
                    excl = lt - (run + _scan_cols(log_not, upto, 2))
                    a = jnp.exp(log_beta + excl)
                    if shift is not None:
                        a = jnp.where(strict, a, 0.0)
                    g_a = a * _dot(dob, vb, NT)
                    g_before = grun + _scan_cols(g_a, before, 1)
                    dz = g_a - (g_a + g_before) * jnp.exp(log_beta)
                    if shift is not None:
                        dz = jnp.where(strict, dz, 0.0)
                    dzb = dz.astype(BF16)
                    dk[krows, :] += _dot(dzb, qb, TN)
                    dv[krows, :] += _dot(a.astype(BF16), dob, TN)
                    return (run + jnp.sum(log_not, axis=-1, keepdims=True), grun + jnp.sum(g_a, axis=-1, keepdims=True),
                            dq_acc + _dot(dzb, kb, NN))

                zero_col = jnp.zeros((SBQ, 1), F32)
                st = lax.fori_loop(0, qi * per_q, lambda kj, st: tile(kj, st, None),
                                   (zero_col, zero_col, jnp.zeros((SBQ, HEAD_DIM), F32)))
                for shift in SB_EDGE:
                    st = tile(qi * per_q + shift // SBK, st, shift)
                dq_acc = st[2]
                dq[rows, :] = (dq_acc * scale).astype(BF16)
                return dg + jnp.sum(dg_rows, axis=0, keepdims=True)

            dg_ref[...] += lax.fori_loop(0, nb, q_block, jnp.zeros((1, HEAD_DIM), F32))
            out_ref[...] = dq[...]

        @pl.when(which == 1)
        def _():
            out_ref[...] = (dk[...] * scale).astype(BF16)

        @pl.when(which == 2)
        def _():
            out_ref[...] = dv[...].astype(BF16)

    hs = n_heads
    col = lambda off: pl.BlockSpec((t, HEAD_DIM), lambda h, b, w: (b, off + h))
    per_head = pl.BlockSpec((t, HEAD_DIM), lambda h, b, w: (b * hs + h, 0))
    gvec = pl.BlockSpec((1, HEAD_DIM), lambda h, b, w: (0, h))
    tb = (t, HEAD_DIM)
    return pl.pallas_call(
        body, name="sb_bwd", grid=(hs, bl, 3),
        in_specs=[col(3 * hs), col(4 * hs), col(5 * hs), col(hs), per_head, col(hs), gvec, HBM_SPEC],
        out_specs=[pl.BlockSpec((t, HEAD_DIM), lambda h, b, w: (b, (3 + w) * hs + h)), gvec],
        out_shape=[S(dproj_in.shape, BF16), S((1, hs * HEAD_DIM), F32)],
        scratch_shapes=[pltpu.VMEM(tb, BF16), pltpu.VMEM(tb, F32), pltpu.VMEM(tb, F32)],
        input_output_aliases={7: 0}, compiler_params=_cp(3),
    )(proj, proj, proj, opre, ltot, d_omix, g_sb, dproj_in)


def _all_gather(shards):
    k_w = len(shards)

    def body(*refs):
        ins, outs = refs[:k_w], refs[k_w:2 * k_w]
        send_sems, recv_sems, local_sems = refs[2 * k_w:]
        x, y, c = lax.axis_index("x"), lax.axis_index("y"), lax.axis_index("c")
        me, sibling = (x, y, c), (x, y, 1 - c)
        chips = [(1 - x, y), (x, 1 - y), (1 - x, 1 - y)]

        def slot(dev):
            return 4 * dev[0] + 2 * dev[1] + dev[2]

        def copy(w, k, block, to, src=None):
            dst = outs[w].at[slot(block)]
            return pltpu.make_async_remote_copy(
                src_ref=dst if src is None else src, dst_ref=dst, send_sem=send_sems.at[w * 7 + k], recv_sem=recv_sems.at[w * 7 + k],
                device_id=to, device_id_type=MESH)

        mine = [pltpu.make_async_copy(ins[w], outs[w].at[slot(me)], local_sems.at[w]) for w in range(k_w)]
        first = []
        for w in range(k_w):
            mine[w].start()
            first.append(copy(w, 0, me, sibling, src=ins[w]))
            first += [copy(w, 1 + j, me, (*chip, c), src=ins[w]) for j, chip in enumerate(chips)]
        for cp in first:
            cp.start()
        passed = []
        for w in range(k_w):
            for j, chip in enumerate(chips):
                copy(w, 1 + j, (*chip, c), me).wait_recv()
                fwd = copy(w, 4 + j, (*chip, c), sibling)
                fwd.start()
                passed.append(fwd)
        for w in range(k_w):
            copy(w, 0, sibling, me).wait_recv()
            for j, chip in enumerate(chips):
                copy(w, 4 + j, (*chip, 1 - c), me).wait_recv()
        for cp in first + passed:
            cp.wait_send()
        for cp in mine:
            cp.wait()

    return pl.pallas_call(
        body, name="weights_all_gather", in_specs=[HBM_SPEC] * k_w, out_specs=[HBM_SPEC] * k_w,
        out_shape=[S((N_DEV,) + s.shape, s.dtype) for s in shards],
        scratch_shapes=[pltpu.SemaphoreType.DMA((7 * k_w,)), pltpu.SemaphoreType.DMA((7 * k_w,)), pltpu.SemaphoreType.DMA((k_w,))],
    )(*shards)


def _grad_exchange(grads, after):
    k_w = len(grads)

    def body(*refs):
        ins, outs = refs[:k_w], refs[k_w + 1:2 * k_w + 1]
        send_sems, recv_sems, local_sems = refs[2 * k_w + 1:]
        local, remote = _exchange_copies(ins, outs, send_sems, recv_sems, local_sems, False, ALL_PEERS)
        for cp in local + remote:
            cp.start()
        for cp in remote:
            cp.wait_send()
            cp.wait_recv()
        for cp in local:
            cp.wait()

    n_sem = (N_DEV - 1) * k_w
    return pl.pallas_call(
        body, name="grad_exchange", in_specs=[HBM_SPEC] * k_w + [pl.BlockSpec(memory_space=pl.ANY)], out_specs=[HBM_SPEC] * k_w,
        out_shape=[S(g.shape, g.dtype) for g in grads],
        scratch_shapes=[pltpu.SemaphoreType.DMA((n_sem,)), pltpu.SemaphoreType.DMA((n_sem,)), pltpu.SemaphoreType.DMA((k_w,))],
    )(*grads, after)


def _peer(x, y, c, k):
    px, py, pc = x ^ (k >> 2), y ^ ((k >> 1) & 1), c ^ (k & 1)
    return (px, py, pc), 4 * px + 2 * py + pc


ALL_PEERS = tuple(range(1, N_DEV))
SAME_CORE_PEERS = (2, 4, 6)


def _exchange_copies(srcs, lands, send_sems, recv_sems, local_sems, gather, peers):
    x, y, c = lax.axis_index("x"), lax.axis_index("y"), lax.axis_index("c")
    my_slot = 4 * x + 2 * y + c
    local, remote = [], []
    for w, (src, land) in enumerate(zip(srcs, lands)):
        local.append(pltpu.make_async_copy(src if gather else src.at[my_slot], land.at[my_slot], local_sems.at[w]))
        for i, k in enumerate(peers):
            peer, peer_slot = _peer(x, y, c, k)
            remote.append(pltpu.make_async_remote_copy(
                src_ref=src if gather else src.at[peer_slot], dst_ref=land.at[my_slot],
                send_sem=send_sems.at[w * len(peers) + i], recv_sem=recv_sems.at[w * len(peers) + i],
                device_id=peer, device_id_type=MESH))
    return local, remote


def _exchange_start(name, srcs, gather, after, peers=ALL_PEERS):
    k_w = len(srcs)
    land_shapes = [((N_DEV,) + s.shape) if gather else s.shape for s in srcs]

    def body(*refs):
        src_refs, land_refs = refs[:k_w], refs[k_w:2 * k_w]
        send_sems, recv_sems, local_sems = refs[2 * k_w + 1:2 * k_w + 4]
        token = refs[-1]
        local, remote = _exchange_copies(src_refs, land_refs, send_sems, recv_sems, local_sems, gather, peers)
        for cp in local + remote:
            cp.start()
        token[...] = jnp.zeros_like(token)

    n_sem = len(peers) * k_w
    hbm = lambda a: pltpu.with_memory_space_constraint(a, pltpu.HBM)
    outs = pl.pallas_call(
        body, name=name,
        in_specs=[HBM_SPEC] * (2 * k_w) + [pl.BlockSpec(memory_space=pl.ANY)],
        out_shape=(pltpu.SemaphoreType.DMA((n_sem,)), pltpu.SemaphoreType.DMA((n_sem,)), pltpu.SemaphoreType.DMA((k_w,)),
                   *[pltpu.HBM(s.shape, s.dtype) for s in srcs], *[pltpu.HBM(ls, s.dtype) for ls, s in zip(land_shapes, srcs)],
                   S((8, HEAD_DIM), F32)),
        out_specs=(SEM_SPEC, SEM_SPEC, SEM_SPEC, *[HBM_SPEC] * (2 * k_w), pl.BlockSpec(memory_space=pltpu.VMEM)),
        input_output_aliases={i: 3 + i for i in range(2 * k_w)},
        compiler_params=pltpu.CompilerParams(has_side_effects=SIDE_EFFECT),
    )(*[hbm(s) for s in srcs], *[hbm(lax.empty(ls, s.dtype)) for ls, s in zip(land_shapes, srcs)], after)
    return dict(sems=outs[:3], srcs=outs[3:3 + k_w], lands=outs[3 + k_w:3 + 2 * k_w], token_block=outs[-1], token=outs[-1][0, 0], gather=gather, peers=peers)


def _exchange_wait(name, handle, after):
    k_w = len(handle["srcs"])
    gather = handle["gather"]

    def body(*refs):
        src_refs, land_refs = refs[:k_w], refs[k_w:2 * k_w]
        send_sems, recv_sems, local_sems = refs[2 * k_w:2 * k_w + 3]
        local, remote = _exchange_copies(src_refs, land_refs, send_sems, recv_sems, local_sems, gather, handle["peers"])
        for cp in local:
            cp.wait()
        for cp in remote:
            cp.wait_send()
            cp.wait_recv()

    outs = pl.pallas_call(
        body, name=name,
        in_specs=[HBM_SPEC] * (2 * k_w) + [SEM_SPEC] * 3 + [pl.BlockSpec(memory_space=pl.ANY)],
        out_shape=tuple(pltpu.HBM(a.shape, a.dtype) for a in (*handle["srcs"], *handle["lands"])),
        out_specs=tuple([HBM_SPEC] * (2 * k_w)),
        input_output_aliases={i: i for i in range(2 * k_w)},
        compiler_params=pltpu.CompilerParams(has_side_effects=SIDE_EFFECT),
    )(*handle["srcs"], *handle["lands"], *handle["sems"], after)
    return outs[k_w:]


CHIPS = ((0, 0), (0, 1), (1, 0), (1, 1))


def _swap_copies(lands, send_sems, recv_sems):
    x, y, c = lax.axis_index("x"), lax.axis_index("y"), lax.axis_index("c")
    copies = []
    for w, land in enumerate(lands):
        for i, (px, py) in enumerate(CHIPS):
            mine = land.at[4 * px + 2 * py + c]
            copies.append(pltpu.make_async_remote_copy(
                src_ref=mine, dst_ref=mine, send_sem=send_sems.at[w * len(CHIPS) + i], recv_sem=recv_sems.at[w * len(CHIPS) + i],
                device_id=(x, y, 1 - c), device_id_type=MESH))
    return copies


def _swap_start(name, lands, after):
    k_w = len(lands)

    def body(*refs):
        land_refs = refs[:k_w]
        send_sems, recv_sems = refs[k_w + 1:k_w + 3]
        for cp in _swap_copies(land_refs, send_sems, recv_sems):
            cp.start()
        refs[-1][...] = jnp.zeros_like(refs[-1])

    n_sem = len(CHIPS) * k_w
    outs = pl.pallas_call(
        body, name=name, in_specs=[HBM_SPEC] * k_w + [pl.BlockSpec(memory_space=pl.ANY)],
        out_shape=(pltpu.SemaphoreType.DMA((n_sem,)), pltpu.SemaphoreType.DMA((n_sem,)),
                   *[pltpu.HBM(a.shape, a.dtype) for a in lands], S((8, HEAD_DIM), F32)),
        out_specs=(SEM_SPEC, SEM_SPEC, *[HBM_SPEC] * k_w, pl.BlockSpec(memory_space=pltpu.VMEM)),
        input_output_aliases={i: 2 + i for i in range(k_w)},
        compiler_params=pltpu.CompilerParams(has_side_effects=SIDE_EFFECT),
    )(*lands, after)
    return dict(sems=outs[:2], lands=outs[2:2 + k_w], token_block=outs[-1])


def _swap_wait(name, handle, after):
    k_w = len(handle["lands"])

    def body(*refs):
        for cp in _swap_copies(refs[:k_w], refs[k_w], refs[k_w + 1]):
            cp.wait_send()
            cp.wait_recv()

    return pl.pallas_call(
        body, name=name, in_specs=[HBM_SPEC] * k_w + [SEM_SPEC] * 2 + [pl.BlockSpec(memory_space=pl.ANY)],
        out_shape=tuple(pltpu.HBM(a.shape, a.dtype) for a in handle["lands"]), out_specs=tuple([HBM_SPEC] * k_w),
        input_output_aliases={i: i for i in range(k_w)},
        compiler_params=pltpu.CompilerParams(has_side_effects=SIDE_EFFECT),
    )(*handle["lands"], *handle["sems"], after)


def _adamw(name, parts, w, m, v):
    r, c = w.shape
    tr = _tile(r, max(16, (1 << 19) // c // 16 * 16))

    def body(p_ref, w_ref, m_ref, v_ref, g_ref, d_ref, nm_ref, nv_ref):
        g = p_ref[0].astype(F32)
        for s in range(1, N_DEV):
            g = g + p_ref[s].astype(F32)
        m_new = ADAM_B1 * m_ref[...] + (1.0 - ADAM_B1) * g
        v_new = ADAM_B2 * v_ref[...] + (1.0 - ADAM_B2) * jnp.square(g)
        m_hat = m_new / (1.0 - ADAM_B1 ** ADAM_STEP)
        v_hat = v_new / (1.0 - ADAM_B2 ** ADAM_STEP)
        g_ref[...] = g
        d_ref[...] = -ADAM_LR * (m_hat / (jnp.sqrt(v_hat) + ADAM_EPS) + ADAM_WD * w_ref[...])
        nm_ref[...] = m_new
        nv_ref[...] = v_new

    blk = pl.BlockSpec((tr, c), lambda i: (i, 0))
    return pl.pallas_call(body, name=name, grid=(r // tr,), in_specs=[pl.BlockSpec((N_DEV, tr, c), lambda i: (0, i, 0)), blk, blk, blk],
                          out_specs=[blk] * 4, out_shape=[S((r, c), F32)] * 4, compiler_params=_cp(1))(parts, w, m, v)


def _rope_tables(positions):
    inv_freq = jnp.power(jnp.float32(ROPE_THETA), -jnp.arange(ROPE_HALF, dtype=F32) / ROPE_HALF)
    ang = positions.astype(F32).reshape(-1, 1) * inv_freq
    cos, sin = jnp.cos(ang), jnp.sin(ang)
    n = ang.shape[0]
    rest = HEAD_DIM - 2 * ROPE_HALF
    zeros = jnp.zeros((n, ROPE_HALF), F32)
    c = jnp.concatenate([cos, cos, jnp.ones((n, rest), F32)], axis=1)
    s1 = jnp.concatenate([zeros, sin, jnp.zeros((n, rest), F32)], axis=1)
    s2 = jnp.concatenate([-sin, zeros, jnp.zeros((n, rest), F32)], axis=1)
    return c, s1, s2


def kernel(x, positions, norm_mix_g, w_in, norm_out_dil_g, norm_out_sb_g, w_out, norm_ffn_g, w_gate, w_up, w_down, norm_final_g, loss_target, m_norm_mix_g, m_w_in, m_norm_out_dil_g, m_norm_out_sb_g, m_w_out, m_norm_ffn_g, m_w_gate, m_w_up, m_w_down, m_norm_final_g, v_norm_mix_g, v_w_in, v_norm_out_dil_g, v_norm_out_sb_g, v_w_out, v_norm_ffn_g, v_w_gate, v_w_up, v_w_down, v_norm_final_g):
    bl, t, d = x.shape
    n = bl * t
    hs = d // (2 * HEAD_DIM)
    x2 = x.reshape(n, d)
    target = loss_target.reshape(n, d)
    g_final = norm_final_g.reshape(1, d)
    rc, rs1, rs2 = _rope_tables(positions)

    (win_all,) = _all_gather([w_in[0].astype(BF16)])
    ex_wout = _exchange_start("wout_gather_start", [w_out[0].astype(BF16)], True, win_all, SAME_CORE_PEERS)
    tr = lambda a: jnp.swapaxes(a[0], 0, 1)
    ex_wgu = _exchange_start("wgu_gather_start", [tr(w_gate).astype(BF16), tr(w_up).astype(BF16)], True, ex_wout["token_block"], SAME_CORE_PEERS)
    ex_wd = _exchange_start("wd_gather_start", [w_down[0].astype(BF16)], True, ex_wgu["token_block"], SAME_CORE_PEERS)
    rc = rc + ex_wd["token"]

    hn1 = _rms_fwd("rms_mix_fwd", x2, norm_mix_g)
    proj = _proj_fwd(hn1, win_all, rc, rs1, rs2, 2 * hs)
    omix, opre, lse = _dil_fwd(proj, norm_out_dil_g, bl, t, hs)
    omix, opre, ltot = _sb_fwd(proj, norm_out_sb_g, omix, opre, bl, t, hs)
    (wout_half,) = _exchange_wait("wout_gather_wait", ex_wout, ltot)
    wg_half, wu_half = _exchange_wait("wgu_gather_wait", ex_wgu, wout_half)
    sw_wout = _swap_start("wout_swap_start", [wout_half], wg_half)
    sw_wgu = _swap_start("wgu_swap_start", [wg_half, wu_half], sw_wout["token_block"])
    (wout_all,) = _swap_wait("wout_swap_wait", sw_wout, sw_wgu["token_block"])
    wout_full = wout_all.reshape(d, d)
    h1 = _dense_res("out_fwd", omix, wout_full, x2, NN)
    hn2 = _rms_fwd("rms_ffn_fwd", h1, norm_ffn_g)
    wg_all, wu_all = _swap_wait("wgu_swap_wait", sw_wgu, hn2)
    (wd_half,) = _exchange_wait("wd_gather_wait", ex_wd, wg_all)
    sw_wd = _swap_start("wd_swap_start", [wd_half], wg_all)
    gate = _gate_fwd(hn2, wg_all, sw_wd["token_block"])
    up, act = _up_fwd(hn2, wu_all, gate)
    (wd_all,) = _swap_wait("wd_swap_wait", sw_wd, act)
    h2 = _rows_fwd("down_fwd", (act,), (wd_all,), h1)
    dh2, dh2b, dg_final, sse = _final_loss(h2, g_final, target)
    loss = lax.psum(sse[0, 0], ("x", "y", "c")) * (0.5 / d)

    dgate, dup = _dact_bwd(dh2b, wd_all, gate, up)
    dwd = _dw_rows_bwd("dwd_bwd", act, dh2b)
    ex_dwd = _exchange_start("dwd_exchange_start", [dwd], False, dgate)
    dwg = _dw_rows_bwd("dwg_bwd", dgate, hn2, ex_dwd["token_block"])
    dwu = _dw_rows_bwd("dwu_bwd", dup, hn2)
    ex_dwgu = _exchange_start("dwgu_exchange_start", [dwg, dwu], False, ex_dwd["token_block"])
    dhn2 = _rows_fwd("dhn2_bwd", (dgate, dup), (wg_all, wu_all), None, ex_dwgu["token_block"])
    dh1, dh1b, dg_ffn = _rms_bwd("rms_ffn_bwd", h1, norm_ffn_g, dhn2, dh2)
    d_omix = _dense_res("domix_bwd", dh1b, wout_full, None, NT)
    dwout = _tn_full("dwout_bwd", omix, dh1b).reshape(N_DEV, d // N_DEV, d)
    ex_dwout = _exchange_start("dwout_exchange_start", [dwout], False, d_omix)
    dproj, dg_dil = _dil_bwd(proj, opre, lse, d_omix, norm_out_dil_g + ex_dwout["token"], rc, rs1, rs2, bl, t, hs)
    dproj, dg_sb = _sb_bwd(proj, opre, ltot, d_omix, norm_out_sb_g, dproj, bl, t, hs)
    dwin = _dwin_bwd(hn1, dproj)
    ex_dwin = _exchange_start("dwin_exchange_start", [dwin], False, dproj)
    dhn1 = _dhn_from_shards("dhn1_bwd", (dproj,), (win_all,), True, ex_dwin["token_block"])
    dx, _, dg_mix = _rms_bwd("rms_mix_bwd", x2, norm_mix_g, dhn1, dh1)

    gains = [norm_mix_g, norm_out_dil_g, norm_out_sb_g, norm_ffn_g, g_final]
    m_gains = [m_norm_mix_g, m_norm_out_dil_g, m_norm_out_sb_g, m_norm_ffn_g, m_norm_final_g.reshape(1, d)]
    v_gains = [v_norm_mix_g, v_norm_out_dil_g, v_norm_out_sb_g, v_norm_ffn_g, v_norm_final_g.reshape(1, d)]
    dg_vec = jnp.concatenate([dg_mix, dg_dil, dg_sb, dg_ffn, dg_final], axis=1)
    dg_all = jnp.broadcast_to(dg_vec[None], (N_DEV,) + dg_vec.shape)

    out_w = {}
    (rwd,) = _exchange_wait("dwd_exchange_wait", ex_dwd, dx)
    out_w["w_down"] = _adamw("adamw_w_down", rwd, w_down[0], m_w_down[0], v_w_down[0])
    rwg, rwu = _exchange_wait("dwgu_exchange_wait", ex_dwgu, out_w["w_down"][0])
    gate_t = _adamw("adamw_w_gate", rwg, tr(w_gate), tr(m_w_gate), tr(v_w_gate))
    up_t = _adamw("adamw_w_up", rwu, tr(w_up), tr(m_w_up), tr(v_w_up))
    out_w["w_gate"] = [jnp.swapaxes(o, 0, 1) for o in gate_t]
    out_w["w_up"] = [jnp.swapaxes(o, 0, 1) for o in up_t]
    (rwout,) = _exchange_wait("dwout_exchange_wait", ex_dwout, up_t[0])
    out_w["w_out"] = _adamw("adamw_w_out", rwout, w_out[0], m_w_out[0], v_w_out[0])
    (rwin,) = _exchange_wait("dwin_exchange_wait", ex_dwin, out_w["w_out"][0])
    out_w["w_in"] = _adamw("adamw_w_in", rwin, w_in[0], m_w_in[0], v_w_in[0])
    (rg,) = _grad_exchange([dg_all], out_w["w_in"][0])
    out_w = {name: [o[None] for o in outs] for name, outs in out_w.items()}
    cat = lambda vs: jnp.concatenate(vs, axis=1)
    gain_out = _adamw("adamw_gains", rg, cat(gains), cat(m_gains), cat(v_gains))
    widths = [d, d // 2, d // 2, d]
    cuts = [sum(widths[:i + 1]) for i in range(4)]
    gain_split = [jnp.split(o, cuts, axis=1) for o in gain_out]

    def ordered(kind):
        gs = gain_split[kind]
        return (gs[0], out_w["w_in"][kind], gs[1], gs[2], out_w["w_out"][kind], gs[3], out_w["w_gate"][kind],
                out_w["w_up"][kind], out_w["w_down"][kind], gs[4].reshape(d))

    return (loss, dx.reshape(bl, t, d), *ordered(0), *ordered(1), *ordered(2), *ordered(3))
```

```python
import functools
import math

import jax
import jax.numpy as jnp
from jax import lax
from jax.experimental import pallas as pl
from jax.experimental.pallas import tpu as pltpu

F32 = jnp.float32
BF16 = jnp.bfloat16
S = jax.ShapeDtypeStruct

N_DEV = 8
HEAD_DIM = 128
BLK = 128
SBQ = 1024
SBK = 256
DIL_UNROLL = 16
ROPE_HALF = 16
ROPE_THETA = 500000.0
RMS_EPS = 1e-5
DILATIONS = (1, 4, 16)
NEG = -1e30
VMEM_LIMIT = 56 * 1024 * 1024

ADAM_LR = 0.001
ADAM_B1 = 0.9
ADAM_B2 = 0.999
ADAM_EPS = 1e-08
ADAM_WD = 0.01
ADAM_STEP = 10

MESH = pl.DeviceIdType.MESH
HBM_SPEC = pl.BlockSpec(memory_space=pltpu.HBM)
SEM_SPEC = pl.BlockSpec(memory_space=pltpu.SEMAPHORE)
SIDE_EFFECT = pltpu.SideEffectType.DATAFLOW_SIDE_EFFECTING


def _cp(n_axes):
    return pltpu.CompilerParams(dimension_semantics=("arbitrary",) * n_axes, vmem_limit_bytes=VMEM_LIMIT)


def _tile(n, want):
    if n <= want:
        return n
    t = want
    while t >= 16:
        if n % t == 0 and t % 16 == 0:
            return t
        t -= 16
    return n


NN = (((1,), (0,)), ((), ()))
NT = (((1,), (1,)), ((), ()))
TN = (((0,), (0,)), ((), ()))
ROW_TILE = 512


def _matmul(name, grid, red_axis, ins, in_specs, terms, dims, acc_shapes, out_shapes, out_specs, epilogue, after=None):
    if after is not None:
        ins, in_specs = (*ins, after), [*in_specs, pl.BlockSpec(memory_space=pl.ANY)]
    n_in, n_out = len(ins), len(out_shapes)
    n_red = grid[red_axis]

    def body(*refs):
        in_refs, out_refs, acc_refs = refs[:n_in], refs[n_in:n_in + n_out], refs[n_in + n_out:]
        sums = {}
        for a_idx, a_sl, b_idx, b_sl, acc_idx in terms:
            a = (in_refs[a_idx][...] if a_sl is None else in_refs[a_idx][a_sl]).astype(BF16)
            b = (in_refs[b_idx][...] if b_sl is None else in_refs[b_idx][b_sl]).astype(BF16)
            prod = lax.dot_general(a, b, dims, preferred_element_type=F32)
            sums[acc_idx] = prod if acc_idx not in sums else sums[acc_idx] + prod
        if n_red == 1:
            for idx, v in sums.items():
                acc_refs[idx][...] = v
            epilogue(acc_refs, in_refs, out_refs)
            return
        k = pl.program_id(red_axis)

        @pl.when(k == 0)
        def _():
            for idx, v in sums.items():
                acc_refs[idx][...] = v

        @pl.when(k > 0)
        def _():
            for idx, v in sums.items():
                acc_refs[idx][...] += v

        @pl.when(k == n_red - 1)
        def _():
            epilogue(acc_refs, in_refs, out_refs)

    return pl.pallas_call(
        body, name=name, grid=grid, in_specs=in_specs, out_specs=out_specs, out_shape=out_shapes,
        scratch_shapes=[pltpu.VMEM(s, F32) for s in acc_shapes], compiler_params=_cp(len(grid)),
    )(*ins)


def _store_epilogue(acc_refs, in_refs, out_refs):
    for acc, out in zip(acc_refs, out_refs):
        out[...] = acc[...].astype(out.dtype)


def _rope_fwd(a, c, s1, s2):
    return a * c + pltpu.roll(a, ROPE_HALF, 1) * s1 + pltpu.roll(a, HEAD_DIM - ROPE_HALF, 1) * s2


def _rope_bwd(d, c, s1, s2):
    return d * c + pltpu.roll(d * s1, HEAD_DIM - ROPE_HALF, 1) + pltpu.roll(d * s2, ROPE_HALF, 1)


def _proj_fwd(hn, w_all, rc, rs1, rs2, n_rope_heads):
    n, d = hn.shape
    _, _, ws = w_all.shape
    tm = _tile(n, ROW_TILE)
    heads_per_shard = ws // HEAD_DIM
    rows = _tile(tm, 256)

    def epilogue(acc_refs, in_refs, out_refs):
        acc, out = acc_refs[0], out_refs[0]
        j = pl.program_id(0)
        for r0 in range(0, tm, rows):
            c, s1, s2 = (ref[pl.ds(r0, rows), :] for ref in in_refs[2:5])
            for hh in range(heads_per_shard):
                a = acc[pl.ds(r0, rows), pl.ds(hh * HEAD_DIM, HEAD_DIM)]
                roped = _rope_fwd(a, c, s1, s2)
                a = jnp.where(j * heads_per_shard + hh < n_rope_heads, roped, a)
                out[pl.ds(r0, rows), pl.ds(hh * HEAD_DIM, HEAD_DIM)] = a.astype(out.dtype)

    tab = pl.BlockSpec((tm, HEAD_DIM), lambda j, m, k: (m, 0))
    return _matmul(
        "proj_fwd", (N_DEV, n // tm, 1), 2, (hn, w_all, rc, rs1, rs2),
        [pl.BlockSpec((tm, d), lambda j, m, k: (m, 0)), pl.BlockSpec((None, d, ws), lambda j, m, k: (j, 0, 0)), tab, tab, tab],
        [(0, None, 1, None, 0)], NN, [(tm, ws)], [S((n, N_DEV * ws), BF16)], [pl.BlockSpec((tm, ws), lambda j, m, k: (m, j))], epilogue)[0]


def _dense_res(name, a, b, res, dims, out_dtype=F32):
    m, kdim = a.shape
    n = b.shape[1] if dims == NN else b.shape[0]
    tm, tn = _tile(m, ROW_TILE), _tile(n, 1024)
    ins = [a, b] + ([res] if res is not None else [])
    b_spec = pl.BlockSpec((kdim, tn), lambda j, i, k: (0, j)) if dims == NN else pl.BlockSpec((tn, kdim), lambda j, i, k: (j, 0))
    specs = [pl.BlockSpec((tm, kdim), lambda j, i, k: (i, 0)), b_spec]
    if res is not None:
        specs.append(pl.BlockSpec((tm, tn), lambda j, i, k: (i, j)))

    def epilogue(acc_refs, in_refs, out_refs):
        v = acc_refs[0][...]
        if res is not None:
            v = v + in_refs[2][...]
        out_refs[0][...] = v.astype(out_dtype)

    return _matmul(name, (n // tn, m // tm, 1), 2, ins, specs, [(0, None, 1, None, 0)], dims, [(tm, tn)],
                   [S((m, n), out_dtype)], [pl.BlockSpec((tm, tn), lambda j, i, k: (i, j))], epilogue)[0]


def _tn_full(name, a, b, out_dtype=BF16):
    m, kdim = a.shape
    n = b.shape[1]
    tk, tn = _tile(kdim, 512), _tile(n, 1024)
    return _matmul(name, (kdim // tk, n // tn, 1), 2, (a, b),
                   [pl.BlockSpec((m, tk), lambda i, j, t: (0, i)), pl.BlockSpec((m, tn), lambda i, j, t: (0, j))],
                   [(0, None, 1, None, 0)], TN, [(tk, tn)], [S((kdim, n), out_dtype)], [pl.BlockSpec((tk, tn), lambda i, j, t: (i, j))],
                   _store_epilogue)[0]


def _gate_fwd(hn, wgt_all, after=None):
    n, d = hn.shape
    _, fs, _ = wgt_all.shape
    tm = _tile(n, ROW_TILE)
    o_spec = pl.BlockSpec((None, tm, fs), lambda j, m, k: (j, m, 0))
    return _matmul("gate_fwd", (N_DEV, n // tm, 1), 2, (hn, wgt_all),
                   [pl.BlockSpec((tm, d), lambda j, m, k: (m, 0)), pl.BlockSpec((None, fs, d), lambda j, m, k: (j, 0, 0))],
                   [(0, None, 1, None, 0)], NT, [(tm, fs)], [S((N_DEV, n, fs), BF16)], [o_spec], _store_epilogue, after)[0]


def _up_fwd(hn, wut_all, gate, after=None):
    n, d = hn.shape
    _, fs, _ = wut_all.shape
    tm = _tile(n, ROW_TILE)
    rows = _tile(tm, 256)

    def epilogue(acc_refs, in_refs, out_refs):
        for r0 in range(0, tm, rows):
            u = acc_refs[0][pl.ds(r0, rows), :]
            g = in_refs[2][pl.ds(r0, rows), :].astype(F32)
            out_refs[0][pl.ds(r0, rows), :] = u.astype(BF16)
            out_refs[1][pl.ds(r0, rows), :] = (g * jax.nn.sigmoid(g) * u).astype(BF16)

    t_spec = pl.BlockSpec((None, tm, fs), lambda j, m, k: (j, m, 0))
    o_shape = S((N_DEV, n, fs), BF16)
    return _matmul("up_fwd", (N_DEV, n // tm, 1), 2, (hn, wut_all, gate),
                   [pl.BlockSpec((tm, d), lambda j, m, k: (m, 0)), pl.BlockSpec((None, fs, d), lambda j, m, k: (j, 0, 0)), t_spec],
                   [(0, None, 1, None, 0)], NT, [(tm, fs)], [o_shape] * 2, [t_spec] * 2, epilogue, after)


def _rows_fwd(name, ys, ws_all, res, after=None):
    _, n, fs = ys[0].shape
    d = ws_all[0].shape[2]
    k_terms = len(ys)
    group = 4 // k_terms
    tm, tn = _tile(n, ROW_TILE), _tile(d, 1024)
    ins = [*ys, *ws_all] + ([res] if res is not None else [])
    specs = ([pl.BlockSpec((group, tm, fs), lambda i, j, s: (s, i, 0))] * k_terms
             + [pl.BlockSpec((group, fs, tn), lambda i, j, s: (s, 0, j))] * k_terms)
    if res is not None:
        specs.append(pl.BlockSpec((tm, tn), lambda i, j, s: (i, j)))

    def epilogue(acc_refs, in_refs, out_refs):
        v = acc_refs[0][...]
        out_refs[0][...] = v if res is None else v + in_refs[2 * k_terms][...]

    return _matmul(name, (n // tm, d // tn, N_DEV // group), 2, ins, specs,
                   [(i, q, k_terms + i, q, 0) for i in range(k_terms) for q in range(group)], NN, [(tm, tn)], [S((n, d), F32)],
                   [pl.BlockSpec((tm, tn), lambda i, j, s: (i, j))], epilogue, after)[0]


def _dact_bwd(dh, wd_all, gate, up):
    n, d = dh.shape
    _, fs, _ = wd_all.shape
    tm = _tile(n, ROW_TILE)
    rows = _tile(tm, 256)

    def epilogue(acc_refs, in_refs, out_refs):
        for r0 in range(0, tm, rows):
            da = acc_refs[0][pl.ds(r0, rows), :]
            g = in_refs[2][pl.ds(r0, rows), :].astype(F32)
            u = in_refs[3][pl.ds(r0, rows), :].astype(F32)
            sg = jax.nn.sigmoid(g)
            out_refs[0][pl.ds(r0, rows), :] = (da * u * (sg * (1.0 + g * (1.0 - sg)))).astype(BF16)
            out_refs[1][pl.ds(r0, rows), :] = (da * (g * sg)).astype(BF16)

    t_spec = pl.BlockSpec((None, tm, fs), lambda j, m, k: (j, m, 0))
    o_shape = S((N_DEV, n, fs), BF16)
    return _matmul("dact_bwd", (N_DEV, n // tm, 1), 2, (dh, wd_all, gate, up),
                   [pl.BlockSpec((tm, d), lambda j, m, k: (m, 0)), pl.BlockSpec((None, fs, d), lambda j, m, k: (j, 0, 0)), t_spec, t_spec],
                   [(0, None, 1, None, 0)], NT, [(tm, fs)], [o_shape] * 2, [t_spec] * 2, epilogue)


def _dw_rows_bwd(name, act, dh, after=None):
    _, n, fs = act.shape
    d = dh.shape[1]
    tn = _tile(d, 512)
    return _matmul(name, (N_DEV, d // tn, 1), 2, (act, dh),
                   [pl.BlockSpec((None, n, fs), lambda j, c, t: (j, 0, 0)), pl.BlockSpec((n, tn), lambda j, c, t: (0, c))],
                   [(0, None, 1, None, 0)], TN, [(fs, tn)], [S((N_DEV, fs, d), BF16)], [pl.BlockSpec((None, fs, tn), lambda j, c, t: (j, 0, c))],
                   _store_epilogue, after)[0]


def _dwin_bwd(hn, dproj):
    n, d = hn.shape
    ws = dproj.shape[1] // N_DEV
    tk = _tile(d, 512)
    return _matmul("dwin_bwd", (N_DEV, d // tk, 1), 2, (hn, dproj),
                   [pl.BlockSpec((n, tk), lambda j, c, t: (0, c)), pl.BlockSpec((n, ws), lambda j, c, t: (0, j))],
                   [(0, None, 1, None, 0)], TN, [(tk, ws)], [S((N_DEV, d, ws), BF16)], [pl.BlockSpec((None, tk, ws), lambda j, c, t: (j, c, 0))],
                   _store_epilogue)[0]


def _dhn_from_shards(name, dys, ws_all, dy_is_flat, after=None):
    if dy_is_flat:
        n, ws = dys[0].shape[0], dys[0].shape[1] // N_DEV
    else:
        _, n, ws = dys[0].shape
    d = ws_all[0].shape[1]
    k_terms = len(dys)
    group = 4 // k_terms
    tm, tn = _tile(n, ROW_TILE), _tile(d, 1024)
    if dy_is_flat:
        y_spec = pl.BlockSpec((tm, group * ws), lambda i, j, s: (i, s))
        y_sl = lambda q: (slice(None), slice(q * ws, (q + 1) * ws))
    else:
        y_spec = pl.BlockSpec((group, tm, ws), lambda i, j, s: (s, i, 0))
        y_sl = lambda q: q
    w_spec = pl.BlockSpec((group, tn, ws), lambda i, j, s: (s, j, 0))
    return _matmul(name, (n // tm, d // tn, N_DEV // group), 2, (*dys, *ws_all), [y_spec] * k_terms + [w_spec] * k_terms,
                   [(i, y_sl(q), k_terms + i, q, 0) for i in range(k_terms) for q in range(group)], NT, [(tm, tn)], [S((n, d), F32)],
                   [pl.BlockSpec((tm, tn), lambda i, j, s: (i, j))], _store_epilogue, after)[0]


def _rms_fwd(name, x, g):
    n, d = x.shape
    tm = _tile(n, 256)

    def body(x_ref, g_ref, o_ref):
        xv = x_ref[...]
        r = lax.rsqrt(jnp.mean(xv * xv, axis=-1, keepdims=True) + RMS_EPS)
        o_ref[...] = (xv * r * g_ref[...]).astype(BF16)

    return pl.pallas_call(body, name=name, grid=(n // tm,),
                          in_specs=[pl.BlockSpec((tm, d), lambda i: (i, 0)), pl.BlockSpec((1, d), lambda i: (0, 0))],
                          out_specs=pl.BlockSpec((tm, d), lambda i: (i, 0)), out_shape=S((n, d), BF16), compiler_params=_cp(1))(x, g)


def _rms_bwd(name, x, g, dy, res):
    n, d = x.shape
    tm = _tile(n, 256)

    def body(x_ref, g_ref, dy_ref, res_ref, dx_ref, dxb_ref, dg_ref):
        xv, dyv = x_ref[...], dy_ref[...]
        r = lax.rsqrt(jnp.mean(xv * xv, axis=-1, keepdims=True) + RMS_EPS)
        xr = xv * r
        dgy = dyv * g_ref[...]
        dx = res_ref[...] + r * (dgy - xr * jnp.mean(dgy * xr, axis=-1, keepdims=True))
        dx_ref[...] = dx
        dxb_ref[...] = dx.astype(BF16)

        @pl.when(pl.program_id(0) == 0)
        def _():
            dg_ref[...] = jnp.zeros_like(dg_ref)

        dg_ref[...] += jnp.sum(dyv * xr, axis=0, keepdims=True)

    row = pl.BlockSpec((tm, d), lambda i: (i, 0))
    vec = pl.BlockSpec((1, d), lambda i: (0, 0))
    return pl.pallas_call(body, name=name, grid=(n // tm,), in_specs=[row, vec, row, row], out_specs=[row, row, vec],
                          out_shape=[S((n, d), F32), S((n, d), BF16), S((1, d), F32)], compiler_params=_cp(1))(x, g, dy, res)


def _final_loss(h, g, target):
    n, d = h.shape
    tm = _tile(n, 256)

    def body(h_ref, g_ref, t_ref, dh_ref, dhb_ref, dg_ref, sse_ref):
        hv, gv = h_ref[...], g_ref[...]
        r = lax.rsqrt(jnp.mean(hv * hv, axis=-1, keepdims=True) + RMS_EPS)
        hr = hv * r
        err = hr * gv - t_ref[...]
        dy = err * (1.0 / d)
        dgy = dy * gv
        dh = r * (dgy - hr * jnp.mean(dgy * hr, axis=-1, keepdims=True))
        dh_ref[...] = dh
        dhb_ref[...] = dh.astype(BF16)

        @pl.when(pl.program_id(0) == 0)
        def _():
            dg_ref[...] = jnp.zeros_like(dg_ref)
            sse_ref[...] = jnp.zeros_like(sse_ref)

        dg_ref[...] += jnp.sum(dy * hr, axis=0, keepdims=True)
        sse_ref[...] += jnp.sum(err * err)

    row = pl.BlockSpec((tm, d), lambda i: (i, 0))
    vec = pl.BlockSpec((1, d), lambda i: (0, 0))
    one = pl.BlockSpec((8, HEAD_DIM), lambda i: (0, 0))
    return pl.pallas_call(body, name="final_loss", grid=(n // tm,), in_specs=[row, vec, row], out_specs=[row, row, vec, one],
                          out_shape=[S((n, d), F32), S((n, d), BF16), S((1, d), F32), S((8, HEAD_DIM), F32)],
                          compiler_params=_cp(1))(h, g, target)


def _dot(a, b, dims):
    return lax.dot_general(a, b, dims, preferred_element_type=F32)


def _split3(x):
    hi = x.astype(BF16)
    r1 = x - hi.astype(F32)
    mid = r1.astype(BF16)
    lo = (r1 - mid.astype(F32)).astype(BF16)
    return hi, mid, lo


def _scan_cols(x, tri, terms):
    parts = _split3(x)[:terms]
    out = _dot(parts[0], tri, NN)
    for p in parts[1:]:
        out = out + _dot(p, tri, NN)
    return out


def _head_norm_fwd(o, g):
    r = lax.rsqrt(jnp.mean(o * o, axis=-1, keepdims=True) + RMS_EPS)
    return o * r * g


def _head_norm_bwd(o, g, dy):
    r = lax.rsqrt(jnp.mean(o * o, axis=-1, keepdims=True) + RMS_EPS)
    orr = o * r
    dgy = dy * g
    return r * (dgy - orr * jnp.mean(dgy * orr, axis=-1, keepdims=True)), dy * orr


def _interleave_plan(t):
    return [(i, dil, t // dil) for i, dil in enumerate(DILATIONS)]


def _band_mask(u, blocks_per_seq):
    row = lax.broadcasted_iota(jnp.int32, (BLK, 2 * BLK), 0)
    col = lax.broadcasted_iota(jnp.int32, (BLK, 2 * BLK), 1)
    dist = row + BLK - col
    has_prev = (u % blocks_per_seq) != 0
    return (dist >= 0) & (dist <= BLK) & ((col >= BLK) | has_prev)


def _dil_fwd(proj, g_dil, bl, t, n_heads):
    n = bl * t
    nb = t // BLK
    scale = HEAD_DIM ** -0.5
    plan = _interleave_plan(t)
    chunk = _tile(t, 256)

    def body(q_ref, k_ref, v_ref, g_ref, omix_ref, opre_ref, lse_ref, stg, qd, kd, vd, ob, lb, on, ln):
        for src, dst, pad in ((q_ref, qd, 0), (k_ref, kd, BLK), (v_ref, vd, BLK)):
            stg[...] = src[...].astype(F32)
            for bi, dil, sub in plan:
                if pad:
                    dst[bi, pl.ds(0, BLK), :] = jnp.zeros((BLK, HEAD_DIM), BF16)
                if dil == 1:
                    dst[bi, pl.ds(pad, t), :] = src[...]
                else:
                    for r in range(dil):
                        dst[bi, pl.ds(pad + r * sub, sub), :] = stg[pl.ds(r, sub, stride=dil), :].astype(BF16)

        for bi, dil, sub in plan:
            def blk(u, carry, bi=bi, sub=sub):
                rows = pl.ds(pl.multiple_of(u * BLK, BLK), BLK)
                win = pl.ds(pl.multiple_of(u * BLK, BLK), 2 * BLK)
                sc = _dot(qd[bi, rows, :], kd[bi, win, :], NT) * scale
                sc = jnp.where(_band_mask(u, sub // BLK), sc, NEG)
                m = jnp.max(sc, axis=-1, keepdims=True)
                p = jnp.exp(sc - m)
                den = jnp.sum(p, axis=-1, keepdims=True)
                ob[bi, rows, :] = _dot((p / den).astype(BF16), vd[bi, win, :], NN)
                lb[bi, rows, :] = jnp.broadcast_to(m + jnp.log(den), (BLK, HEAD_DIM))
                return carry
            lax.fori_loop(0, nb // DIL_UNROLL, lambda i, c, blk=blk: [blk(i * DIL_UNROLL + s, c) for s in range(DIL_UNROLL)][-1], 0)

        for bi, dil, sub in plan[1:]:
            for r in range(dil):
                on[bi - 1, pl.ds(r, sub, stride=dil), :] = ob[bi, pl.ds(r * sub, sub), :]
                ln[bi - 1, pl.ds(r, sub, stride=dil), :] = lb[bi, pl.ds(r * sub, sub), :]

        def merge(i, carry):
            rows = pl.ds(pl.multiple_of(i * chunk, chunk), chunk)
            l0, l1, l2 = lb[0, rows, :], ln[0, rows, :], ln[1, rows, :]
            mx = jnp.maximum(jnp.maximum(l0, l1), l2)
            w0, w1, w2 = jnp.exp(l0 - mx), jnp.exp(l1 - mx), jnp.exp(l2 - mx)
            tot = w0 + w1 + w2
            o = (w0 / tot) * ob[0, rows, :] + (w1 / tot) * on[0, rows, :] + (w2 / tot) * on[1, rows, :]
            lse_ref[rows, :] = mx + jnp.log(tot)
            opre_ref[rows, :] = o
            omix_ref[rows, :] = _head_norm_fwd(o, g_ref[...]).astype(BF16)
            return carry
        lax.fori_loop(0, t // chunk, merge, 0)

    hs = n_heads
    col = lambda off: pl.BlockSpec((t, HEAD_DIM), lambda b, h: (b, off + h))
    return pl.pallas_call(
        body, name="dil_fwd", grid=(bl, hs),
        in_specs=[col(0), col(hs), col(2 * hs), pl.BlockSpec((1, HEAD_DIM), lambda b, h: (0, h))],
        out_specs=[col(0), col(0), pl.BlockSpec((t, HEAD_DIM), lambda b, h: (b * hs + h, 0))],
        out_shape=[S((n, 2 * hs * HEAD_DIM), BF16), S((n, 2 * hs * HEAD_DIM), F32), S((bl * hs * t, HEAD_DIM), F32)],
        scratch_shapes=[pltpu.VMEM((t, HEAD_DIM), F32), pltpu.VMEM((3, t, HEAD_DIM), BF16),
                        pltpu.VMEM((3, t + BLK, HEAD_DIM), BF16), pltpu.VMEM((3, t + BLK, HEAD_DIM), BF16),
                        pltpu.VMEM((3, t, HEAD_DIM), F32), pltpu.VMEM((3, t, HEAD_DIM), F32),
                        pltpu.VMEM((2, t, HEAD_DIM), F32), pltpu.VMEM((2, t, HEAD_DIM), F32)],
        compiler_params=_cp(2),
    )(proj, proj, proj, g_dil)


def _dil_bwd(proj, opre, lse, d_omix, g_dil, rc, rs1, rs2, bl, t, n_heads):
    n = bl * t
    nb = t // BLK
    scale = HEAD_DIM ** -0.5
    plan = _interleave_plan(t)
    chunk = _tile(t, 256)

    def body(q_ref, k_ref, v_ref, opre_ref, lse_ref, dy_ref, g_ref, c_ref, s1_ref, s2_ref, out_ref, dg_ref,
             stg, qd, kd, vd, dod, ldd, dqd, dkc, dkp, dvc, dvp, sk, sv):
        which = pl.program_id(2)

        @pl.when(jnp.logical_and(which == 0, pl.program_id(1) == 0))
        def _():
            dg_ref[...] = jnp.zeros_like(dg_ref)

        @pl.when(which == 0)
        def _():
            def prep(i, dg):
                rows = pl.ds(pl.multiple_of(i * chunk, chunk), chunk)
                o = opre_ref[rows, :]
                d_o, dg_rows = _head_norm_bwd(o, g_ref[...], dy_ref[rows, :])
                stg[rows, :] = d_o
                lane = lax.broadcasted_iota(jnp.int32, (chunk, HEAD_DIM), 1)
                ldd[0, rows, :] = jnp.where(lane < HEAD_DIM // 2, lse_ref[rows, :], jnp.sum(d_o * o, axis=-1, keepdims=True))
                return dg + jnp.sum(dg_rows, axis=0, keepdims=True)
            dg_ref[...] += lax.fori_loop(0, t // chunk, prep, jnp.zeros((1, HEAD_DIM), F32))

            dod[0] = stg[...].astype(BF16)
            for bi, dil, sub in plan[1:]:
                for r in range(dil):
                    dst = pl.ds(r * sub, sub)
                    dod[bi, dst, :] = stg[pl.ds(r, sub, stride=dil), :].astype(BF16)
                    ldd[bi, dst, :] = ldd[0, pl.ds(r, sub, stride=dil), :]
            for src, dst, pad in ((q_ref, qd, 0), (k_ref, kd, BLK), (v_ref, vd, BLK)):
                stg[...] = src[...].astype(F32)
                for bi, dil, sub in plan:
                    if pad:
                        dst[bi, pl.ds(0, BLK), :] = jnp.zeros((BLK, HEAD_DIM), BF16)
                    if dil == 1:
                        dst[bi, pl.ds(pad, t), :] = src[...]
                    else:
                        for r in range(dil):
                            dst[bi, pl.ds(pad + r * sub, sub), :] = stg[pl.ds(r, sub, stride=dil), :].astype(BF16)

            for bi, dil, sub in plan:
                def blk(u, carry, bi=bi, sub=sub):
                    rows = pl.ds(pl.multiple_of(u * BLK, BLK), BLK)
                    win = pl.ds(pl.multiple_of(u * BLK, BLK), 2 * BLK)
                    qb, kw, vw, dob = qd[bi, rows, :], kd[bi, win, :], vd[bi, win, :], dod[bi, rows, :]
                    sc = _dot(qb, kw, NT) * scale
                    stats = ldd[bi, rows, :]
                    p = jnp.where(_band_mask(u, sub // BLK), jnp.exp(sc - stats[:, :1]), 0.0)
                    dp = _dot(dob, vw, NT)
                    ds = (p * (dp - stats[:, HEAD_DIM // 2:HEAD_DIM // 2 + 1]) * scale).astype(BF16)
                    dqd[bi, rows, :] = _dot(ds, kw, NN)
                    dk_win = _dot(ds, qb, TN)
                    dv_win = _dot(p.astype(BF16), dob, TN)
                    dkp[bi, rows, :] = dk_win[:BLK]
                    dkc[bi, rows, :] = dk_win[BLK:]
                    dvp[bi, rows, :] = dv_win[:BLK]
                    dvc[bi, rows, :] = dv_win[BLK:]
                    return carry
                lax.fori_loop(0, nb // DIL_UNROLL, lambda i, c, blk=blk: [blk(i * DIL_UNROLL + s, c) for s in range(DIL_UNROLL)][-1], 0)

            for cur, prev, undo_rope, dst in ((dqd, None, True, out_ref), (dkc, dkp, True, sk), (dvc, dvp, False, sv)):
                def summed(bi, start, size, cur=cur, prev=prev):
                    v = cur[bi, pl.ds(start, size), :]
                    if prev is None:
                        return v
                    if start + size < t:
                        return v + prev[bi, pl.ds(start + BLK, size), :]
                    if size == BLK:
                        return v
                    return v + jnp.concatenate([prev[bi, pl.ds(start + BLK, size - BLK), :], jnp.zeros((BLK, HEAD_DIM), F32)], axis=0)
                stg[...] = summed(0, 0, t)
                for bi, dil, sub in plan[1:]:
                    for r in range(dil):
                        stg[pl.ds(r, sub, stride=dil), :] += summed(bi, r * sub, sub)
                if undo_rope:
                    dst[...] = _rope_bwd(stg[...], c_ref[...], s1_ref[...], s2_ref[...]).astype(BF16)
                else:
                    dst[...] = stg[...].astype(BF16)

        @pl.when(which == 1)
        def _():
            out_ref[...] = sk[...]

        @pl.when(which == 2)
        def _():
            out_ref[...] = sv[...]

    hs = n_heads
    col = lambda off: pl.BlockSpec((t, HEAD_DIM), lambda h, b, w: (b, off + h))
    per_head = pl.BlockSpec((t, HEAD_DIM), lambda h, b, w: (b * hs + h, 0))
    tab = pl.BlockSpec((t, HEAD_DIM), lambda h, b, w: (b, 0))
    gvec = pl.BlockSpec((1, HEAD_DIM), lambda h, b, w: (0, h))
    tb = (t, HEAD_DIM)
    tp = (t + BLK, HEAD_DIM)
    return pl.pallas_call(
        body, name="dil_bwd", grid=(hs, bl, 3),
        in_specs=[col(0), col(hs), col(2 * hs), col(0), per_head, col(0), gvec, tab, tab, tab],
        out_specs=[pl.BlockSpec((t, HEAD_DIM), lambda h, b, w: (b, w * hs + h)), gvec],
        out_shape=[S((n, 6 * hs * HEAD_DIM), BF16), S((1, hs * HEAD_DIM), F32)],
        scratch_shapes=[pltpu.VMEM(tb, F32), pltpu.VMEM((3,) + tb, BF16), pltpu.VMEM((3,) + tp, BF16), pltpu.VMEM((3,) + tp, BF16),
                        pltpu.VMEM((3,) + tb, BF16), pltpu.VMEM((3,) + tb, F32), pltpu.VMEM((3,) + tb, F32),
                        pltpu.VMEM((3,) + tb, F32), pltpu.VMEM((3,) + tb, F32), pltpu.VMEM((3,) + tb, F32), pltpu.VMEM((3,) + tb, F32),
                        pltpu.VMEM(tb, BF16), pltpu.VMEM(tb, BF16)],
        compiler_params=_cp(3),
    )(proj, proj, proj, opre, lse, d_omix, g_dil, rc, rs1, rs2)


def _sb_tile(qb, kb, scale, shift):
    z = _dot(qb, kb, NT) * scale
    tl = jnp.log(1.0 + jnp.exp(-jnp.abs(z)))
    log_not = -(jnp.maximum(z, 0.0) + tl)
    log_beta = z + log_not
    strict = None
    if shift is not None:
        row = lax.broadcasted_iota(jnp.int32, (SBQ, SBK), 0)
        col = lax.broadcasted_iota(jnp.int32, (SBQ, SBK), 1)
        strict = col + shift < row
        log_not = jnp.where(strict, log_not, 0.0)
    return log_not, log_beta, strict


def _tri(cmp):
    row = lax.broadcasted_iota(jnp.int32, (SBK, SBK), 0)
    col = lax.broadcasted_iota(jnp.int32, (SBK, SBK), 1)
    return jnp.where(cmp(row, col), 1.0, 0.0).astype(BF16)


SB_EDGE = tuple(range(0, SBQ, SBK))


def _sb_fwd(proj, g_sb, omix_in, opre_in, bl, t, n_heads):
    nb = t // SBQ
    per_q = SBQ // SBK
    scale = HEAD_DIM ** -0.5

    def body(q_ref, k_ref, v_ref, g_ref, _omix_in, _opre_in, omix_ref, opre_ref, lt_ref):
        later = _tri(lambda r, c: r > c)

        def q_block(qi, carry):
            rows = pl.ds(pl.multiple_of(qi * SBQ, SBQ), SBQ)
            qb = q_ref[rows, :]

            def tile(kj, st, shift):
                run, acc = st
                krows = pl.ds(pl.multiple_of(kj * SBK, SBK), SBK)
                log_not, log_beta, strict = _sb_tile(qb, k_ref[krows, :], scale, shift)
                a = jnp.exp(log_beta + _scan_cols(log_not, later, 2) + run)
                if shift is not None:
                    a = jnp.where(strict, a, 0.0)
                return run + jnp.sum(log_not, axis=-1, keepdims=True), acc + _dot(a.astype(BF16), v_ref[krows, :], NN)

            st = (jnp.zeros((SBQ, 1), F32), jnp.zeros((SBQ, HEAD_DIM), F32))
            for shift in reversed(SB_EDGE):
                st = tile(qi * per_q + shift // SBK, st, shift)
            run, acc = lax.fori_loop(0, qi * per_q, lambda it, st: tile(qi * per_q - 1 - it, st, None), st)
            lt_ref[rows, :] = jnp.broadcast_to(run, (SBQ, HEAD_DIM))
            opre_ref[rows, :] = acc
            omix_ref[rows, :] = _head_norm_fwd(acc, g_ref[...]).astype(BF16)
            return carry

        lax.fori_loop(0, nb, q_block, 0)

    hs = n_heads
    col = lambda off: pl.BlockSpec((t, HEAD_DIM), lambda b, h: (b, off + h))
    return pl.pallas_call(
        body, name="sb_fwd", grid=(bl, hs),
        in_specs=[col(3 * hs), col(4 * hs), col(5 * hs), pl.BlockSpec((1, HEAD_DIM), lambda b, h: (0, h)), HBM_SPEC, HBM_SPEC],
        out_specs=[col(hs), col(hs), pl.BlockSpec((t, HEAD_DIM), lambda b, h: (b * hs + h, 0))],
        out_shape=[S(omix_in.shape, BF16), S(opre_in.shape, F32), S((bl * hs * t, HEAD_DIM), F32)],
        input_output_aliases={4: 0, 5: 1}, compiler_params=_cp(2),
    )(proj, proj, proj, g_sb, omix_in, opre_in)


def _sb_bwd(proj, opre, ltot, d_omix, g_sb, dproj_in, bl, t, n_heads):
    nb = t // SBQ
    per_q = SBQ // SBK
    scale = HEAD_DIM ** -0.5

    def body(q_ref, k_ref, v_ref, opre_ref, lt_ref, dy_ref, g_ref, _dproj_in, out_ref, dg_ref, dq, dk, dv):
        which = pl.program_id(2)

        @pl.when(jnp.logical_and(which == 0, pl.program_id(1) == 0))
        def _():
            dg_ref[...] = jnp.zeros_like(dg_ref)

        @pl.when(which == 0)
        def _():
            upto = _tri(lambda r, c: r <= c)
            before = _tri(lambda r, c: r < c)
            dk[...] = jnp.zeros_like(dk)
            dv[...] = jnp.zeros_like(dv)

            def q_block(qi, dg):
                rows = pl.ds(pl.multiple_of(qi * SBQ, SBQ), SBQ)
                qb = q_ref[rows, :]
                o = opre_ref[rows, :]
                d_o, dg_rows = _head_norm_bwd(o, g_ref[...], dy_ref[rows, :])
                dob = d_o.astype(BF16)
                lt = lt_ref[rows, :][:, :1]

                def tile(kj, st, shift):
                    run, grun, dq_acc = st
                    krows = pl.ds(pl.multiple_of(kj * SBK, SBK), SBK)
                    kb, vb = k_ref[krows, :], v_ref[krows, :]
                    log_not, log_beta, strict = _sb_tile(qb, kb, scale, shift)
                    excl = lt - (run + _scan_cols(log_not, upto, 2))
                    a = jnp.exp(log_beta + excl)
                    if shift is not None:
                        a = jnp.where(strict, a, 0.0)
                    g_a = a * _dot(dob, vb, NT)
                    g_before = grun + _scan_cols(g_a, before, 1)
                    dz = g_a - (g_a + g_before) * jnp.exp(log_beta)
                    if shift is not None:
                        dz = jnp.where(strict, dz, 0.0)
                    dzb = dz.astype(BF16)
                    dk[krows, :] += _dot(dzb, qb, TN)
                    dv[krows, :] += _dot(a.astype(BF16), dob, TN)
                    return (run + jnp.sum(log_not, axis=-1, keepdims=True), grun + jnp.sum(g_a, axis=-1, keepdims=True),
                            dq_acc + _dot(dzb, kb, NN))

                zero_col = jnp.zeros((SBQ, 1), F32)
                st = lax.fori_loop(0, qi * per_q, lambda kj, st: tile(kj, st, None),
                                   (zero_col, zero_col, jnp.zeros((SBQ, HEAD_DIM), F32)))
                for shift in SB_EDGE:
                    st = tile(qi * per_q + shift // SBK, st, shift)
                dq_acc = st[2]
                dq[rows, :] = (dq_acc * scale).astype(BF16)
                return dg + jnp.sum(dg_rows, axis=0, keepdims=True)

            dg_ref[...] += lax.fori_loop(0, nb, q_block, jnp.zeros((1, HEAD_DIM), F32))
            out_ref[...] = dq[...]

        @pl.when(which == 1)
        def _():
            out_ref[...] = (dk[...] * scale).astype(BF16)

        @pl.when(which == 2)
        def _():
            out_ref[...] = dv[...].astype(BF16)

    hs = n_heads
    col = lambda off: pl.BlockSpec((t, HEAD_DIM), lambda h, b, w: (b, off + h))
    per_head = pl.BlockSpec((t, HEAD_DIM), lambda h, b, w: (b * hs + h, 0))
    gvec = pl.BlockSpec((1, HEAD_DIM), lambda h, b, w: (0, h))
    tb = (t, HEAD_DIM)
    return pl.pallas_call(
        body, name="sb_bwd", grid=(hs, bl, 3),
        in_specs=[col(3 * hs), col(4 * hs), col(5 * hs), col(hs), per_head, col(hs), gvec, HBM_SPEC],
        out_specs=[pl.BlockSpec((t, HEAD_DIM), lambda h, b, w: (b, (3 + w) * hs + h)), gvec],
        out_shape=[S(dproj_in.shape, BF16), S((1, hs * HEAD_DIM), F32)],
        scratch_shapes=[pltpu.VMEM(tb, BF16), pltpu.VMEM(tb, F32), pltpu.VMEM(tb, F32)],
        input_output_aliases={7: 0}, compiler_params=_cp(3),
    )(proj, proj, proj, opre, ltot, d_omix, g_sb, dproj_in)


def _all_gather(shards):
    k_w = len(shards)

    def body(*refs):
        ins, outs = refs[:k_w], refs[k_w:2 * k_w]
        send_sems, recv_sems, local_sems = refs[2 * k_w:]
        x, y, c = lax.axis_index("x"), lax.axis_index("y"), lax.axis_index("c")
        me, sibling = (x, y, c), (x, y, 1 - c)
        chips = [(1 - x, y), (x, 1 - y), (1 - x, 1 - y)]

        def slot(dev):
            return 4 * dev[0] + 2 * dev[1] + dev[2]

        def copy(w, k, block, to, src=None):
            dst = outs[w].at[slot(block)]
            return pltpu.make_async_remote_copy(
                src_ref=dst if src is None else src, dst_ref=dst, send_sem=send_sems.at[w * 7 + k], recv_sem=recv_sems.at[w * 7 + k],
                device_id=to, device_id_type=MESH)

        mine = [pltpu.make_async_copy(ins[w], outs[w].at[slot(me)], local_sems.at[w]) for w in range(k_w)]
        first = []
        for w in range(k_w):
            mine[w].start()
            first.append(copy(w, 0, me, sibling, src=ins[w]))
            first += [copy(w, 1 + j, me, (*chip, c), src=ins[w]) for j, chip in enumerate(chips)]
        for cp in first:
            cp.start()
        passed = []
        for w in range(k_w):
            for j, chip in enumerate(chips):
                copy(w, 1 + j, (*chip, c), me).wait_recv()
                fwd = copy(w, 4 + j, (*chip, c), sibling)
                fwd.start()
                passed.append(fwd)
        for w in range(k_w):
            copy(w, 0, sibling, me).wait_recv()
            for j, chip in enumerate(chips):
                copy(w, 4 + j, (*chip, 1 - c), me).wait_recv()
        for cp in first + passed:
            cp.wait_send()
        for cp in mine:
            cp.wait()

    return pl.pallas_call(
        body, name="weights_all_gather", in_specs=[HBM_SPEC] * k_w, out_specs=[HBM_SPEC] * k_w,
        out_shape=[S((N_DEV,) + s.shape, s.dtype) for s in shards],
        scratch_shapes=[pltpu.SemaphoreType.DMA((7 * k_w,)), pltpu.SemaphoreType.DMA((7 * k_w,)), pltpu.SemaphoreType.DMA((k_w,))],
    )(*shards)


def _grad_exchange(grads, after):
    k_w = len(grads)

    def body(*refs):
        ins, outs = refs[:k_w], refs[k_w + 1:2 * k_w + 1]
        send_sems, recv_sems, local_sems = refs[2 * k_w + 1:]
        local, remote = _exchange_copies(ins, outs, send_sems, recv_sems, local_sems, False, ALL_PEERS)
        for cp in local + remote:
            cp.start()
        for cp in remote:
            cp.wait_send()
            cp.wait_recv()
        for cp in local:
            cp.wait()

    n_sem = (N_DEV - 1) * k_w
    return pl.pallas_call(
        body, name="grad_exchange", in_specs=[HBM_SPEC] * k_w + [pl.BlockSpec(memory_space=pl.ANY)], out_specs=[HBM_SPEC] * k_w,
        out_shape=[S(g.shape, g.dtype) for g in grads],
        scratch_shapes=[pltpu.SemaphoreType.DMA((n_sem,)), pltpu.SemaphoreType.DMA((n_sem,)), pltpu.SemaphoreType.DMA((k_w,))],
    )(*grads, after)


def _peer(x, y, c, k):
    px, py, pc = x ^ (k >> 2), y ^ ((k >> 1) & 1), c ^ (k & 1)
    return (px, py, pc), 4 * px + 2 * py + pc


ALL_PEERS = tuple(range(1, N_DEV))
SAME_CORE_PEERS = (2, 4, 6)


def _exchange_copies(srcs, lands, send_sems, recv_sems, local_sems, gather, peers):
    x, y, c = lax.axis_index("x"), lax.axis_index("y"), lax.axis_index("c")
    my_slot = 4 * x + 2 * y + c
    local, remote = [], []
    for w, (src, land) in enumerate(zip(srcs, lands)):
        local.append(pltpu.make_async_copy(src if gather else src.at[my_slot], land.at[my_slot], local_sems.at[w]))
        for i, k in enumerate(peers):
            peer, peer_slot = _peer(x, y, c, k)
            remote.append(pltpu.make_async_remote_copy(
                src_ref=src if gather else src.at[peer_slot], dst_ref=land.at[my_slot],
                send_sem=send_sems.at[w * len(peers) + i], recv_sem=recv_sems.at[w * len(peers) + i],
                device_id=peer, device_id_type=MESH))
    return local, remote


def _exchange_start(name, srcs, gather, after, peers=ALL_PEERS):
    k_w = len(srcs)
    land_shapes = [((N_DEV,) + s.shape) if gather else s.shape for s in srcs]

    def body(*refs):
        src_refs, land_refs = refs[:k_w], refs[k_w:2 * k_w]
        send_sems, recv_sems, local_sems = refs[2 * k_w + 1:2 * k_w + 4]
        token = refs[-1]
        local, remote = _exchange_copies(src_refs, land_refs, send_sems, recv_sems, local_sems, gather, peers)
        for cp in local + remote:
            cp.start()
        token[...] = jnp.zeros_like(token)

    n_sem = len(peers) * k_w
    hbm = lambda a: pltpu.with_memory_space_constraint(a, pltpu.HBM)
    outs = pl.pallas_call(
        body, name=name,
        in_specs=[HBM_SPEC] * (2 * k_w) + [pl.BlockSpec(memory_space=pl.ANY)],
        out_shape=(pltpu.SemaphoreType.DMA((n_sem,)), pltpu.SemaphoreType.DMA((n_sem,)), pltpu.SemaphoreType.DMA((k_w,)),
                   *[pltpu.HBM(s.shape, s.dtype) for s in srcs], *[pltpu.HBM(ls, s.dtype) for ls, s in zip(land_shapes, srcs)],
                   S((8, HEAD_DIM), F32)),
        out_specs=(SEM_SPEC, SEM_SPEC, SEM_SPEC, *[HBM_SPEC] * (2 * k_w), pl.BlockSpec(memory_space=pltpu.VMEM)),
        input_output_aliases={i: 3 + i for i in range(2 * k_w)},
        compiler_params=pltpu.CompilerParams(has_side_effects=SIDE_EFFECT),
    )(*[hbm(s) for s in srcs], *[hbm(lax.empty(ls, s.dtype)) for ls, s in zip(land_shapes, srcs)], after)
    return dict(sems=outs[:3], srcs=outs[3:3 + k_w], lands=outs[3 + k_w:3 + 2 * k_w], token_block=outs[-1], token=outs[-1][0, 0], gather=gather, peers=peers)


def _exchange_wait(name, handle, after):
    k_w = len(handle["srcs"])
    gather = handle["gather"]

    def body(*refs):
        src_refs, land_refs = refs[:k_w], refs[k_w:2 * k_w]
        send_sems, recv_sems, local_sems = refs[2 * k_w:2 * k_w + 3]
        local, remote = _exchange_copies(src_refs, land_refs, send_sems, recv_sems, local_sems, gather, handle["peers"])
        for cp in local:
            cp.wait()
        for cp in remote:
            cp.wait_send()
            cp.wait_recv()

    outs = pl.pallas_call(
        body, name=name,
        in_specs=[HBM_SPEC] * (2 * k_w) + [SEM_SPEC] * 3 + [pl.BlockSpec(memory_space=pl.ANY)],
        out_shape=tuple(pltpu.HBM(a.shape, a.dtype) for a in (*handle["srcs"], *handle["lands"])),
        out_specs=tuple([HBM_SPEC] * (2 * k_w)),
        input_output_aliases={i: i for i in range(2 * k_w)},
        compiler_params=pltpu.CompilerParams(has_side_effects=SIDE_EFFECT),
    )(*handle["srcs"], *handle["lands"], *handle["sems"], after)
    return outs[k_w:]


CHIPS = ((0, 0), (0, 1), (1, 0), (1, 1))


def _swap_copies(lands, send_sems, recv_sems):
    x, y, c = lax.axis_index("x"), lax.axis_index("y"), lax.axis_index("c")
    copies = []
    for w, land in enumerate(lands):
        for i, (px, py) in enumerate(CHIPS):
            mine = land.at[4 * px + 2 * py + c]
            copies.append(pltpu.make_async_remote_copy(
                src_ref=mine, dst_ref=mine, send_sem=send_sems.at[w * len(CHIPS) + i], recv_sem=recv_sems.at[w * len(CHIPS) + i],
                device_id=(x, y, 1 - c), device_id_type=MESH))
    return copies


def _swap_start(name, lands, after):
    k_w = len(lands)

    def body(*refs):
        land_refs = refs[:k_w]
        send_sems, recv_sems = refs[k_w + 1:k_w + 3]
        for cp in _swap_copies(land_refs, send_sems, recv_sems):
            cp.start()
        refs[-1][...] = jnp.zeros_like(refs[-1])

    n_sem = len(CHIPS) * k_w
    outs = pl.pallas_call(
        body, name=name, in_specs=[HBM_SPEC] * k_w + [pl.BlockSpec(memory_space=pl.ANY)],
        out_shape=(pltpu.SemaphoreType.DMA((n_sem,)), pltpu.SemaphoreType.DMA((n_sem,)),
                   *[pltpu.HBM(a.shape, a.dtype) for a in lands], S((8, HEAD_DIM), F32)),
        out_specs=(SEM_SPEC, SEM_SPEC, *[HBM_SPEC] * k_w, pl.BlockSpec(memory_space=pltpu.VMEM)),
        input_output_aliases={i: 2 + i for i in range(k_w)},
        compiler_params=pltpu.CompilerParams(has_side_effects=SIDE_EFFECT),
    )(*lands, after)
    return dict(sems=outs[:2], lands=outs[2:2 + k_w], token_block=outs[-1])


def _swap_wait(name, handle, after):
    k_w = len(handle["lands"])

    def body(*refs):
        for cp in _swap_copies(refs[:k_w], refs[k_w], refs[k_w + 1]):
            cp.wait_send()
            cp.wait_recv()

    return pl.pallas_call(
        body, name=name, in_specs=[HBM_SPEC] * k_w + [SEM_SPEC] * 2 + [pl.BlockSpec(memory_space=pl.ANY)],
        out_shape=tuple(pltpu.HBM(a.shape, a.dtype) for a in handle["lands"]), out_specs=tuple([HBM_SPEC] * k_w),
        input_output_aliases={i: i for i in range(k_w)},
        compiler_params=pltpu.CompilerParams(has_side_effects=SIDE_EFFECT),
    )(*handle["lands"], *handle["sems"], after)


def _adamw(name, parts, w, m, v):
    r, c = w.shape
    tr = _tile(r, max(16, (1 << 19) // c // 16 * 16))

    def body(p_ref, w_ref, m_ref, v_ref, g_ref, d_ref, nm_ref, nv_ref):
        g = p_ref[0].astype(F32)
        for s in range(1, N_DEV):
            g = g + p_ref[s].astype(F32)
        m_new = ADAM_B1 * m_ref[...] + (1.0 - ADAM_B1) * g
        v_new = ADAM_B2 * v_ref[...] + (1.0 - ADAM_B2) * jnp.square(g)
        m_hat = m_new / (1.0 - ADAM_B1 ** ADAM_STEP)
        v_hat = v_new / (1.0 - ADAM_B2 ** ADAM_STEP)
        g_ref[...] = g
        d_ref[...] = -ADAM_LR * (m_hat / (jnp.sqrt(v_hat) + ADAM_EPS) + ADAM_WD * w_ref[...])
        nm_ref[...] = m_new
        nv_ref[...] = v_new

    blk = pl.BlockSpec((tr, c), lambda i: (i, 0))
    return pl.pallas_call(body, name=name, grid=(r // tr,), in_specs=[pl.BlockSpec((N_DEV, tr, c), lambda i: (0, i, 0)), blk, blk, blk],
                          out_specs=[blk] * 4, out_shape=[S((r, c), F32)] * 4, compiler_params=_cp(1))(parts, w, m, v)


def _rope_tables(positions):
    inv_freq = jnp.power(jnp.float32(ROPE_THETA), -jnp.arange(ROPE_HALF, dtype=F32) / ROPE_HALF)
    ang = positions.astype(F32).reshape(-1, 1) * inv_freq
    cos, sin = jnp.cos(ang), jnp.sin(ang)
    n = ang.shape[0]
    rest = HEAD_DIM - 2 * ROPE_HALF
    zeros = jnp.zeros((n, ROPE_HALF), F32)
    c = jnp.concatenate([cos, cos, jnp.ones((n, rest), F32)], axis=1)
    s1 = jnp.concatenate([zeros, sin, jnp.zeros((n, rest), F32)], axis=1)
    s2 = jnp.concatenate([-sin, zeros, jnp.zeros((n, rest), F32)], axis=1)
    return c, s1, s2


def kernel(x, positions, norm_mix_g, w_in, norm_out_dil_g, norm_out_sb_g, w_out, norm_ffn_g, w_gate, w_up, w_down, norm_final_g, loss_target, m_norm_mix_g, m_w_in, m_norm_out_dil_g, m_norm_out_sb_g, m_w_out, m_norm_ffn_g, m_w_gate, m_w_up, m_w_down, m_norm_final_g, v_norm_mix_g, v_w_in, v_norm_out_dil_g, v_norm_out_sb_g, v_w_out, v_norm_ffn_g, v_w_gate, v_w_up, v_w_down, v_norm_final_g):
    bl, t, d = x.shape
    n = bl * t
    hs = d // (2 * HEAD_DIM)
    x2 = x.reshape(n, d)
    target = loss_target.reshape(n, d)
    g_final = norm_final_g.reshape(1, d)
    rc, rs1, rs2 = _rope_tables(positions)

    (win_all,) = _all_gather([w_in[0].astype(BF16)])
    ex_wout = _exchange_start("wout_gather_start", [w_out[0].astype(BF16)], True, win_all, SAME_CORE_PEERS)
    tr = lambda a: jnp.swapaxes(a[0], 0, 1)
    ex_wgu = _exchange_start("wgu_gather_start", [tr(w_gate).astype(BF16), tr(w_up).astype(BF16)], True, ex_wout["token_block"], SAME_CORE_PEERS)
    ex_wd = _exchange_start("wd_gather_start", [w_down[0].astype(BF16)], True, ex_wgu["token_block"], SAME_CORE_PEERS)
    rc = rc + ex_wd["token"]

    hn1 = _rms_fwd("rms_mix_fwd", x2, norm_mix_g)
    proj = _proj_fwd(hn1, win_all, rc, rs1, rs2, 2 * hs)
    omix, opre, lse = _dil_fwd(proj, norm_out_dil_g, bl, t, hs)
    omix, opre, ltot = _sb_fwd(proj, norm_out_sb_g, omix, opre, bl, t, hs)
    (wout_half,) = _exchange_wait("wout_gather_wait", ex_wout, ltot)
    wg_half, wu_half = _exchange_wait("wgu_gather_wait", ex_wgu, wout_half)
    sw_wout = _swap_start("wout_swap_start", [wout_half], wg_half)
    sw_wgu = _swap_start("wgu_swap_start", [wg_half, wu_half], sw_wout["token_block"])
    (wout_all,) = _swap_wait("wout_swap_wait", sw_wout, sw_wgu["token_block"])
    wout_full = wout_all.reshape(d, d)
    h1 = _dense_res("out_fwd", omix, wout_full, x2, NN)
    hn2 = _rms_fwd("rms_ffn_fwd", h1, norm_ffn_g)
    wg_all, wu_all = _swap_wait("wgu_swap_wait", sw_wgu, hn2)
    gate = _gate_fwd(hn2, wg_all)
    (wd_half,) = _exchange_wait("wd_gather_wait", ex_wd, gate)
    sw_wd = _swap_start("wd_swap_start", [wd_half], gate)
    up, act = _up_fwd(hn2, wu_all, gate, sw_wd["token_block"])
    (wd_all,) = _swap_wait("wd_swap_wait", sw_wd, act)
    h2 = _rows_fwd("down_fwd", (act,), (wd_all,), h1)
    dh2, dh2b, dg_final, sse = _final_loss(h2, g_final, target)
    loss = lax.psum(sse[0, 0], ("x", "y", "c")) * (0.5 / d)

    dgate, dup = _dact_bwd(dh2b, wd_all, gate, up)
    dwd = _dw_rows_bwd("dwd_bwd", act, dh2b)
    ex_dwd = _exchange_start("dwd_exchange_start", [dwd], False, dgate)
    dwg = _dw_rows_bwd("dwg_bwd", dgate, hn2, ex_dwd["token_block"])
    dwu = _dw_rows_bwd("dwu_bwd", dup, hn2)
    ex_dwgu = _exchange_start("dwgu_exchange_start", [dwg, dwu], False, ex_dwd["token_block"])
    dhn2 = _rows_fwd("dhn2_bwd", (dgate, dup), (wg_all, wu_all), None, ex_dwgu["token_block"])
    dh1, dh1b, dg_ffn = _rms_bwd("rms_ffn_bwd", h1, norm_ffn_g, dhn2, dh2)
    d_omix = _dense_res("domix_bwd", dh1b, wout_full, None, NT)
    dwout = _tn_full("dwout_bwd", omix, dh1b).reshape(N_DEV, d // N_DEV, d)
    ex_dwout = _exchange_start("dwout_exchange_start", [dwout], False, d_omix)
    dproj, dg_dil = _dil_bwd(proj, opre, lse, d_omix, norm_out_dil_g + ex_dwout["token"], rc, rs1, rs2, bl, t, hs)
    dproj, dg_sb = _sb_bwd(proj, opre, ltot, d_omix, norm_out_sb_g, dproj, bl, t, hs)
    dwin = _dwin_bwd(hn1, dproj)
    ex_dwin = _exchange_start("dwin_exchange_start", [dwin], False, dproj)
    dhn1 = _dhn_from_shards("dhn1_bwd", (dproj,), (win_all,), True, ex_dwin["token_block"])
    dx, _, dg_mix = _rms_bwd("rms_mix_bwd", x2, norm_mix_g, dhn1, dh1)

    gains = [norm_mix_g, norm_out_dil_g, norm_out_sb_g, norm_ffn_g, g_final]
    m_gains = [m_norm_mix_g, m_norm_out_dil_g, m_norm_out_sb_g, m_norm_ffn_g, m_norm_final_g.reshape(1, d)]
    v_gains = [v_norm_mix_g, v_norm_out_dil_g, v_norm_out_sb_g, v_norm_ffn_g, v_norm_final_g.reshape(1, d)]
    dg_vec = jnp.concatenate([dg_mix, dg_dil, dg_sb, dg_ffn, dg_final], axis=1)
    dg_all = jnp.broadcast_to(dg_vec[None], (N_DEV,) + dg_vec.shape)

    out_w = {}
    (rwd,) = _exchange_wait("dwd_exchange_wait", ex_dwd, dx)
    out_w["w_down"] = _adamw("adamw_w_down", rwd, w_down[0], m_w_down[0], v_w_down[0])
    rwg, rwu = _exchange_wait("dwgu_exchange_wait", ex_dwgu, out_w["w_down"][0])
    gate_t = _adamw("adamw_w_gate", rwg, tr(w_gate), tr(m_w_gate), tr(v_w_gate))
    up_t = _adamw("adamw_w_up", rwu, tr(w_up), tr(m_w_up), tr(v_w_up))
    out_w["w_gate"] = [jnp.swapaxes(o, 0, 1) for o in gate_t]
    out_w["w_up"] = [jnp.swapaxes(o, 0, 1) for o in up_t]
    (rwout,) = _exchange_wait("dwout_exchange_wait", ex_dwout, up_t[0])
    out_w["w_out"] = _adamw("adamw_w_out", rwout, w_out[0], m_w_out[0], v_w_out[0])
    (rwin,) = _exchange_wait("dwin_exchange_wait", ex_dwin, out_w["w_out"][0])
    out_w["w_in"] = _adamw("adamw_w_in", rwin, w_in[0], m_w_in[0], v_w_in[0])
    (rg,) = _grad_exchange([dg_all], out_w["w_in"][0])
    out_w = {name: [o[None] for o in outs] for name, outs in out_w.items()}
    cat = lambda vs: jnp.concatenate(vs, axis=1)
    gain_out = _adamw("adamw_gains", rg, cat(gains), cat(m_gains), cat(v_gains))
    widths = [d, d // 2, d // 2, d]
    cuts = [sum(widths[:i + 1]) for i in range(4)]
    gain_split = [jnp.split(o, cuts, axis=1) for o in gain_out]

    def ordered(kind):
        gs = gain_split[kind]
        return (gs[0], out_w["w_in"][kind], gs[1], gs[2], out_w["w_out"][kind], gs[3], out_w["w_gate"][kind],
                out_w["w_up"][kind], out_w["w_down"][kind], gs[4].reshape(d))

    return (loss, dx.reshape(bl, t, d), *ordered(0), *ordered(1), *ordered(2), *ordered(3))
```

```python
import functools
import math

import jax
import jax.numpy as jnp
from jax import lax
from jax.experimental import pallas as pl
from jax.experimental.pallas import tpu as pltpu

F32 = jnp.float32
BF16 = jnp.bfloat16
S = jax.ShapeDtypeStruct

N_DEV = 8
HEAD_DIM = 128
BLK = 128
SBQ = 1024
SBK = 256
DIL_UNROLL = 16
ROPE_HALF = 16
ROPE_THETA = 500000.0
RMS_EPS = 1e-5
DILATIONS = (1, 4, 16)
NEG = -1e30
VMEM_LIMIT = 56 * 1024 * 1024

ADAM_LR = 0.001
ADAM_B1 = 0.9
ADAM_B2 = 0.999
ADAM_EPS = 1e-08
ADAM_WD = 0.01
ADAM_STEP = 10

MESH = pl.DeviceIdType.MESH
HBM_SPEC = pl.BlockSpec(memory_space=pltpu.HBM)
SEM_SPEC = pl.BlockSpec(memory_space=pltpu.SEMAPHORE)
SIDE_EFFECT = pltpu.SideEffectType.DATAFLOW_SIDE_EFFECTING


def _cp(n_axes):
    return pltpu.CompilerParams(dimension_semantics=("arbitrary",) * n_axes, vmem_limit_bytes=VMEM_LIMIT)


def _tile(n, want):
    if n <= want:
        return n
    t = want
    while t >= 16:
        if n % t == 0 and t % 16 == 0:
            return t
        t -= 16
    return n


NN = (((1,), (0,)), ((), ()))
NT = (((1,), (1,)), ((), ()))
TN = (((0,), (0,)), ((), ()))
ROW_TILE = 512


def _matmul(name, grid, red_axis, ins, in_specs, terms, dims, acc_shapes, out_shapes, out_specs, epilogue, after=None):
    if after is not None:
        ins, in_specs = (*ins, after), [*in_specs, pl.BlockSpec(memory_space=pl.ANY)]
    n_in, n_out = len(ins), len(out_shapes)
    n_red = grid[red_axis]

    def body(*refs):
        in_refs, out_refs, acc_refs = refs[:n_in], refs[n_in:n_in + n_out], refs[n_in + n_out:]
        sums = {}
        for a_idx, a_sl, b_idx, b_sl, acc_idx in terms:
            a = (in_refs[a_idx][...] if a_sl is None else in_refs[a_idx][a_sl]).astype(BF16)
            b = (in_refs[b_idx][...] if b_sl is None else in_refs[b_idx][b_sl]).astype(BF16)
            prod = lax.dot_general(a, b, dims, preferred_element_type=F32)
            sums[acc_idx] = prod if acc_idx not in sums else sums[acc_idx] + prod
        if n_red == 1:
            for idx, v in sums.items():
                acc_refs[idx][...] = v
            epilogue(acc_refs, in_refs, out_refs)
            return
        k = pl.program_id(red_axis)

        @pl.when(k == 0)
        def _():
            for idx, v in sums.items():
                acc_refs[idx][...] = v

        @pl.when(k > 0)
        def _():
            for idx, v in sums.items():
                acc_refs[idx][...] += v

        @pl.when(k == n_red - 1)
        def _():
            epilogue(acc_refs, in_refs, out_refs)

    return pl.pallas_call(
        body, name=name, grid=grid, in_specs=in_specs, out_specs=out_specs, out_shape=out_shapes,
        scratch_shapes=[pltpu.VMEM(s, F32) for s in acc_shapes], compiler_params=_cp(len(grid)),
    )(*ins)


def _store_epilogue(acc_refs, in_refs, out_refs):
    for acc, out in zip(acc_refs, out_refs):
        out[...] = acc[...].astype(out.dtype)


def _rope_fwd(a, c, s1, s2):
    return a * c + pltpu.roll(a, ROPE_HALF, 1) * s1 + pltpu.roll(a, HEAD_DIM - ROPE_HALF, 1) * s2


def _rope_bwd(d, c, s1, s2):
    return d * c + pltpu.roll(d * s1, HEAD_DIM - ROPE_HALF, 1) + pltpu.roll(d * s2, ROPE_HALF, 1)


def _proj_fwd(hn, w_all, rc, rs1, rs2, n_rope_heads):
    n, d = hn.shape
    _, _, ws = w_all.shape
    tm = _tile(n, ROW_TILE)
    heads_per_shard = ws // HEAD_DIM
    rows = _tile(tm, 256)

    def epilogue(acc_refs, in_refs, out_refs):
        acc, out = acc_refs[0], out_refs[0]
        j = pl.program_id(0)
        for r0 in range(0, tm, rows):
            c, s1, s2 = (ref[pl.ds(r0, rows), :] for ref in in_refs[2:5])
            for hh in range(heads_per_shard):
                a = acc[pl.ds(r0, rows), pl.ds(hh * HEAD_DIM, HEAD_DIM)]
                roped = _rope_fwd(a, c, s1, s2)
                a = jnp.where(j * heads_per_shard + hh < n_rope_heads, roped, a)
                out[pl.ds(r0, rows), pl.ds(hh * HEAD_DIM, HEAD_DIM)] = a.astype(out.dtype)

    tab = pl.BlockSpec((tm, HEAD_DIM), lambda j, m, k: (m, 0))
    return _matmul(
        "proj_fwd", (N_DEV, n // tm, 1), 2, (hn, w_all, rc, rs1, rs2),
        [pl.BlockSpec((tm, d), lambda j, m, k: (m, 0)), pl.BlockSpec((None, d, ws), lambda j, m, k: (j, 0, 0)), tab, tab, tab],
        [(0, None, 1, None, 0)], NN, [(tm, ws)], [S((n, N_DEV * ws), BF16)], [pl.BlockSpec((tm, ws), lambda j, m, k: (m, j))], epilogue)[0]


def _dense_res(name, a, b, res, dims, out_dtype=F32):
    m, kdim = a.shape
    n = b.shape[1] if dims == NN else b.shape[0]
    tm, tn = _tile(m, ROW_TILE), _tile(n, 1024)
    ins = [a, b] + ([res] if res is not None else [])
    b_spec = pl.BlockSpec((kdim, tn), lambda j, i, k: (0, j)) if dims == NN else pl.BlockSpec((tn, kdim), lambda j, i, k: (j, 0))
    specs = [pl.BlockSpec((tm, kdim), lambda j, i, k: (i, 0)), b_spec]
    if res is not None:
        specs.append(pl.BlockSpec((tm, tn), lambda j, i, k: (i, j)))

    def epilogue(acc_refs, in_refs, out_refs):
        v = acc_refs[0][...]
        if res is not None:
            v = v + in_refs[2][...]
        out_refs[0][...] = v.astype(out_dtype)

    return _matmul(name, (n // tn, m // tm, 1), 2, ins, specs, [(0, None, 1, None, 0)], dims, [(tm, tn)],
                   [S((m, n), out_dtype)], [pl.BlockSpec((tm, tn), lambda j, i, k: (i, j))], epilogue)[0]


def _tn_full(name, a, b, out_dtype=BF16):
    m, kdim = a.shape
    n = b.shape[1]
    tk, tn = _tile(kdim, 512), _tile(n, 1024)
    return _matmul(name, (kdim // tk, n // tn, 1), 2, (a, b),
                   [pl.BlockSpec((m, tk), lambda i, j, t: (0, i)), pl.BlockSpec((m, tn), lambda i, j, t: (0, j))],
                   [(0, None, 1, None, 0)], TN, [(tk, tn)], [S((kdim, n), out_dtype)], [pl.BlockSpec((tk, tn), lambda i, j, t: (i, j))],
                   _store_epilogue)[0]


def _gate_fwd(hn, wgt_all, after=None):
    n, d = hn.shape
    _, fs, _ = wgt_all.shape
    tm = _tile(n, ROW_TILE)
    o_spec = pl.BlockSpec((None, tm, fs), lambda j, m, k: (j, m, 0))
    return _matmul("gate_fwd", (N_DEV, n // tm, 1), 2, (hn, wgt_all),
                   [pl.BlockSpec((tm, d), lambda j, m, k: (m, 0)), pl.BlockSpec((None, fs, d), lambda j, m, k: (j, 0, 0))],
                   [(0, None, 1, None, 0)], NT, [(tm, fs)], [S((N_DEV, n, fs), BF16)], [o_spec], _store_epilogue, after)[0]


def _up_fwd(hn, wut_all, gate, after=None):
    n, d = hn.shape
    _, fs, _ = wut_all.shape
    tm = _tile(n, ROW_TILE)
    rows = _tile(tm, 256)

    def epilogue(acc_refs, in_refs, out_refs):
        for r0 in range(0, tm, rows):
            u = acc_refs[0][pl.ds(r0, rows), :]
            g = in_refs[2][pl.ds(r0, rows), :].astype(F32)
            out_refs[0][pl.ds(r0, rows), :] = u.astype(BF16)
            out_refs[1][pl.ds(r0, rows), :] = (g * jax.nn.sigmoid(g) * u).astype(BF16)

    t_spec = pl.BlockSpec((None, tm, fs), lambda j, m, k: (j, m, 0))
    o_shape = S((N_DEV, n, fs), BF16)
    return _matmul("up_fwd", (N_DEV, n // tm, 1), 2, (hn, wut_all, gate),
                   [pl.BlockSpec((tm, d), lambda j, m, k: (m, 0)), pl.BlockSpec((None, fs, d), lambda j, m, k: (j, 0, 0)), t_spec],
                   [(0, None, 1, None, 0)], NT, [(tm, fs)], [o_shape] * 2, [t_spec] * 2, epilogue, after)


def _rows_fwd(name, ys, ws_all, res, after=None):
    _, n, fs = ys[0].shape
    d = ws_all[0].shape[2]
    k_terms = len(ys)
    group = 4 // k_terms
    tm, tn = _tile(n, ROW_TILE), _tile(d, 1024)
    ins = [*ys, *ws_all] + ([res] if res is not None else [])
    specs = ([pl.BlockSpec((group, tm, fs), lambda i, j, s: (s, i, 0))] * k_terms
             + [pl.BlockSpec((group, fs, tn), lambda i, j, s: (s, 0, j))] * k_terms)
    if res is not None:
        specs.append(pl.BlockSpec((tm, tn), lambda i, j, s: (i, j)))

    def epilogue(acc_refs, in_refs, out_refs):
        v = acc_refs[0][...]
        out_refs[0][...] = v if res is None else v + in_refs[2 * k_terms][...]

    return _matmul(name, (n // tm, d // tn, N_DEV // group), 2, ins, specs,
                   [(i, q, k_terms + i, q, 0) for i in range(k_terms) for q in range(group)], NN, [(tm, tn)], [S((n, d), F32)],
                   [pl.BlockSpec((tm, tn), lambda i, j, s: (i, j))], epilogue, after)[0]


def _dact_bwd(dh, wd_all, gate, up):
    n, d = dh.shape
    _, fs, _ = wd_all.shape
    tm = _tile(n, ROW_TILE)
    rows = _tile(tm, 256)

    def epilogue(acc_refs, in_refs, out_refs):
        for r0 in range(0, tm, rows):
            da = acc_refs[0][pl.ds(r0, rows), :]
            g = in_refs[2][pl.ds(r0, rows), :].astype(F32)
            u = in_refs[3][pl.ds(r0, rows), :].astype(F32)
            sg = jax.nn.sigmoid(g)
            out_refs[0][pl.ds(r0, rows), :] = (da * u * (sg * (1.0 + g * (1.0 - sg)))).astype(BF16)
            out_refs[1][pl.ds(r0, rows), :] = (da * (g * sg)).astype(BF16)

    t_spec = pl.BlockSpec((None, tm, fs), lambda j, m, k: (j, m, 0))
    o_shape = S((N_DEV, n, fs), BF16)
    return _matmul("dact_bwd", (N_DEV, n // tm, 1), 2, (dh, wd_all, gate, up),
                   [pl.BlockSpec((tm, d), lambda j, m, k: (m, 0)), pl.BlockSpec((None, fs, d), lambda j, m, k: (j, 0, 0)), t_spec, t_spec],
                   [(0, None, 1, None, 0)], NT, [(tm, fs)], [o_shape] * 2, [t_spec] * 2, epilogue)


def _dw_rows_bwd(name, act, dh, after=None):
    _, n, fs = act.shape
    d = dh.shape[1]
    tn = _tile(d, 512)
    return _matmul(name, (N_DEV, d // tn, 1), 2, (act, dh),
                   [pl.BlockSpec((None, n, fs), lambda j, c, t: (j, 0, 0)), pl.BlockSpec((n, tn), lambda j, c, t: (0, c))],
                   [(0, None, 1, None, 0)], TN, [(fs, tn)], [S((N_DEV, fs, d), BF16)], [pl.BlockSpec((None, fs, tn), lambda j, c, t: (j, 0, c))],
                   _store_epilogue, after)[0]


def _dwin_bwd(hn, dproj):
    n, d = hn.shape
    ws = dproj.shape[1] // N_DEV
    tk = _tile(d, 512)
    return _matmul("dwin_bwd", (N_DEV, d // tk, 1), 2, (hn, dproj),
                   [pl.BlockSpec((n, tk), lambda j, c, t: (0, c)), pl.BlockSpec((n, ws), lambda j, c, t: (0, j))],
                   [(0, None, 1, None, 0)], TN, [(tk, ws)], [S((N_DEV, d, ws), BF16)], [pl.BlockSpec((None, tk, ws), lambda j, c, t: (j, c, 0))],
                   _store_epilogue)[0]


def _dhn_from_shards(name, dys, ws_all, dy_is_flat, after=None):
    if dy_is_flat:
        n, ws = dys[0].shape[0], dys[0].shape[1] // N_DEV
    else:
        _, n, ws = dys[0].shape
    d = ws_all[0].shape[1]
    k_terms = len(dys)
    group = 4 // k_terms
    tm, tn = _tile(n, ROW_TILE), _tile(d, 1024)
    if dy_is_flat:
        y_spec = pl.BlockSpec((tm, group * ws), lambda i, j, s: (i, s))
        y_sl = lambda q: (slice(None), slice(q * ws, (q + 1) * ws))
    else:
        y_spec = pl.BlockSpec((group, tm, ws), lambda i, j, s: (s, i, 0))
        y_sl = lambda q: q
    w_spec = pl.BlockSpec((group, tn, ws), lambda i, j, s: (s, j, 0))
    return _matmul(name, (n // tm, d // tn, N_DEV // group), 2, (*dys, *ws_all), [y_spec] * k_terms + [w_spec] * k_terms,
                   [(i, y_sl(q), k_terms + i, q, 0) for i in range(k_terms) for q in range(group)], NT, [(tm, tn)], [S((n, d), F32)],
                   [pl.BlockSpec((tm, tn), lambda i, j, s: (i, j))], _store_epilogue, after)[0]


def _rms_fwd(name, x, g):
    n, d = x.shape
    tm = _tile(n, 256)

    def body(x_ref, g_ref, o_ref):
        xv = x_ref[...]
        r = lax.rsqrt(jnp.mean(xv * xv, axis=-1, keepdims=True) + RMS_EPS)
        o_ref[...] = (xv * r * g_ref[...]).astype(BF16)

    return pl.pallas_call(body, name=name, grid=(n // tm,),
                          in_specs=[pl.BlockSpec((tm, d), lambda i: (i, 0)), pl.BlockSpec((1, d), lambda i: (0, 0))],
                          out_specs=pl.BlockSpec((tm, d), lambda i: (i, 0)), out_shape=S((n, d), BF16), compiler_params=_cp(1))(x, g)


def _rms_bwd(name, x, g, dy, res):
    n, d = x.shape
    tm = _tile(n, 256)

    def body(x_ref, g_ref, dy_ref, res_ref, dx_ref, dxb_ref, dg_ref):
        xv, dyv = x_ref[...], dy_ref[...]
        r = lax.rsqrt(jnp.mean(xv * xv, axis=-1, keepdims=True) + RMS_EPS)
        xr = xv * r
        dgy = dyv * g_ref[...]
        dx = res_ref[...] + r * (dgy - xr * jnp.mean(dgy * xr, axis=-1, keepdims=True))
        dx_ref[...] = dx
        dxb_ref[...] = dx.astype(BF16)

        @pl.when(pl.program_id(0) == 0)
        def _():
            dg_ref[...] = jnp.zeros_like(dg_ref)

        dg_ref[...] += jnp.sum(dyv * xr, axis=0, keepdims=True)

    row = pl.BlockSpec((tm, d), lambda i: (i, 0))
    vec = pl.BlockSpec((1, d), lambda i: (0, 0))
    return pl.pallas_call(body, name=name, grid=(n // tm,), in_specs=[row, vec, row, row], out_specs=[row, row, vec],
                          out_shape=[S((n, d), F32), S((n, d), BF16), S((1, d), F32)], compiler_params=_cp(1))(x, g, dy, res)


def _final_loss(h, g, target):
    n, d = h.shape
    tm = _tile(n, 256)

    def body(h_ref, g_ref, t_ref, dh_ref, dhb_ref, dg_ref, sse_ref):
        hv, gv = h_ref[...], g_ref[...]
        r = lax.rsqrt(jnp.mean(hv * hv, axis=-1, keepdims=True) + RMS_EPS)
        hr = hv * r
        err = hr * gv - t_ref[...]
        dy = err * (1.0 / d)
        dgy = dy * gv
        dh = r * (dgy - hr * jnp.mean(dgy * hr, axis=-1, keepdims=True))
        dh_ref[...] = dh
        dhb_ref[...] = dh.astype(BF16)

        @pl.when(pl.program_id(0) == 0)
        def _():
            dg_ref[...] = jnp.zeros_like(dg_ref)
            sse_ref[...] = jnp.zeros_like(sse_ref)

        dg_ref[...] += jnp.sum(dy * hr, axis=0, keepdims=True)
        sse_ref[...] += jnp.sum(err * err)

    row = pl.BlockSpec((tm, d), lambda i: (i, 0))
    vec = pl.BlockSpec((1, d), lambda i: (0, 0))
    one = pl.BlockSpec((8, HEAD_DIM), lambda i: (0, 0))
    return pl.pallas_call(body, name="final_loss", grid=(n // tm,), in_specs=[row, vec, row], out_specs=[row, row, vec, one],
                          out_shape=[S((n, d), F32), S((n, d), BF16), S((1, d), F32), S((8, HEAD_DIM), F32)],
                          compiler_params=_cp(1))(h, g, target)


def _dot(a, b, dims):
    return lax.dot_general(a, b, dims, preferred_element_type=F32)


def _split3(x):
    hi = x.astype(BF16)
    r1 = x - hi.astype(F32)
    mid = r1.astype(BF16)
    lo = (r1 - mid.astype(F32)).astype(BF16)
    return hi, mid, lo


def _scan_cols(x, tri, terms):
    parts = _split3(x)[:terms]
    out = _dot(parts[0], tri, NN)
    for p in parts[1:]:
        out = out + _dot(p, tri, NN)
    return out


def _head_norm_fwd(o, g):
    r = lax.rsqrt(jnp.mean(o * o, axis=-1, keepdims=True) + RMS_EPS)
    return o * r * g


def _head_norm_bwd(o, g, dy):
    r = lax.rsqrt(jnp.mean(o * o, axis=-1, keepdims=True) + RMS_EPS)
    orr = o * r
    dgy = dy * g
    return r * (dgy - orr * jnp.mean(dgy * orr, axis=-1, keepdims=True)), dy * orr


def _interleave_plan(t):
    return [(i, dil, t // dil) for i, dil in enumerate(DILATIONS)]


def _band_mask(u, blocks_per_seq):
    row = lax.broadcasted_iota(jnp.int32, (BLK, 2 * BLK), 0)
    col = lax.broadcasted_iota(jnp.int32, (BLK, 2 * BLK), 1)
    dist = row + BLK - col
    has_prev = (u % blocks_per_seq) != 0
    return (dist >= 0) & (dist <= BLK) & ((col >= BLK) | has_prev)


def _dil_fwd(proj, g_dil, bl, t, n_heads):
    n = bl * t
    nb = t // BLK
    scale = HEAD_DIM ** -0.5
    plan = _interleave_plan(t)
    chunk = _tile(t, 256)

    def body(q_ref, k_ref, v_ref, g_ref, omix_ref, opre_ref, lse_ref, stg, qd, kd, vd, ob, lb, on, ln):
        for src, dst, pad in ((q_ref, qd, 0), (k_ref, kd, BLK), (v_ref, vd, BLK)):
            stg[...] = src[...].astype(F32)
            for bi, dil, sub in plan:
                if pad:
                    dst[bi, pl.ds(0, BLK), :] = jnp.zeros((BLK, HEAD_DIM), BF16)
                if dil == 1:
                    dst[bi, pl.ds(pad, t), :] = src[...]
                else:
                    for r in range(dil):
                        dst[bi, pl.ds(pad + r * sub, sub), :] = stg[pl.ds(r, sub, stride=dil), :].astype(BF16)

        for bi, dil, sub in plan:
            def blk(u, carry, bi=bi, sub=sub):
                rows = pl.ds(pl.multiple_of(u * BLK, BLK), BLK)
                win = pl.ds(pl.multiple_of(u * BLK, BLK), 2 * BLK)
                sc = _dot(qd[bi, rows, :], kd[bi, win, :], NT) * scale
                sc = jnp.where(_band_mask(u, sub // BLK), sc, NEG)
                m = jnp.max(sc, axis=-1, keepdims=True)
                p = jnp.exp(sc - m)
                den = jnp.sum(p, axis=-1, keepdims=True)
                ob[bi, rows, :] = _dot((p / den).astype(BF16), vd[bi, win, :], NN)
                lb[bi, rows, :] = jnp.broadcast_to(m + jnp.log(den), (BLK, HEAD_DIM))
                return carry
            lax.fori_loop(0, nb // DIL_UNROLL, lambda i, c, blk=blk: [blk(i * DIL_UNROLL + s, c) for s in range(DIL_UNROLL)][-1], 0)

        for bi, dil, sub in plan[1:]:
            for r in range(dil):
                on[bi - 1, pl.ds(r, sub, stride=dil), :] = ob[bi, pl.ds(r * sub, sub), :]
                ln[bi - 1, pl.ds(r, sub, stride=dil), :] = lb[bi, pl.ds(r * sub, sub), :]

        def merge(i, carry):
            rows = pl.ds(pl.multiple_of(i * chunk, chunk), chunk)
            l0, l1, l2 = lb[0, rows, :], ln[0, rows, :], ln[1, rows, :]
            mx = jnp.maximum(jnp.maximum(l0, l1), l2)
            w0, w1, w2 = jnp.exp(l0 - mx), jnp.exp(l1 - mx), jnp.exp(l2 - mx)
            tot = w0 + w1 + w2
            o = (w0 / tot) * ob[0, rows, :] + (w1 / tot) * on[0, rows, :] + (w2 / tot) * on[1, rows, :]
            lse_ref[rows, :] = mx + jnp.log(tot)
            opre_ref[rows, :] = o
            omix_ref[rows, :] = _head_norm_fwd(o, g_ref[...]).astype(BF16)
            return carry
        lax.fori_loop(0, t // chunk, merge, 0)

    hs = n_heads
    col = lambda off: pl.BlockSpec((t, HEAD_DIM), lambda b, h: (b, off + h))
    return pl.pallas_call(
        body, name="dil_fwd", grid=(bl, hs),
        in_specs=[col(0), col(hs), col(2 * hs), pl.BlockSpec((1, HEAD_DIM), lambda b, h: (0, h))],
        out_specs=[col(0), col(0), pl.BlockSpec((t, HEAD_DIM), lambda b, h: (b * hs + h, 0))],
        out_shape=[S((n, 2 * hs * HEAD_DIM), BF16), S((n, 2 * hs * HEAD_DIM), F32), S((bl * hs * t, HEAD_DIM), F32)],
        scratch_shapes=[pltpu.VMEM((t, HEAD_DIM), F32), pltpu.VMEM((3, t, HEAD_DIM), BF16),
                        pltpu.VMEM((3, t + BLK, HEAD_DIM), BF16), pltpu.VMEM((3, t + BLK, HEAD_DIM), BF16),
                        pltpu.VMEM((3, t, HEAD_DIM), F32), pltpu.VMEM((3, t, HEAD_DIM), F32),
                        pltpu.VMEM((2, t, HEAD_DIM), F32), pltpu.VMEM((2, t, HEAD_DIM), F32)],
        compiler_params=_cp(2),
    )(proj, proj, proj, g_dil)


def _dil_bwd(proj, opre, lse, d_omix, g_dil, rc, rs1, rs2, bl, t, n_heads):
    n = bl * t
    nb = t // BLK
    scale = HEAD_DIM ** -0.5
    plan = _interleave_plan(t)
    chunk = _tile(t, 256)

    def body(q_ref, k_ref, v_ref, opre_ref, lse_ref, dy_ref, g_ref, c_ref, s1_ref, s2_ref, out_ref, dg_ref,
             stg, qd, kd, vd, dod, ldd, dqd, dkc, dkp, dvc, dvp, sk, sv):
        which = pl.program_id(2)

        @pl.when(jnp.logical_and(which == 0, pl.program_id(1) == 0))
        def _():
            dg_ref[...] = jnp.zeros_like(dg_ref)

        @pl.when(which == 0)
        def _():
            def prep(i, dg):
                rows = pl.ds(pl.multiple_of(i * chunk, chunk), chunk)
                o = opre_ref[rows, :]
                d_o, dg_rows = _head_norm_bwd(o, g_ref[...], dy_ref[rows, :])
                stg[rows, :] = d_o
                lane = lax.broadcasted_iota(jnp.int32, (chunk, HEAD_DIM), 1)
                ldd[0, rows, :] = jnp.where(lane < HEAD_DIM // 2, lse_ref[rows, :], jnp.sum(d_o * o, axis=-1, keepdims=True))
                return dg + jnp.sum(dg_rows, axis=0, keepdims=True)
            dg_ref[...] += lax.fori_loop(0, t // chunk, prep, jnp.zeros((1, HEAD_DIM), F32))

            dod[0] = stg[...].astype(BF16)
            for bi, dil, sub in plan[1:]:
                for r in range(dil):
                    dst = pl.ds(r * sub, sub)
                    dod[bi, dst, :] = stg[pl.ds(r, sub, stride=dil), :].astype(BF16)
                    ldd[bi, dst, :] = ldd[0, pl.ds(r, sub, stride=dil), :]
            for src, dst, pad in ((q_ref, qd, 0), (k_ref, kd, BLK), (v_ref, vd, BLK)):
                stg[...] = src[...].astype(F32)
                for bi, dil, sub in plan:
                    if pad:
                        dst[bi, pl.ds(0, BLK), :] = jnp.zeros((BLK, HEAD_DIM), BF16)
                    if dil == 1:
                        dst[bi, pl.ds(pad, t), :] = src[...]
                    else:
                        for r in range(dil):
                            dst[bi, pl.ds(pad + r * sub, sub), :] = stg[pl.ds(r, sub, stride=dil), :].astype(BF16)

            for bi, dil, sub in plan:
                def blk(u, carry, bi=bi, sub=sub):
                    rows = pl.ds(pl.multiple_of(u * BLK, BLK), BLK)
                    win = pl.ds(pl.multiple_of(u * BLK, BLK), 2 * BLK)
                    qb, kw, vw, dob = qd[bi, rows, :], kd[bi, win, :], vd[bi, win, :], dod[bi, rows, :]
                    sc = _dot(qb, kw, NT) * scale
                    stats = ldd[bi, rows, :]
                    p = jnp.where(_band_mask(u, sub // BLK), jnp.exp(sc - stats[:, :1]), 0.0)
                    dp = _dot(dob, vw, NT)
                    ds = (p * (dp - stats[:, HEAD_DIM // 2:HEAD_DIM // 2 + 1]) * scale).astype(BF16)
                    dqd[bi, rows, :] = _dot(ds, kw, NN)
                    dk_win = _dot(ds, qb, TN)
                    dv_win = _dot(p.astype(BF16), dob, TN)
                    dkp[bi, rows, :] = dk_win[:BLK]
                    dkc[bi, rows, :] = dk_win[BLK:]
                    dvp[bi, rows, :] = dv_win[:BLK]
                    dvc[bi, rows, :] = dv_win[BLK:]
                    return carry
                lax.fori_loop(0, nb // DIL_UNROLL, lambda i, c, blk=blk: [blk(i * DIL_UNROLL + s, c) for s in range(DIL_UNROLL)][-1], 0)

            for cur, prev, undo_rope, dst in ((dqd, None, True, out_ref), (dkc, dkp, True, sk), (dvc, dvp, False, sv)):
                def summed(bi, start, size, cur=cur, prev=prev):
                    v = cur[bi, pl.ds(start, size), :]
                    if prev is None:
                        return v
                    if start + size < t:
                        return v + prev[bi, pl.ds(start + BLK, size), :]
                    if size == BLK:
                        return v
                    return v + jnp.concatenate([prev[bi, pl.ds(start + BLK, size - BLK), :], jnp.zeros((BLK, HEAD_DIM), F32)], axis=0)
                stg[...] = summed(0, 0, t)
                for bi, dil, sub in plan[1:]:
                    for r in range(dil):
                        stg[pl.ds(r, sub, stride=dil), :] += summed(bi, r * sub, sub)
                if undo_rope:
                    dst[...] = _rope_bwd(stg[...], c_ref[...], s1_ref[...], s2_ref[...]).astype(BF16)
                else:
                    dst[...] = stg[...].astype(BF16)

        @pl.when(which == 1)
        def _():
            out_ref[...] = sk[...]

        @pl.when(which == 2)
        def _():
            out_ref[...] = sv[...]

    hs = n_heads
    col = lambda off: pl.BlockSpec((t, HEAD_DIM), lambda h, b, w: (b, off + h))
    per_head = pl.BlockSpec((t, HEAD_DIM), lambda h, b, w: (b * hs + h, 0))
    tab = pl.BlockSpec((t, HEAD_DIM), lambda h, b, w: (b, 0))
    gvec = pl.BlockSpec((1, HEAD_DIM), lambda h, b, w: (0, h))
    tb = (t, HEAD_DIM)
    tp = (t + BLK, HEAD_DIM)
    return pl.pallas_call(
        body, name="dil_bwd", grid=(hs, bl, 3),
        in_specs=[col(0), col(hs), col(2 * hs), col(0), per_head, col(0), gvec, tab, tab, tab],
        out_specs=[pl.BlockSpec((t, HEAD_DIM), lambda h, b, w: (b, w * hs + h)), gvec],
        out_shape=[S((n, 6 * hs * HEAD_DIM), BF16), S((1, hs * HEAD_DIM), F32)],
        scratch_shapes=[pltpu.VMEM(tb, F32), pltpu.VMEM((3,) + tb, BF16), pltpu.VMEM((3,) + tp, BF16), pltpu.VMEM((3,) + tp, BF16),
                        pltpu.VMEM((3,) + tb, BF16), pltpu.VMEM((3,) + tb, F32), pltpu.VMEM((3,) + tb, F32),
                        pltpu.VMEM((3,) + tb, F32), pltpu.VMEM((3,) + tb, F32), pltpu.VMEM((3,) + tb, F32), pltpu.VMEM((3,) + tb, F32),
                        pltpu.VMEM(tb, BF16), pltpu.VMEM(tb, BF16)],
        compiler_params=_cp(3),
    )(proj, proj, proj, opre, lse, d_omix, g_dil, rc, rs1, rs2)


def _sb_tile(qb, kb, scale, shift):
    z = _dot(qb, kb, NT) * scale
    tl = jnp.log(1.0 + jnp.exp(-jnp.abs(z)))
    log_not = -(jnp.maximum(z, 0.0) + tl)
    log_beta = z + log_not
    strict = None
    if shift is not None:
        row = lax.broadcasted_iota(jnp.int32, (SBQ, SBK), 0)
        col = lax.broadcasted_iota(jnp.int32, (SBQ, SBK), 1)
        strict = col + shift < row
        log_not = jnp.where(strict, log_not, 0.0)
    return log_not, log_beta, strict


def _tri(cmp):
    row = lax.broadcasted_iota(jnp.int32, (SBK, SBK), 0)
    col = lax.broadcasted_iota(jnp.int32, (SBK, SBK), 1)
    return jnp.where(cmp(row, col), 1.0, 0.0).astype(BF16)


SB_EDGE = tuple(range(0, SBQ, SBK))


def _sb_fwd(proj, g_sb, omix_in, opre_in, bl, t, n_heads):
    nb = t // SBQ
    per_q = SBQ // SBK
    scale = HEAD_DIM ** -0.5

    def body(q_ref, k_ref, v_ref, g_ref, _omix_in, _opre_in, omix_ref, opre_ref, lt_ref):
        later = _tri(lambda r, c: r > c)

        def q_block(qi, carry):
            rows = pl.ds(pl.multiple_of(qi * SBQ, SBQ), SBQ)
            qb = q_ref[rows, :]

            def tile(kj, st, shift):
                run, acc = st
                krows = pl.ds(pl.multiple_of(kj * SBK, SBK), SBK)
                log_not, log_beta, strict = _sb_tile(qb, k_ref[krows, :], scale, shift)
                a = jnp.exp(log_beta + _scan_cols(log_not, later, 2) + run)
                if shift is not None:
                    a = jnp.where(strict, a, 0.0)
                return run + jnp.sum(log_not, axis=-1, keepdims=True), acc + _dot(a.astype(BF16), v_ref[krows, :], NN)

            st = (jnp.zeros((SBQ, 1), F32), jnp.zeros((SBQ, HEAD_DIM), F32))
            for shift in reversed(SB_EDGE):
                st = tile(qi * per_q + shift // SBK, st, shift)
            run, acc = lax.fori_loop(0, qi * per_q, lambda it, st: tile(qi * per_q - 1 - it, st, None), st)
            lt_ref[rows, :] = jnp.broadcast_to(run, (SBQ, HEAD_DIM))
            opre_ref[rows, :] = acc
            omix_ref[rows, :] = _head_norm_fwd(acc, g_ref[...]).astype(BF16)
            return carry

        lax.fori_loop(0, nb, q_block, 0)

    hs = n_heads
    col = lambda off: pl.BlockSpec((t, HEAD_DIM), lambda b, h: (b, off + h))
    return pl.pallas_call(
        body, name="sb_fwd", grid=(bl, hs),
        in_specs=[col(3 * hs), col(4 * hs), col(5 * hs), pl.BlockSpec((1, HEAD_DIM), lambda b, h: (0, h)), HBM_SPEC, HBM_SPEC],
        out_specs=[col(hs), col(hs), pl.BlockSpec((t, HEAD_DIM), lambda b, h: (b * hs + h, 0))],
        out_shape=[S(omix_in.shape, BF16), S(opre_in.shape, F32), S((bl * hs * t, HEAD_DIM), F32)],
        input_output_aliases={4: 0, 5: 1}, compiler_params=_cp(2),
    )(proj, proj, proj, g_sb, omix_in, opre_in)


def _sb_bwd(proj, opre, ltot, d_omix, g_sb, dproj_in, bl, t, n_heads):
    nb = t // SBQ
    per_q = SBQ // SBK
    scale = HEAD_DIM ** -0.5

    def body(q_ref, k_ref, v_ref, opre_ref, lt_ref, dy_ref, g_ref, _dproj_in, out_ref, dg_ref, dq, dk, dv):
        which = pl.program_id(2)

        @pl.when(jnp.logical_and(which == 0, pl.program_id(1) == 0))
        def _():
            dg_ref[...] = jnp.zeros_like(dg_ref)

        @pl.when(which == 0)
        def _():
            upto = _tri(lambda r, c: r <= c)
            before = _tri(lambda r, c: r < c)
            dk[...] = jnp.zeros_like(dk)
            dv[...] = jnp.zeros_like(dv)

            def q_block(qi, dg):
                rows = pl.ds(pl.multiple_of(qi * SBQ, SBQ), SBQ)
                qb = q_ref[rows, :]
                o = opre_ref[rows, :]
                d_o, dg_rows = _head_norm_bwd(o, g_ref[...], dy_ref[rows, :])
                dob = d_o.astype(BF16)
                lt = lt_ref[rows, :][:, :1]

                def tile(kj, st, shift):
                    run, grun, dq_acc = st
                    krows = pl.ds(pl.multiple_of(kj * SBK, SBK), SBK)
                    kb, vb = k_ref[krows, :], v_ref[krows, :]
                    log_not, log_beta, strict = _sb_tile(qb, kb, scale, shift)
                    excl = lt - (run + _scan_cols(log_not, upto, 2))
                    a = jnp.exp(log_beta + excl)
                    if shift is not None:
                        a = jnp.where(strict, a, 0.0)
                    g_a = a * _dot(dob, vb, NT)
                    g_before = grun + _scan_cols(g_a, before, 1)
                    dz = g_a - (g_a + g_before) * jnp.exp(log_beta)
                    if shift is not None:
                        dz = jnp.where(strict, dz, 0.0)
                    dzb = dz.astype(BF16)
                    dk[krows, :] += _dot(dzb, qb, TN)
                    dv[krows, :] += _dot(a.astype(BF16), dob, TN)
                    return (run + jnp.sum(log_not, axis=-1, keepdims=True), grun + jnp.sum(g_a, axis=-1, keepdims=True),
                            dq_acc + _dot(dzb, kb, NN))

                zero_col = jnp.zeros((SBQ, 1), F32)
                st = lax.fori_loop(0, qi * per_q, lambda kj, st: tile(kj, st, None),
                                   (zero_col, zero_col, jnp.zeros((SBQ, HEAD_DIM), F32)))
                for shift in SB_EDGE:
                    st = tile(qi * per_q + shift // SBK, st, shift)
                dq_acc = st[2]
                dq[rows, :] = (dq_acc * scale).astype(BF16)
                return dg + jnp.sum(dg_rows, axis=0, keepdims=True)

            dg_ref[...] += lax.fori_loop(0, nb, q_block, jnp.zeros((1, HEAD_DIM), F32))
            out_ref[...] = dq[...]

        @pl.when(which == 1)
        def _():
            out_ref[...] = (dk[...] * scale).astype(BF16)

        @pl.when(which == 2)
        def _():
            out_ref[...] = dv[...].astype(BF16)

    hs = n_heads
    col = lambda off: pl.BlockSpec((t, HEAD_DIM), lambda h, b, w: (b, off + h))
    per_head = pl.BlockSpec((t, HEAD_DIM), lambda h, b, w: (b * hs + h, 0))
    gvec = pl.BlockSpec((1, HEAD_DIM), lambda h, b, w: (0, h))
    tb = (t, HEAD_DIM)
    return pl.pallas_call(
        body, name="sb_bwd", grid=(hs, bl, 3),
        in_specs=[col(3 * hs), col(4 * hs), col(5 * hs), col(hs), per_head, col(hs), gvec, HBM_SPEC],
        out_specs=[pl.BlockSpec((t, HEAD_DIM), lambda h, b, w: (b, (3 + w) * hs + h)), gvec],
        out_shape=[S(dproj_in.shape, BF16), S((1, hs * HEAD_DIM), F32)],
        scratch_shapes=[pltpu.VMEM(tb, BF16), pltpu.VMEM(tb, F32), pltpu.VMEM(tb, F32)],
        input_output_aliases={7: 0}, compiler_params=_cp(3),
    )(proj, proj, proj, opre, ltot, d_omix, g_sb, dproj_in)


def _grad_exchange(grads, after):
    k_w = len(grads)

    def body(*refs):
        ins, outs = refs[:k_w], refs[k_w + 1:2 * k_w + 1]
        send_sems, recv_sems, local_sems = refs[2 * k_w + 1:]
        local, remote = _exchange_copies(ins, outs, send_sems, recv_sems, local_sems, False, ALL_PEERS)
        for cp in local + remote:
            cp.start()
        for cp in remote:
            cp.wait_send()
            cp.wait_recv()
        for cp in local:
            cp.wait()

    n_sem = (N_DEV - 1) * k_w
    return pl.pallas_call(
        body, name="grad_exchange", in_specs=[HBM_SPEC] * k_w + [pl.BlockSpec(memory_space=pl.ANY)], out_specs=[HBM_SPEC] * k_w,
        out_shape=[S(g.shape, g.dtype) for g in grads],
        scratch_shapes=[pltpu.SemaphoreType.DMA((n_sem,)), pltpu.SemaphoreType.DMA((n_sem,)), pltpu.SemaphoreType.DMA((k_w,))],
    )(*grads, after)


def _peer(x, y, c, k):
    px, py, pc = x ^ (k >> 2), y ^ ((k >> 1) & 1), c ^ (k & 1)
    return (px, py, pc), 4 * px + 2 * py + pc


ALL_PEERS = tuple(range(1, N_DEV))
SAME_CORE_PEERS = (2, 4, 6)


def _exchange_copies(srcs, lands, send_sems, recv_sems, local_sems, gather, peers):
    x, y, c = lax.axis_index("x"), lax.axis_index("y"), lax.axis_index("c")
    my_slot = 4 * x + 2 * y + c
    local, remote = [], []
    for w, (src, land) in enumerate(zip(srcs, lands)):
        local.append(pltpu.make_async_copy(src if gather else src.at[my_slot], land.at[my_slot], local_sems.at[w]))
        for i, k in enumerate(peers):
            peer, peer_slot = _peer(x, y, c, k)
            remote.append(pltpu.make_async_remote_copy(
                src_ref=src if gather else src.at[peer_slot], dst_ref=land.at[my_slot],
                send_sem=send_sems.at[w * len(peers) + i], recv_sem=recv_sems.at[w * len(peers) + i],
                device_id=peer, device_id_type=MESH))
    return local, remote


def _exchange_start(name, srcs, gather, after, peers=ALL_PEERS):
    k_w = len(srcs)
    land_shapes = [((N_DEV,) + s.shape) if gather else s.shape for s in srcs]

    def body(*refs):
        src_refs, land_refs = refs[:k_w], refs[k_w:2 * k_w]
        send_sems, recv_sems, local_sems = refs[2 * k_w + 1:2 * k_w + 4]
        token = refs[-1]
        local, remote = _exchange_copies(src_refs, land_refs, send_sems, recv_sems, local_sems, gather, peers)
        for cp in local + remote:
            cp.start()
        token[...] = jnp.zeros_like(token)

    n_sem = len(peers) * k_w
    hbm = lambda a: pltpu.with_memory_space_constraint(a, pltpu.HBM)
    outs = pl.pallas_call(
        body, name=name,
        in_specs=[HBM_SPEC] * (2 * k_w) + [pl.BlockSpec(memory_space=pl.ANY)],
        out_shape=(pltpu.SemaphoreType.DMA((n_sem,)), pltpu.SemaphoreType.DMA((n_sem,)), pltpu.SemaphoreType.DMA((k_w,)),
                   *[pltpu.HBM(s.shape, s.dtype) for s in srcs], *[pltpu.HBM(ls, s.dtype) for ls, s in zip(land_shapes, srcs)],
                   S((8, HEAD_DIM), F32)),
        out_specs=(SEM_SPEC, SEM_SPEC, SEM_SPEC, *[HBM_SPEC] * (2 * k_w), pl.BlockSpec(memory_space=pltpu.VMEM)),
        input_output_aliases={i: 3 + i for i in range(2 * k_w)},
        compiler_params=pltpu.CompilerParams(has_side_effects=SIDE_EFFECT),
    )(*[hbm(s) for s in srcs], *[hbm(lax.empty(ls, s.dtype)) for ls, s in zip(land_shapes, srcs)], after)
    return dict(sems=outs[:3], srcs=outs[3:3 + k_w], lands=outs[3 + k_w:3 + 2 * k_w], token_block=outs[-1], token=outs[-1][0, 0], gather=gather, peers=peers)


def _exchange_wait(name, handle, after):
    k_w = len(handle["srcs"])
    gather = handle["gather"]
    afters = after if isinstance(after, tuple) else (after,)

    def body(*refs):
        src_refs, land_refs = refs[:k_w], refs[k_w:2 * k_w]
        send_sems, recv_sems, local_sems = refs[2 * k_w:2 * k_w + 3]
        local, remote = _exchange_copies(src_refs, land_refs, send_sems, recv_sems, local_sems, gather, handle["peers"])
        for cp in local:
            cp.wait()
        for cp in remote:
            cp.wait_send()
            cp.wait_recv()

    outs = pl.pallas_call(
        body, name=name,
        in_specs=[HBM_SPEC] * (2 * k_w) + [SEM_SPEC] * 3 + [pl.BlockSpec(memory_space=pl.ANY)] * len(afters),
        out_shape=tuple(pltpu.HBM(a.shape, a.dtype) for a in (*handle["srcs"], *handle["lands"])),
        out_specs=tuple([HBM_SPEC] * (2 * k_w)),
        input_output_aliases={i: i for i in range(2 * k_w)},
        compiler_params=pltpu.CompilerParams(has_side_effects=SIDE_EFFECT),
    )(*handle["srcs"], *handle["lands"], *handle["sems"], *afters)
    return outs[k_w:]


CHIPS = ((0, 0), (0, 1), (1, 0), (1, 1))


def _swap_copies(lands, send_sems, recv_sems):
    x, y, c = lax.axis_index("x"), lax.axis_index("y"), lax.axis_index("c")
    copies = []
    for w, land in enumerate(lands):
        for i, (px, py) in enumerate(CHIPS):
            mine = land.at[4 * px + 2 * py + c]
            copies.append(pltpu.make_async_remote_copy(
                src_ref=mine, dst_ref=mine, send_sem=send_sems.at[w * len(CHIPS) + i], recv_sem=recv_sems.at[w * len(CHIPS) + i],
                device_id=(x, y, 1 - c), device_id_type=MESH))
    return copies


def _swap_start(name, lands, after):
    k_w = len(lands)

    def body(*refs):
        land_refs = refs[:k_w]
        send_sems, recv_sems = refs[k_w + 1:k_w + 3]
        for cp in _swap_copies(land_refs, send_sems, recv_sems):
            cp.start()
        refs[-1][...] = jnp.zeros_like(refs[-1])

    n_sem = len(CHIPS) * k_w
    outs = pl.pallas_call(
        body, name=name, in_specs=[HBM_SPEC] * k_w + [pl.BlockSpec(memory_space=pl.ANY)],
        out_shape=(pltpu.SemaphoreType.DMA((n_sem,)), pltpu.SemaphoreType.DMA((n_sem,)),
                   *[pltpu.HBM(a.shape, a.dtype) for a in lands], S((8, HEAD_DIM), F32)),
        out_specs=(SEM_SPEC, SEM_SPEC, *[HBM_SPEC] * k_w, pl.BlockSpec(memory_space=pltpu.VMEM)),
        input_output_aliases={i: 2 + i for i in range(k_w)},
        compiler_params=pltpu.CompilerParams(has_side_effects=SIDE_EFFECT),
    )(*lands, after)
    return dict(sems=outs[:2], lands=outs[2:2 + k_w], token_block=outs[-1])


def _swap_wait(name, handle, after):
    k_w = len(handle["lands"])

    def body(*refs):
        for cp in _swap_copies(refs[:k_w], refs[k_w], refs[k_w + 1]):
            cp.wait_send()
            cp.wait_recv()

    return pl.pallas_call(
        body, name=name, in_specs=[HBM_SPEC] * k_w + [SEM_SPEC] * 2 + [pl.BlockSpec(memory_space=pl.ANY)],
        out_shape=tuple(pltpu.HBM(a.shape, a.dtype) for a in handle["lands"]), out_specs=tuple([HBM_SPEC] * k_w),
        input_output_aliases={i: i for i in range(k_w)},
        compiler_params=pltpu.CompilerParams(has_side_effects=SIDE_EFFECT),
    )(*handle["lands"], *handle["sems"], after)


def _adamw(name, parts, w, m, v):
    r, c = w.shape
    tr = _tile(r, max(16, (1 << 19) // c // 16 * 16))

    def body(p_ref, w_ref, m_ref, v_ref, g_ref, d_ref, nm_ref, nv_ref):
        g = p_ref[0].astype(F32)
        for s in range(1, N_DEV):
            g = g + p_ref[s].astype(F32)
        m_new = ADAM_B1 * m_ref[...] + (1.0 - ADAM_B1) * g
        v_new = ADAM_B2 * v_ref[...] + (1.0 - ADAM_B2) * jnp.square(g)
        m_hat = m_new / (1.0 - ADAM_B1 ** ADAM_STEP)
        v_hat = v_new / (1.0 - ADAM_B2 ** ADAM_STEP)
        g_ref[...] = g
        d_ref[...] = -ADAM_LR * (m_hat / (jnp.sqrt(v_hat) + ADAM_EPS) + ADAM_WD * w_ref[...])
        nm_ref[...] = m_new
        nv_ref[...] = v_new

    blk = pl.BlockSpec((tr, c), lambda i: (i, 0))
    return pl.pallas_call(body, name=name, grid=(r // tr,), in_specs=[pl.BlockSpec((N_DEV, tr, c), lambda i: (0, i, 0)), blk, blk, blk],
                          out_specs=[blk] * 4, out_shape=[S((r, c), F32)] * 4, compiler_params=_cp(1))(parts, w, m, v)


def _rope_tables(positions):
    inv_freq = jnp.power(jnp.float32(ROPE_THETA), -jnp.arange(ROPE_HALF, dtype=F32) / ROPE_HALF)
    ang = positions.astype(F32).reshape(-1, 1) * inv_freq
    cos, sin = jnp.cos(ang), jnp.sin(ang)
    n = ang.shape[0]
    rest = HEAD_DIM - 2 * ROPE_HALF
    zeros = jnp.zeros((n, ROPE_HALF), F32)
    c = jnp.concatenate([cos, cos, jnp.ones((n, rest), F32)], axis=1)
    s1 = jnp.concatenate([zeros, sin, jnp.zeros((n, rest), F32)], axis=1)
    s2 = jnp.concatenate([-sin, zeros, jnp.zeros((n, rest), F32)], axis=1)
    return c, s1, s2


def kernel(x, positions, norm_mix_g, w_in, norm_out_dil_g, norm_out_sb_g, w_out, norm_ffn_g, w_gate, w_up, w_down, norm_final_g, loss_target, m_norm_mix_g, m_w_in, m_norm_out_dil_g, m_norm_out_sb_g, m_w_out, m_norm_ffn_g, m_w_gate, m_w_up, m_w_down, m_norm_final_g, v_norm_mix_g, v_w_in, v_norm_out_dil_g, v_norm_out_sb_g, v_w_out, v_norm_ffn_g, v_w_gate, v_w_up, v_w_down, v_norm_final_g):
    bl, t, d = x.shape
    n = bl * t
    hs = d // (2 * HEAD_DIM)
    x2 = x.reshape(n, d)
    target = loss_target.reshape(n, d)
    g_final = norm_final_g.reshape(1, d)
    rc, rs1, rs2 = _rope_tables(positions)

    ex_win = _exchange_start("win_gather_start", [w_in[0].astype(BF16)], True, rs1, SAME_CORE_PEERS)
    wout_b, wgt_b, wut_b, wd_b = (w_out[0].astype(BF16), jnp.swapaxes(w_gate[0], 0, 1).astype(BF16),
                                  jnp.swapaxes(w_up[0], 0, 1).astype(BF16), w_down[0].astype(BF16))
    hn1 = _rms_fwd("rms_mix_fwd", x2, norm_mix_g + ex_win["token"])
    (win_half,) = _exchange_wait("win_gather_wait", ex_win, (hn1, wout_b, wgt_b, wut_b, wd_b))
    sw_win = _swap_start("win_swap_start", [win_half], hn1)
    (win_all,) = _swap_wait("win_swap_wait", sw_win, sw_win["token_block"])
    ex_wout = _exchange_start("wout_gather_start", [wout_b], True, win_all, SAME_CORE_PEERS)
    tr = lambda a: jnp.swapaxes(a[0], 0, 1)
    ex_wgu = _exchange_start("wgu_gather_start", [wgt_b, wut_b], True, ex_wout["token_block"], SAME_CORE_PEERS)
    ex_wd = _exchange_start("wd_gather_start", [wd_b], True, ex_wgu["token_block"], SAME_CORE_PEERS)
    rc = rc + ex_wd["token"]

    proj = _proj_fwd(hn1, win_all, rc, rs1, rs2, 2 * hs)
    omix, opre, lse = _dil_fwd(proj, norm_out_dil_g, bl, t, hs)
    omix, opre, ltot = _sb_fwd(proj, norm_out_sb_g, omix, opre, bl, t, hs)
    (wout_half,) = _exchange_wait("wout_gather_wait", ex_wout, ltot)
    wg_half, wu_half = _exchange_wait("wgu_gather_wait", ex_wgu, wout_half)
    sw_wout = _swap_start("wout_swap_start", [wout_half], wg_half)
    sw_wgu = _swap_start("wgu_swap_start", [wg_half, wu_half], sw_wout["token_block"])
    (wout_all,) = _swap_wait("wout_swap_wait", sw_wout, sw_wgu["token_block"])
    wout_full = wout_all.reshape(d, d)
    h1 = _dense_res("out_fwd", omix, wout_full, x2, NN)
    hn2 = _rms_fwd("rms_ffn_fwd", h1, norm_ffn_g)
    wg_all, wu_all = _swap_wait("wgu_swap_wait", sw_wgu, hn2)
    gate = _gate_fwd(hn2, wg_all)
    (wd_half,) = _exchange_wait("wd_gather_wait", ex_wd, gate)
    sw_wd = _swap_start("wd_swap_start", [wd_half], gate)
    up, act = _up_fwd(hn2, wu_all, gate, sw_wd["token_block"])
    (wd_all,) = _swap_wait("wd_swap_wait", sw_wd, act)
    h2 = _rows_fwd("down_fwd", (act,), (wd_all,), h1)
    dh2, dh2b, dg_final, sse = _final_loss(h2, g_final, target)
    loss = lax.psum(sse[0, 0], ("x", "y", "c")) * (0.5 / d)

    dgate, dup = _dact_bwd(dh2b, wd_all, gate, up)
    dwd = _dw_rows_bwd("dwd_bwd", act, dh2b)
    ex_dwd = _exchange_start("dwd_exchange_start", [dwd], False, dgate)
    dwg = _dw_rows_bwd("dwg_bwd", dgate, hn2, ex_dwd["token_block"])
    dwu = _dw_rows_bwd("dwu_bwd", dup, hn2)
    ex_dwgu = _exchange_start("dwgu_exchange_start", [dwg, dwu], False, ex_dwd["token_block"])
    dhn2 = _rows_fwd("dhn2_bwd", (dgate, dup), (wg_all, wu_all), None, ex_dwgu["token_block"])
    dh1, dh1b, dg_ffn = _rms_bwd("rms_ffn_bwd", h1, norm_ffn_g, dhn2, dh2)
    d_omix = _dense_res("domix_bwd", dh1b, wout_full, None, NT)
    dwout = _tn_full("dwout_bwd", omix, dh1b).reshape(N_DEV, d // N_DEV, d)
    ex_dwout = _exchange_start("dwout_exchange_start", [dwout], False, d_omix)
    dproj, dg_dil = _dil_bwd(proj, opre, lse, d_omix, norm_out_dil_g + ex_dwout["token"], rc, rs1, rs2, bl, t, hs)
    dproj, dg_sb = _sb_bwd(proj, opre, ltot, d_omix, norm_out_sb_g, dproj, bl, t, hs)
    dwin = _dwin_bwd(hn1, dproj)
    ex_dwin = _exchange_start("dwin_exchange_start", [dwin], False, dproj)
    dhn1 = _dhn_from_shards("dhn1_bwd", (dproj,), (win_all,), True, ex_dwin["token_block"])
    dx, _, dg_mix = _rms_bwd("rms_mix_bwd", x2, norm_mix_g, dhn1, dh1)

    gains = [norm_mix_g, norm_out_dil_g, norm_out_sb_g, norm_ffn_g, g_final]
    m_gains = [m_norm_mix_g, m_norm_out_dil_g, m_norm_out_sb_g, m_norm_ffn_g, m_norm_final_g.reshape(1, d)]
    v_gains = [v_norm_mix_g, v_norm_out_dil_g, v_norm_out_sb_g, v_norm_ffn_g, v_norm_final_g.reshape(1, d)]
    dg_vec = jnp.concatenate([dg_mix, dg_dil, dg_sb, dg_ffn, dg_final], axis=1)
    dg_all = jnp.broadcast_to(dg_vec[None], (N_DEV,) + dg_vec.shape)

    out_w = {}
    (rwd,) = _exchange_wait("dwd_exchange_wait", ex_dwd, dx)
    out_w["w_down"] = _adamw("adamw_w_down", rwd, w_down[0], m_w_down[0], v_w_down[0])
    rwg, rwu = _exchange_wait("dwgu_exchange_wait", ex_dwgu, out_w["w_down"][0])
    gate_t = _adamw("adamw_w_gate", rwg, tr(w_gate), tr(m_w_gate), tr(v_w_gate))
    up_t = _adamw("adamw_w_up", rwu, tr(w_up), tr(m_w_up), tr(v_w_up))
    out_w["w_gate"] = [jnp.swapaxes(o, 0, 1) for o in gate_t]
    out_w["w_up"] = [jnp.swapaxes(o, 0, 1) for o in up_t]
    (rwout,) = _exchange_wait("dwout_exchange_wait", ex_dwout, up_t[0])
    out_w["w_out"] = _adamw("adamw_w_out", rwout, w_out[0], m_w_out[0], v_w_out[0])
    (rwin,) = _exchange_wait("dwin_exchange_wait", ex_dwin, out_w["w_out"][0])
    out_w["w_in"] = _adamw("adamw_w_in", rwin, w_in[0], m_w_in[0], v_w_in[0])
    (rg,) = _grad_exchange([dg_all], out_w["w_in"][0])
    out_w = {name: [o[None] for o in outs] for name, outs in out_w.items()}
    cat = lambda vs: jnp.concatenate(vs, axis=1)
    gain_out = _adamw("adamw_gains", rg, cat(gains), cat(m_gains), cat(v_gains))
    widths = [d, d // 2, d // 2, d]
    cuts = [sum(widths[:i + 1]) for i in range(4)]
    gain_split = [jnp.split(o, cuts, axis=1) for o in gain_out]

    def ordered(kind):
        gs = gain_split[kind]
        return (gs[0], out_w["w_in"][kind], gs[1], gs[2], out_w["w_out"][kind], gs[3], out_w["w_gate"][kind],
                out_w["w_up"][kind], out_w["w_down"][kind], gs[4].reshape(d))

    return (loss, dx.reshape(bl, t, d), *ordered(0), *ordered(1), *ordered(2), *ordered(3))
```
